```python
import math
import jax, jax.numpy as jnp
from jax import lax
import numpy as np

D_MODEL = 1024
BATCH = 32
SEQ = 256
DEPTH = 2
DEC_BATCH = 8
DEC_SEQ = 1024
PAST_LEN = 256

GRID_W = 64
CHUNK = 128
Q_BLOCK = 128
EPS = 1e-6
ROPE_BASE = 10000.0
W_A = 256
G_A = 4
DG_A = W_A // G_A
H_B = 4
DK_B = 64
DV_B = 64
W_B = H_B * DV_B
H_C = 4
HD_C = 64
DV_C = 2 * HD_C
W_C = H_C * DV_C
ROPE_PAIRS = HD_C // 4
MIX_W = W_A + W_B + W_C
IN_A = 2 * W_A
IN_B = 2 * H_B * DK_B + 2 * W_B
IN_C = 2 * H_C * 2 * HD_C + W_C
IN_W = IN_A + IN_B + IN_C
D_FF = 2816
CONV_W = 3

kernel_name = 'hybrid_dit_prefix_step'


def rms_norm(x, g):
    xf = x.astype(jnp.float32)
    y = xf * lax.rsqrt(jnp.mean(xf * xf, axis=-1, keepdims=True) + EPS)
    return (y * g.astype(jnp.float32)).astype(x.dtype)


def adaln(cond2d, w_mod_l, b_mod_l):
    m = jax.nn.silu(cond2d) @ w_mod_l + b_mod_l
    return jnp.split(m[:, None, :], 6, axis=-1)


def axial_rope_angles(L):
    rows = L // GRID_W
    t_row = jnp.repeat(jnp.arange(rows, dtype=jnp.float32), GRID_W)
    t_col = jnp.tile(jnp.arange(GRID_W, dtype=jnp.float32), rows)
    inv = ROPE_BASE ** (-jnp.arange(ROPE_PAIRS, dtype=jnp.float32) / ROPE_PAIRS)
    return t_row[:, None] * inv, t_col[:, None] * inv


def _rotate(x, ang):
    x1, x2 = x[..., :ROPE_PAIRS], x[..., ROPE_PAIRS:]
    cos = jnp.cos(ang)[:, None, None, :]
    sin = jnp.sin(ang)[:, None, None, :]
    return jnp.concatenate([x1 * cos - x2 * sin, x1 * sin + x2 * cos], axis=-1)


def apply_axial_rope(x, ang_row, ang_col):
    xf = x.astype(jnp.float32)
    half = HD_C // 2
    out = jnp.concatenate([_rotate(xf[..., :half], ang_row), _rotate(xf[..., half:], ang_col)], axis=-1)
    return out.astype(x.dtype)


def chunk_sgu(z, norm_g, w_s, b_s):
    B, L, _ = z.shape
    u, v = jnp.split(jax.nn.gelu(z), 2, axis=-1)
    v = rms_norm(v, norm_g)
    vg = v.reshape(B, L // CHUNK, CHUNK, G_A, DG_A)
    s = jnp.einsum('gpq,bnqgc->bnpgc', w_s, vg) + b_s.T[None, None, :, :, None]
    return u * s.reshape(B, L, W_A)


def retention_chunked(q, k, v, log_gamma, r0):
    B, L, H, _ = q.shape
    dv = v.shape[-1]
    n = L // CHUNK

    def chunks(t):
        return t.astype(jnp.float32).reshape(B, n, CHUNK, H, t.shape[-1]).transpose(1, 0, 3, 2, 4)

    pos = jnp.arange(CHUNK, dtype=jnp.float32)
    dist = pos[:, None] - pos[None, :]
    lg = log_gamma[:, None, None]
    decay_mat = jnp.where(dist >= 0, jnp.exp(lg * jnp.maximum(dist, 0.0)), 0.0)
    q_decay = jnp.exp(log_gamma[:, None] * (pos + 1.0))[:, :, None]
    k_decay = jnp.exp(log_gamma[:, None] * (CHUNK - 1.0 - pos))[:, :, None]
    chunk_decay = jnp.exp(log_gamma * CHUNK)[:, None, None]

    def step(r, inp):
        qc, kc, vc = inp
        inner = jnp.einsum('bhqd,bhkd->bhqk', qc, kc) * decay_mat
        o = jnp.einsum('bhqk,bhke->bhqe', inner, vc) + jnp.einsum('bhqd,bhde->bhqe', qc, r) * q_decay
        r_new = r * chunk_decay + jnp.einsum('bhkd,bhke->bhde', kc * k_decay, vc)
        return r_new, o

    r_fin, o = lax.scan(step, r0.astype(jnp.float32), (chunks(q), chunks(k), chunks(v)))
    o = o.transpose(1, 0, 3, 2, 4).reshape(B, L, H, dv)
    return o, r_fin


def bidirectional_retention(q, k, v, logit_f, logit_b, r0_f, r0_b):
    lg_f = jax.nn.log_sigmoid(logit_f.astype(jnp.float32))
    lg_b = jax.nn.log_sigmoid(logit_b.astype(jnp.float32))
    o_f, r_f = retention_chunked(q, k, v, lg_f, r0_f)
    o_b, r_b = retention_chunked(q[:, ::-1], k[:, ::-1], v[:, ::-1], lg_b, r0_b)
    return o_f + o_b[:, ::-1], r_f, r_b


def diff_attention(q, k, v, lam, lam_init, norm_g):
    B, Lq = q.shape[:2]
    nb = Lq // Q_BLOCK
    qb = q.astype(jnp.float32).reshape(B, nb, Q_BLOCK, H_C, 2, HD_C).transpose(1, 0, 2, 3, 4, 5)
    kf = k.astype(jnp.float32)
    vf = v.astype(jnp.float32)
    scale = HD_C ** -0.5

    def block(qq):
        s = jnp.einsum('bqhid,bkhid->bhiqk', qq, kf) * scale
        p = jax.nn.softmax(s, axis=-1)
        a = p[:, :, 0] - lam * p[:, :, 1]
        return jnp.einsum('bhqk,bkhe->bqhe', a, vf)

    o = lax.map(block, qb)
    o = o.transpose(1, 0, 2, 3, 4).reshape(B, Lq, H_C, DV_C)
    o = rms_norm(o, norm_g) * (1.0 - lam_init)
    return o.astype(q.dtype).reshape(B, Lq, W_C)


def conv_ffn(h, lp):
    up = h @ lp['ffn_up']
    pad = jnp.pad(up, ((0, 0), (1, 1), (0, 0)))
    w = lp['ffn_conv']
    y = pad[:, :-2] * w[0] + pad[:, 1:-1] * w[1] + pad[:, 2:] * w[2] + lp['ffn_conv_b']
    a, b = jnp.split(y, 2, axis=-1)
    return (jax.nn.silu(a) * b) @ lp['ffn_down']


def mixing(h, lp, lam_init, cache):
    B, L, _ = h.shape
    z = h @ lp['w_in']
    za, zb, zc = jnp.split(z, [IN_A, IN_A + IN_B], axis=-1)
    y_a = chunk_sgu(za, lp['sgu_norm'], lp['sgu_w'], lp['sgu_b'])
    qb, kb, vb, gb = jnp.split(zb, 4, axis=-1)
    qb = qb.reshape(B, L, H_B, DK_B)
    kb = kb.reshape(B, L, H_B, DK_B) * (DK_B ** -0.5)
    vb = vb.reshape(B, L, H_B, DV_B)
    if cache is None:
        r0_f = jnp.zeros((B, H_B, DK_B, DV_B), jnp.float32)
        r0_b = r0_f
    else:
        r0_f, r0_b = cache[2], cache[3]
    o_ret, r_f, r_b = bidirectional_retention(qb, kb, vb, lp['ret_logit_fwd'], lp['ret_logit_bwd'], r0_f, r0_b)
    o_ret = rms_norm(o_ret.astype(h.dtype), lp['ret_norm'])
    y_b = jax.nn.silu(gb) * o_ret.reshape(B, L, W_B)
    qc, kc, vc = jnp.split(zc, [H_C * 2 * HD_C, 2 * H_C * 2 * HD_C], axis=-1)
    qc = rms_norm(qc.reshape(B, L, H_C, 2, HD_C), lp['q_norm'])
    kc = rms_norm(kc.reshape(B, L, H_C, 2, HD_C), lp['k_norm'])
    vc = vc.reshape(B, L, H_C, DV_C)
    dl = lp['diff_lam'].astype(jnp.float32)
    lam = jnp.exp(jnp.sum(dl[0] * dl[1])) - jnp.exp(jnp.sum(dl[2] * dl[3])) + lam_init
    if cache is None:
        keys, vals = kc, vc
        new_ctx = (kc, vc, r_f.astype(h.dtype), r_b.astype(h.dtype))
    else:
        ang_row, ang_col = axial_rope_angles(L)
        qc = apply_axial_rope(qc, ang_row, ang_col)
        keys = jnp.concatenate([cache[0], apply_axial_rope(kc, ang_row, ang_col)], axis=1)
        vals = jnp.concatenate([cache[1], vc], axis=1)
        new_ctx = None
    y_c = diff_attention(qc, keys, vals, lam, lam_init, lp['diff_norm'])
    y = jnp.concatenate([y_a, y_b, y_c], axis=-1) @ lp['w_out']
    return y, new_ctx


def trunk_layer(x, mods, lp, lam_init, cache):
    sh1, sc1, g1, sh2, sc2, g2 = mods
    h = rms_norm(x, lp['norm1']) * (1.0 + sc1) + sh1
    y, new_ctx = mixing(h, lp, lam_init, cache)
    x = x + g1 * y
    h = rms_norm(x, lp['norm2']) * (1.0 + sc2) + sh2
    x = x + g2 * conv_ffn(h, lp)
    return x, new_ctx


def setup_inputs(seed: int = 0) -> dict:
    key = jax.random.key(seed)
    ks = jax.random.split(key, 32)
    nrm = lambda i, shape, s: jax.random.normal(ks[i], shape, jnp.float32) * s
    base_gamma = 1.0 - 2.0 ** (-5.0 - np.arange(H_B, dtype=np.float32))
    base_logit = jnp.asarray(np.log(base_gamma / (1.0 - base_gamma)), jnp.float32)
    return {
        'x_prompt': nrm(0, (BATCH, SEQ, D_MODEL), 1.0),
        'x_sample': nrm(1, (DEC_BATCH, DEC_SEQ, D_MODEL), 1.0),
        'c': nrm(2, (DEC_BATCH, D_MODEL), 1.0),
        'cache_k': nrm(3, (DEC_BATCH, DEPTH, PAST_LEN, H_C, 2, HD_C), 1.0),
        'cache_v': nrm(4, (DEC_BATCH, DEPTH, PAST_LEN, H_C, DV_C), 1.0),
        'state_ret_fwd': nrm(5, (DEC_BATCH, DEPTH, H_B, DK_B, DV_B), 1.0),
        'state_ret_bwd': nrm(6, (DEC_BATCH, DEPTH, H_B, DK_B, DV_B), 1.0),
        'c_ctx': nrm(7, (D_MODEL,), 1.0),
        'norm1': 1.0 + nrm(8, (DEPTH, D_MODEL), 0.02),
        'w_mod': nrm(9, (DEPTH, D_MODEL, 6 * D_MODEL), 0.5 * D_MODEL ** -0.5),
        'b_mod': nrm(10, (DEPTH, 6 * D_MODEL), 0.02),
        'w_in': nrm(11, (DEPTH, D_MODEL, IN_W), D_MODEL ** -0.5),
        'sgu_norm': 1.0 + nrm(12, (DEPTH, W_A), 0.02),
        'sgu_w': nrm(13, (DEPTH, G_A, CHUNK, CHUNK), CHUNK ** -0.5),
        'sgu_b': 1.0 + nrm(14, (DEPTH, G_A, CHUNK), 0.01),
        'ret_logit_fwd': base_logit + nrm(15, (DEPTH, H_B), 0.1),
        'ret_logit_bwd': base_logit + nrm(16, (DEPTH, H_B), 0.1),
        'ret_norm': 1.0 + nrm(17, (DEPTH, H_B, DV_B), 0.02),
        'q_norm': 1.0 + nrm(18, (DEPTH, HD_C), 0.02),
        'k_norm': 1.0 + nrm(19, (DEPTH, HD_C), 0.02),
        'diff_lam': nrm(20, (DEPTH, 4, HD_C), 0.1),
        'diff_norm': 1.0 + nrm(21, (DEPTH, DV_C), 0.02),
        'w_out': nrm(22, (DEPTH, MIX_W, D_MODEL), MIX_W ** -0.5),
        'norm2': 1.0 + nrm(23, (DEPTH, D_MODEL), 0.02),
        'ffn_up': nrm(24, (DEPTH, D_MODEL, 2 * D_FF), D_MODEL ** -0.5),
        'ffn_conv': nrm(25, (DEPTH, CONV_W, 2 * D_FF), CONV_W ** -0.5),
        'ffn_conv_b': nrm(26, (DEPTH, 2 * D_FF), 0.01),
        'ffn_down': nrm(27, (DEPTH, D_FF, D_MODEL), D_FF ** -0.5),
    }


def reference(x_prompt, x_sample, c, cache_k, cache_v, state_ret_fwd, state_ret_bwd, c_ctx,
              norm1, w_mod, b_mod, w_in, sgu_norm, sgu_w, sgu_b, ret_logit_fwd, ret_logit_bwd,
              ret_norm, q_norm, k_norm, diff_lam, diff_norm, w_out, norm2, ffn_up, ffn_conv,
              ffn_conv_b, ffn_down):
    def layer_params(l):
        return dict(norm1=norm1[l], w_in=w_in[l], sgu_norm=sgu_norm[l], sgu_w=sgu_w[l], sgu_b=sgu_b[l],
                    ret_logit_fwd=ret_logit_fwd[l], ret_logit_bwd=ret_logit_bwd[l], ret_norm=ret_norm[l],
                    q_norm=q_norm[l], k_norm=k_norm[l], diff_lam=diff_lam[l], diff_norm=diff_norm[l],
                    w_out=w_out[l], norm2=norm2[l], ffn_up=ffn_up[l], ffn_conv=ffn_conv[l],
                    ffn_conv_b=ffn_conv_b[l], ffn_down=ffn_down[l])

    y_prompt = x_prompt
    ks_out, vs_out, rf_out, rb_out = [], [], [], []
    for l in range(DEPTH):
        lam_init = 0.8 - 0.6 * math.exp(-0.3 * l)
        mods = adaln(c_ctx[None, :], w_mod[l], b_mod[l])
        y_prompt, (k_l, v_l, rf_l, rb_l) = trunk_layer(y_prompt, mods, layer_params(l), lam_init, None)
        ks_out.append(k_l)
        vs_out.append(v_l)
        rf_out.append(rf_l)
        rb_out.append(rb_l)
    new_cache_k = jnp.stack(ks_out, axis=1)
    new_cache_v = jnp.stack(vs_out, axis=1)
    new_state_ret_fwd = jnp.stack(rf_out, axis=1)
    new_state_ret_bwd = jnp.stack(rb_out, axis=1)

    y_sample = x_sample
    for l in range(DEPTH):
        lam_init = 0.8 - 0.6 * math.exp(-0.3 * l)
        mods = adaln(c, w_mod[l], b_mod[l])
        cache_l = (cache_k[:, l], cache_v[:, l], state_ret_fwd[:, l], state_ret_bwd[:, l])
        y_sample, _ = trunk_layer(y_sample, mods, layer_params(l), lam_init, cache_l)

    return (y_prompt, y_sample, new_cache_k, new_cache_v, new_state_ret_fwd, new_state_ret_bwd)
```

```python
import functools
import math

import jax
import jax.numpy as jnp
from jax import lax
from jax.experimental import pallas as pl
from jax.experimental.pallas import tpu as pltpu

F32 = jnp.float32
BF16 = jnp.bfloat16

GRID_W = 64
CHUNK = 128
EPS = 1e-6
ROPE_BASE = 10000.0
G_A = 4
H_B = 4
DK_B = 64
H_C = 4
HD_C = 64
ROPE_PAIRS = HD_C // 4
LANES = 128
FF_CHUNK = 256
ROW_TILE = 512
VMEM_LIMIT = 56 * 1024 * 1024


def _cparams(sem):
    return pltpu.CompilerParams(dimension_semantics=sem, vmem_limit_bytes=VMEM_LIMIT)


def _const_spec(shape):
    n = len(shape)
    return pl.BlockSpec(shape, lambda *_: (0,) * n)


def _silu(x):
    return x * jax.nn.sigmoid(x)


def _dot(a, b):
    return jnp.dot(a, b, preferred_element_type=F32)


def _dot_nt(a, b):
    return lax.dot_general(a, b, (((1,), (1,)), ((), ())), preferred_element_type=F32)


def _dot_tn(a, b):
    return lax.dot_general(a, b, (((0,), (0,)), ((), ())), preferred_element_type=F32)


def _lo_mask(rows):
    return lax.broadcasted_iota(jnp.int32, (rows, LANES), 1) < (LANES // 2)


def _group64_mean(x2):
    rows, n = x2.shape
    lo = _lo_mask(rows)
    outs = []
    for j in range(n // LANES):
        blk = x2[:, j * LANES:(j + 1) * LANES]
        s_lo = jnp.sum(jnp.where(lo, blk, 0.0), axis=-1, keepdims=True)
        s_hi = jnp.sum(jnp.where(lo, 0.0, blk), axis=-1, keepdims=True)
        outs.append(jnp.where(lo, s_lo, s_hi))
    out = outs[0] if len(outs) == 1 else jnp.concatenate(outs, axis=-1)
    return out * (1.0 / 64.0)


def _group64_rmsnorm(x, g):
    return x * lax.rsqrt(_group64_mean(x * x) + EPS) * g


def _rmsnorm(x, g):
    return x * lax.rsqrt(jnp.mean(x * x, axis=-1, keepdims=True) + EPS) * g


def _mod_kernel(cond_ref, w_ref, b_ref, o_ref):
    s = _silu(cond_ref[...]).astype(BF16)
    o_ref[0] = _dot(s, w_ref[0].astype(BF16)) + b_ref[0]


def _modulation(cond, w_mod, b_mod):
    depth, d, n = w_mod.shape
    rows = cond.shape[0]
    tn = 1536
    return pl.pallas_call(
        _mod_kernel,
        grid=(depth, n // tn),
        in_specs=[
            _const_spec((rows, d)),
            pl.BlockSpec((1, d, tn), lambda l, j: (l, 0, j)),
            pl.BlockSpec((1, 1, tn), lambda l, j: (l, 0, j)),
        ],
        out_specs=pl.BlockSpec((1, rows, tn), lambda l, j: (l, 0, j)),
        out_shape=jax.ShapeDtypeStruct((depth, rows, n), F32),
        compiler_params=_cparams(("parallel", "parallel")),
        name="adaln_mod",
    )(cond, w_mod, b_mod.reshape(depth, 1, n))


def _inproj_kernel(x_ref, mod_ref, g_ref, w_ref, z_ref, *, n_split):
    bt, lt, d = x_ref.shape
    x = x_ref[...].reshape(bt * lt, d)
    m = mod_ref[0]
    h = _rmsnorm(x, g_ref[...]) * (1.0 + m[1:2, :]) + m[0:1, :]
    hb = h.astype(BF16)
    n = w_ref.shape[1]
    tn = n // n_split
    for j in range(n_split):
        z = _dot(hb, w_ref[:, j * tn:(j + 1) * tn])
        z_ref[:, :, j * tn:(j + 1) * tn] = z.reshape(bt, lt, tn)


def _row_tiling(b, l):
    lt = min(l, ROW_TILE)
    bt = max(1, ROW_TILE // lt)
    return bt, lt


def _in_projection(x, mods, g, w_bf16, shared_mod):
    b, l, d = x.shape
    n = w_bf16.shape[1]
    bt, lt = _row_tiling(b, l)
    mod_idx = (lambda i, j: (0, 0, 0)) if shared_mod else (lambda i, j: (i, 0, 0))
    return pl.pallas_call(
        functools.partial(_inproj_kernel, n_split=6),
        grid=(b // bt, l // lt),
        in_specs=[
            pl.BlockSpec((bt, lt, d), lambda i, j: (i, j, 0)),
            pl.BlockSpec((1, 6, d), mod_idx),
            _const_spec((1, d)),
            _const_spec((d, n)),
        ],
        out_specs=pl.BlockSpec((bt, lt, n), lambda i, j: (i, j, 0)),
        out_shape=jax.ShapeDtypeStruct((b, l, n), F32),
        compiler_params=_cparams(("parallel", "parallel")),
        name="in_projection",
    )(x, mods, g, w_bf16)


def _sgu_kernel(z_ref, g_ref, ws_ref, bias_ref, y_ref, *, n_chunks):
    wa = y_ref.shape[2]
    dg = wa // G_A
    lane = lax.broadcasted_iota(jnp.int32, (CHUNK, wa), 1)
    for c in range(n_chunks):
        za = jax.nn.gelu(z_ref[0, c * CHUNK:(c + 1) * CHUNK, :])
        u = za[:, :wa]
        v = _rmsnorm(za[:, wa:], g_ref[...]).astype(BF16)
        s = bias_ref[...]
        for g in range(G_A):
            sg = _dot(ws_ref[g], v)
            s = s + jnp.where((lane >= g * dg) & (lane < (g + 1) * dg), sg, 0.0)
        y_ref[0, c * CHUNK:(c + 1) * CHUNK, :] = u * s


def _mixer_a(z, g, ws_bf16, bias_mat):
    b, l, _ = z.shape
    wa = g.shape[1]
    return pl.pallas_call(
        functools.partial(_sgu_kernel, n_chunks=l // CHUNK),
        grid=(b,),
        in_specs=[
            pl.BlockSpec((1, l, 2 * wa), lambda i: (i, 0, 0)),
            _const_spec((1, wa)),
            _const_spec(ws_bf16.shape),
            _const_spec(bias_mat.shape),
        ],
        out_specs=pl.BlockSpec((1, l, wa), lambda i: (i, 0, 0)),
        out_shape=jax.ShapeDtypeStruct((b, l, wa), F32),
        compiler_params=_cparams(("parallel",)),
        name="mixer_a_sgu",
    )(z, g, ws_bf16, bias_mat)


def _log_sigmoid(x):
    y = -x
    return -(jnp.maximum(y, 0.0) + jnp.log1p(jnp.exp(-jnp.abs(y))))


def _ret_kernel(*refs, n_chunks, has_state):
    if has_state:
        (q_ref, k_ref, v_ref, gate_ref, r0f_ref, r0b_ref, lgp_ref, lgh_ref, norm_ref,
         y_ref, rf_ref, rb_ref, d_scr, dec_scr, r_scr, o_scr) = refs
    else:
        (q_ref, k_ref, v_ref, gate_ref, lgp_ref, lgh_ref, norm_ref,
         y_ref, rf_ref, rb_ref, d_scr, dec_scr, r_scr, o_scr) = refs
    n_pairs = H_B // 2
    half = LANES // 2
    lgp = _log_sigmoid(lgp_ref[...])
    lgh = _log_sigmoid(lgh_ref[...])
    row = lax.broadcasted_iota(jnp.int32, (CHUNK, LANES), 0).astype(F32)
    col = lax.broadcasted_iota(jnp.int32, (CHUNK, LANES), 1).astype(F32)
    dist = row - col
    for h in range(H_B):
        lf = lgh[h:h + 1, :]
        lb = lgh[H_B + h:H_B + h + 1, :]
        d_f = jnp.where(dist >= 0, jnp.exp(lf * jnp.maximum(dist, 0.0)), 0.0)
        d_b = jnp.where(dist <= 0, jnp.exp(lb * jnp.maximum(-dist, 0.0)), 0.0)
        d_scr[h] = d_f + d_b
    cdec = []
    for p in range(n_pairs):
        lf = lgp[p:p + 1, :]
        lb = lgp[n_pairs + p:n_pairs + p + 1, :]
        dec_scr[0, p] = jnp.exp(lf * (row + 1.0))
        dec_scr[1, p] = jnp.exp(lf * (CHUNK - 1.0 - row))
        dec_scr[2, p] = jnp.exp(lb * (CHUNK - row))
        dec_scr[3, p] = jnp.exp(lb * row)
        cdec.append((jnp.exp(lf * float(CHUNK)), jnp.exp(lb * float(CHUNK))))
    lo = _lo_mask(CHUNK)
    blockdiag = (row < half) == (col < half)

    for p in range(n_pairs):
        if has_state:
            zero = jnp.zeros((half, half), F32)
            for d, r0_ref in enumerate((r0f_ref, r0b_ref)):
                top = jnp.concatenate([r0_ref[2 * p], zero], axis=1)
                bot = jnp.concatenate([zero, r0_ref[2 * p + 1]], axis=1)
                r_scr[d, p] = jnp.concatenate([top, bot], axis=0)
        else:
            r_scr[0, p] = jnp.zeros((LANES, LANES), F32)
            r_scr[1, p] = jnp.zeros((LANES, LANES), F32)

    scale = DK_B ** -0.5

    def load_chunk(c):
        r0 = pl.multiple_of(c * CHUNK, CHUNK)
        rows = pl.ds(r0, CHUNK)
        return rows, q_ref[0, rows, :], k_ref[0, rows, :] * scale, v_ref[0, rows, :]

    def fwd_body(c, carry):
        rows, qc, kc, vc = load_chunk(c)
        for p in range(n_pairs):
            sl = slice(p * LANES, (p + 1) * LANES)
            q128, k128, v128 = qc[:, sl], kc[:, sl], vc[:, sl]
            kb = k128.astype(BF16)
            vb = v128.astype(BF16)
            q_lo = jnp.where(lo, q128, 0.0).astype(BF16)
            q_hi = jnp.where(lo, 0.0, q128).astype(BF16)
            in_lo = (_dot_nt(q_lo, kb) * d_scr[2 * p]).astype(BF16)
            in_hi = (_dot_nt(q_hi, kb) * d_scr[2 * p + 1]).astype(BF16)
            o = jnp.where(lo, _dot(in_lo, vb), _dot(in_hi, vb))
            r = r_scr[0, p]
            o = o + _dot(q128.astype(BF16), r.astype(BF16)) * dec_scr[0, p]
            o_scr[rows, sl] = o
            kv = _dot_tn((k128 * dec_scr[1, p]).astype(BF16), vb)
            r_scr[0, p] = r * cdec[p][0] + jnp.where(blockdiag, kv, 0.0)
        return carry

    lax.fori_loop(0, n_chunks, fwd_body, 0)

    def bwd_body(i, carry):
        c = n_chunks - 1 - i
        rows, qc, kc, vc = load_chunk(c)
        outs = []
        for p in range(n_pairs):
            sl = slice(p * LANES, (p + 1) * LANES)
            q128, k128, v128 = qc[:, sl], kc[:, sl], vc[:, sl]
            r = r_scr[1, p]
            outs.append(o_scr[rows, sl] + _dot(q128.astype(BF16), r.astype(BF16)) * dec_scr[2, p])
            kv = _dot_tn((k128 * dec_scr[3, p]).astype(BF16), v128.astype(BF16))
            r_scr[1, p] = r * cdec[p][1] + jnp.where(blockdiag, kv, 0.0)
        o = jnp.concatenate(outs, axis=-1)
        o = _group64_rmsnorm(o, norm_ref[...])
        y_ref[0, rows, :] = _silu(gate_ref[0, rows, :]) * o
        return carry

    lax.fori_loop(0, n_chunks, bwd_body, 0)

    for p in range(n_pairs):
        for d, out_ref in enumerate((rf_ref, rb_ref)):
            r = r_scr[d, p]
            out_ref[0, 2 * p] = r[:half, :half]
            out_ref[0, 2 * p + 1] = r[half:, half:]


def _mixer_b(z, col0, lg_pair, lg_head, norm, state):
    b, l, _ = z.shape
    wb = norm.shape[1]
    cb = col0 // wb
    has_state = state is not None
    zspec = lambda j: pl.BlockSpec((1, l, wb), lambda i: (i, 0, cb + j))
    in_specs = [zspec(0), zspec(1), zspec(2), zspec(3)]
    args = [z, z, z, z]
    if has_state:
        r0f, r0b, layer = state
        sspec = pl.BlockSpec((None, None, H_B, DK_B, DK_B), lambda i: (i, layer, 0, 0, 0))
        in_specs += [sspec, sspec]
        args += [r0f, r0b]
    in_specs += [_const_spec(lg_pair.shape), _const_spec(lg_head.shape), _const_spec((1, wb))]
    args += [lg_pair, lg_head, norm]
    rspec = pl.BlockSpec((1, H_B, DK_B, DK_B), lambda i: (i, 0, 0, 0))
    rshape = jax.ShapeDtypeStruct((b, H_B, DK_B, DK_B), F32)
    return pl.pallas_call(
        functools.partial(_ret_kernel, n_chunks=l // CHUNK, has_state=has_state),
        grid=(b,),
        in_specs=in_specs,
        out_specs=[pl.BlockSpec((1, l, wb), lambda i: (i, 0, 0)), rspec, rspec],
        out_shape=[jax.ShapeDtypeStruct((b, l, wb), F32), rshape, rshape],
        scratch_shapes=[
            pltpu.VMEM((H_B, CHUNK, LANES), F32),
            pltpu.VMEM((4, H_B // 2, CHUNK, LANES), F32),
            pltpu.VMEM((2, H_B // 2, LANES, LANES), F32),
            pltpu.VMEM((l, wb), F32),
        ],
        compiler_params=_cparams(("parallel",)),
        name="mixer_b_retention",
    )(*args)


def _rope(x, cos, sin):
    rows, n = x.shape
    lane = lax.broadcasted_iota(jnp.int32, (rows, LANES), 1)
    first = (lane % (2 * ROPE_PAIRS)) < ROPE_PAIRS
    outs = []
    for j in range(n // LANES):
        blk = x[:, j * LANES:(j + 1) * LANES]
        partner = jnp.where(first, pltpu.roll(blk, LANES - ROPE_PAIRS, 1), pltpu.roll(blk, ROPE_PAIRS, 1))
        outs.append(blk * cos + partner * sin)
    return jnp.concatenate(outs, axis=-1)


def _attn_kernel(*refs, has_cache, lam_init):
    if has_cache:
        (q_ref, k_ref, v_ref, ck_ref, cv_ref, cosq_ref, sinq_ref, cosk_ref, sink_ref,
         qg_ref, kg_ref, dl_ref, dn_ref, y_ref, kn_scr, v_scr) = refs
    else:
        (q_ref, k_ref, v_ref, qg_ref, kg_ref, dl_ref, dn_ref, y_ref, kout_ref, kn_scr, v_scr) = refs
    l_new = k_ref.shape[1]
    n_past = kn_scr.shape[0] - l_new

    @pl.when(pl.program_id(1) == 0)
    def _prepare_keys():
        kn = _group64_rmsnorm(k_ref[0], kg_ref[...])
        if has_cache:
            kn = _rope(kn, cosk_ref[...], sink_ref[...])
            kn_scr[:n_past, :] = ck_ref[...].astype(BF16)
            v_scr[:n_past, :] = cv_ref[...].astype(BF16)
        else:
            kout_ref[0] = kn
        kn_scr[n_past:, :] = kn.astype(BF16)
        v_scr[n_past:, :] = v_ref[0].astype(BF16)

    qn = _group64_rmsnorm(q_ref[0], qg_ref[...])
    if has_cache:
        qn = _rope(qn, cosq_ref[...], sinq_ref[...])
    qn = qn * (HD_C ** -0.5)
    tq = qn.shape[0]
    dl = dl_ref[...]
    lam = (jnp.exp(jnp.sum(dl[0:1] * dl[1:2], axis=-1, keepdims=True))
           - jnp.exp(jnp.sum(dl[2:3] * dl[3:4], axis=-1, keepdims=True)) + lam_init)
    lo = _lo_mask(tq)
    for h in range(H_C):
        sl = slice(h * LANES, (h + 1) * LANES)
        q128 = qn[:, sl]
        k128 = kn_scr[:, sl]
        probs = []
        for q_half in (jnp.where(lo, q128, 0.0), jnp.where(lo, 0.0, q128)):
            s = _dot_nt(q_half.astype(BF16), k128)
            e = jnp.exp(s - jnp.max(s, axis=-1, keepdims=True))
            probs.append((e, jnp.sum(e, axis=-1, keepdims=True)))
        (e0, l0), (e1, l1) = probs
        a = e0 * (1.0 / l0) - e1 * (lam / l1)
        o = _dot(a.astype(BF16), v_scr[:, sl])
        y_ref[0, :, sl] = _rmsnorm(o, dn_ref[...]) * (1.0 - lam_init)


def _mixer_c(z, col0, qg, kg, dl, dn, lam_init, cache):
    b, l, _ = z.shape
    wc = H_C * 2 * HD_C
    cb = col0 // wc
    has_cache = cache is not None
    tq = min(l, 256)
    in_specs = [
        pl.BlockSpec((1, tq, wc), lambda i, j: (i, j, cb)),
        pl.BlockSpec((1, l, wc), lambda i, j: (i, 0, cb + 1)),
        pl.BlockSpec((1, l, wc), lambda i, j: (i, 0, cb + 2)),
    ]
    args = [z, z, z]
    n_past = 0
    if has_cache:
        ck, cv, layer, cos, sin = cache
        n_past = ck.shape[2]
        cspec = pl.BlockSpec((None, None, n_past, wc), lambda i, j: (i, layer, 0, 0))
        in_specs += [cspec, cspec,
                     pl.BlockSpec((tq, LANES), lambda i, j: (j, 0)),
                     pl.BlockSpec((tq, LANES), lambda i, j: (j, 0)),
                     _const_spec((l, LANES)), _const_spec((l, LANES))]
        args += [ck, cv, cos, sin, cos, sin]
    in_specs += [_const_spec((1, wc)), _const_spec((1, wc)), _const_spec(dl.shape), _const_spec((1, LANES))]
    args += [qg, kg, dl, dn]
    out_specs = [pl.BlockSpec((1, tq, wc), lambda i, j: (i, j, 0))]
    out_shape = [jax.ShapeDtypeStruct((b, l, wc), F32)]
    if not has_cache:
        out_specs.append(pl.BlockSpec((1, l, wc), lambda i, j: (i, 0, 0)))
        out_shape.append(jax.ShapeDtypeStruct((b, l, wc), F32))
    return pl.pallas_call(
        functools.partial(_attn_kernel, has_cache=has_cache, lam_init=lam_init),
        grid=(b, l // tq),
        in_specs=in_specs,
        out_specs=out_specs,
        out_shape=out_shape,
        scratch_shapes=[pltpu.VMEM((n_past + l, wc), BF16), pltpu.VMEM((n_past + l, wc), BF16)],
        compiler_params=_cparams(("parallel", "arbitrary")),
        name="mixer_c_diff_attention",
    )(*args)


def _outproj_kernel(ya_ref, yb_ref, yc_ref, x_ref, mod_ref, w_ref, o_ref):
    bt, lt, d = x_ref.shape
    rows = bt * lt
    y = None
    k0 = 0
    for y_ref in (ya_ref, yb_ref, yc_ref):
        kw = y_ref.shape[2]
        part = _dot(y_ref[...].reshape(rows, kw).astype(BF16), w_ref[k0:k0 + kw, :])
        y = part if y is None else y + part
        k0 += kw
    gate = mod_ref[0][2:3, :]
    o_ref[...] = x_ref[...] + (gate * y).reshape(bt, lt, d)


def _out_projection(ya, yb, yc, x, mods, w_bf16, shared_mod):
    b, l, d = x.shape
    bt, lt = _row_tiling(b, l)
    mod_idx = (lambda i, j: (0, 0, 0)) if shared_mod else (lambda i, j: (i, 0, 0))
    tile = lambda a: pl.BlockSpec((bt, lt, a.shape[2]), lambda i, j: (i, j, 0))
    return pl.pallas_call(
        _outproj_kernel,
        grid=(b // bt, l // lt),
        in_specs=[tile(ya), tile(yb), tile(yc), tile(x), pl.BlockSpec((1, 6, d), mod_idx),
                  _const_spec(w_bf16.shape)],
        out_specs=tile(x),
        out_shape=jax.ShapeDtypeStruct((b, l, d), F32),
        compiler_params=_cparams(("parallel", "parallel")),
        name="out_projection",
    )(ya, yb, yc, x, mods, w_bf16)


def _ffn_kernel(x_ref, mod_ref, g_ref, up_ref, cw_ref, cb_ref, down_ref, o_ref, h_scr, acc_scr, *, seq_len):
    bt, lt, d = x_ref.shape
    rows = bt * lt
    x = x_ref[...].reshape(rows, d)
    m = mod_ref[0]
    h = _rmsnorm(x, g_ref[...]) * (1.0 + m[4:5, :]) + m[3:4, :]
    h_scr[...] = h.astype(BF16)
    acc_scr[...] = jnp.zeros_like(acc_scr)
    pos = lax.broadcasted_iota(jnp.int32, (rows, FF_CHUNK), 0) % seq_len
    has_prev = pos > 0
    has_next = pos < seq_len - 1
    n_chunks = down_ref.shape[0]

    def conv(u, w, bias):
        prev = jnp.where(has_prev, pltpu.roll(u, 1, 0), 0.0)
        nxt = jnp.where(has_next, pltpu.roll(u, rows - 1, 0), 0.0)
        return prev * w[0:1, :] + u * w[1:2, :] + nxt * w[2:3, :] + bias

    def body(j, carry):
        hb = h_scr[...]
        a = conv(_dot(hb, up_ref[0, j]), cw_ref[0, j], cb_ref[0, j])
        b = conv(_dot(hb, up_ref[1, j]), cw_ref[1, j], cb_ref[1, j])
        gated = (_silu(a) * b).astype(BF16)
        acc_scr[...] += _dot(gated, down_ref[j])
        return carry

    lax.fori_loop(0, n_chunks, body, 0)
    gate = m[5:6, :]
    o_ref[...] = x_ref[...] + (gate * acc_scr[...]).reshape(bt, lt, d)


def _ffn(x, mods, g, up, cw, cb, down, shared_mod):
    b, l, d = x.shape
    bt = max(1, 1024 // l)
    mod_idx = (lambda i: (0, 0, 0)) if shared_mod else (lambda i: (i, 0, 0))
    once = lambda a: pl.BlockSpec(a.shape, lambda i: (0,) * a.ndim, pipeline_mode=pl.Buffered(1))
    tile = pl.BlockSpec((bt, l, d), lambda i: (i, 0, 0))
    return pl.pallas_call(
        functools.partial(_ffn_kernel, seq_len=l),
        grid=(b // bt,),
        in_specs=[tile, pl.BlockSpec((1, 6, d), mod_idx), _const_spec((1, d)),
                  once(up), once(cw), once(cb), once(down)],
        out_specs=tile,
        out_shape=jax.ShapeDtypeStruct((b, l, d), F32),
        scratch_shapes=[pltpu.VMEM((bt * l, d), BF16), pltpu.VMEM((bt * l, d), F32)],
        compiler_params=_cparams(("parallel",)),
        name="conv_ffn",
    )(x, mods, g, up, cw, cb, down)


def _rope_tables(l):
    rows = l // GRID_W
    t_row = jnp.repeat(jnp.arange(rows, dtype=F32), GRID_W)
    t_col = jnp.tile(jnp.arange(GRID_W, dtype=F32), rows)
    inv = ROPE_BASE ** (-jnp.arange(ROPE_PAIRS, dtype=F32) / ROPE_PAIRS)
    ar, ac = t_row[:, None] * inv, t_col[:, None] * inv
    cos = jnp.concatenate([jnp.cos(ar), jnp.cos(ar), jnp.cos(ac), jnp.cos(ac)], axis=-1)
    sin = jnp.concatenate([-jnp.sin(ar), jnp.sin(ar), -jnp.sin(ac), jnp.sin(ac)], axis=-1)
    return jnp.tile(cos, (1, 2)), jnp.tile(sin, (1, 2))


def kernel(x_prompt, x_sample, c, cache_k, cache_v, state_ret_fwd, state_ret_bwd, c_ctx, norm1, w_mod, b_mod,
           w_in, sgu_norm, sgu_w, sgu_b, ret_logit_fwd, ret_logit_bwd, ret_norm, q_norm, k_norm, diff_lam,
           diff_norm, w_out, norm2, ffn_up, ffn_conv, ffn_conv_b, ffn_down):
    depth, d_model, _ = w_mod.shape
    batch, seq, _ = x_prompt.shape
    dec_batch, dec_seq, _ = x_sample.shape
    w_a = sgu_norm.shape[1]
    w_b = H_B * ret_norm.shape[2]
    d_ff = ffn_down.shape[1]
    n_ff = d_ff // FF_CHUNK
    past = cache_k.shape[2]
    col_b = 2 * w_a
    col_c = col_b + 4 * w_b

    pad = (-(dec_batch + 1)) % 8
    cond = jnp.concatenate([c, c_ctx[None, :], jnp.zeros((pad, d_model), F32)], axis=0)
    mods = _modulation(cond, w_mod, b_mod).reshape(depth, cond.shape[0], 6, d_model)

    w_in_b = w_in.astype(BF16)
    w_out_b = w_out.astype(BF16)
    sgu_w_b = sgu_w.astype(BF16)
    sgu_bias = jnp.repeat(jnp.swapaxes(sgu_b, 1, 2), w_a // G_A, axis=2)
    up_b = ffn_up.astype(BF16).reshape(depth, d_model, 2, n_ff, FF_CHUNK).transpose(0, 2, 3, 1, 4)
    down_b = ffn_down.astype(BF16).reshape(depth, n_ff, FF_CHUNK, d_model)
    conv_w = ffn_conv.reshape(depth, -1, 2, n_ff, FF_CHUNK).transpose(0, 2, 3, 1, 4)
    conv_b = ffn_conv_b.reshape(depth, 2, n_ff, 1, FF_CHUNK)
    logits = jnp.stack([ret_logit_fwd, ret_logit_bwd], axis=1)
    lg_pair = jnp.repeat(logits, DK_B, axis=2).reshape(depth, 2 * (H_B // 2), LANES)
    lg_head = jnp.broadcast_to(logits.reshape(depth, 2 * H_B, 1), (depth, 2 * H_B, LANES))
    qg = jnp.tile(q_norm, (1, 2 * H_C)).reshape(depth, 1, 2 * H_C * HD_C)
    kg = jnp.tile(k_norm, (1, 2 * H_C)).reshape(depth, 1, 2 * H_C * HD_C)
    cos, sin = _rope_tables(dec_seq)
    cache_k2 = cache_k.reshape(dec_batch, depth, past, -1)
    cache_v2 = cache_v.reshape(dec_batch, depth, past, -1)

    def layer(x, l, mods_l, shared_mod, cache):
        lam_init = 0.8 - 0.6 * math.exp(-0.3 * l)
        z = _in_projection(x, mods_l, norm1[l][None, :], w_in_b[l], shared_mod)
        ya = _mixer_a(z, sgu_norm[l][None, :], sgu_w_b[l], sgu_bias[l])
        state = None if cache is None else (state_ret_fwd, state_ret_bwd, l)
        yb, rf, rb = _mixer_b(z, col_b, lg_pair[l], lg_head[l], ret_norm[l].reshape(1, w_b), state)
        attn_cache = None if cache is None else (cache_k2, cache_v2, l, cos, sin)
        res = _mixer_c(z, col_c, qg[l], kg[l], diff_lam[l], diff_norm[l][None, :], lam_init, attn_cache)
        x = _out_projection(ya, yb, res[0], x, mods_l, w_out_b[l], shared_mod)
        x = _ffn(x, mods_l, norm2[l][None, :], up_b[l], conv_w[l], conv_b[l], down_b[l], shared_mod)
        new_v = z[:, :, col_c + 2 * H_C * 2 * HD_C:]
        return x, (res[1] if cache is None else None, new_v, rf, rb)

    y_prompt = x_prompt
    ks, vs, rfs, rbs = [], [], [], []
    for l in range(depth):
        y_prompt, (k_l, v_l, rf_l, rb_l) = layer(y_prompt, l, mods[l, dec_batch:dec_batch + 1], True, None)
        ks.append(k_l.reshape(batch, seq, H_C, 2, HD_C))
        vs.append(v_l.reshape(batch, seq, H_C, 2 * HD_C))
        rfs.append(rf_l)
        rbs.append(rb_l)

    y_sample = x_sample
    for l in range(depth):
        y_sample, _ = layer(y_sample, l, mods[l, :dec_batch], False, True)

    return (y_prompt, y_sample, jnp.stack(ks, axis=1), jnp.stack(vs, axis=1),
            jnp.stack(rfs, axis=1), jnp.stack(rbs, axis=1))
```

```python
import functools
import math

import jax
import jax.numpy as jnp
from jax import lax
from jax.experimental import pallas as pl
from jax.experimental.pallas import tpu as pltpu

F32 = jnp.float32
BF16 = jnp.bfloat16

GRID_W = 64
CHUNK = 128
EPS = 1e-6
ROPE_BASE = 10000.0
G_A = 4
H_B = 4
DK_B = 64
H_C = 4
HD_C = 64
ROPE_PAIRS = HD_C // 4
LANES = 128
MXU_DIM = 256
BF16_ROWS = 16
PROJ_ROWS = 512
MLP_ROWS = 1024
ATTN_Q_ROWS = 256
RET_ROWS = 1024
VMEM_LIMIT = 60 * 1024 * 1024


def _cparams(sem):
    return pltpu.CompilerParams(dimension_semantics=sem, vmem_limit_bytes=VMEM_LIMIT)


def _const_spec(shape):
    n = len(shape)
    return pl.BlockSpec(shape, lambda *_: (0,) * n)


def _resident_spec(shape):
    n = len(shape)
    return pl.BlockSpec(shape, lambda *_: (0,) * n, pipeline_mode=pl.Buffered(1))


def _silu(x):
    return x * jax.nn.sigmoid(x)


def _dot(a, b):
    return jnp.dot(a, b, preferred_element_type=F32)


def _dot_nt(a, b):
    return lax.dot_general(a, b, (((1,), (1,)), ((), ())), preferred_element_type=F32)


def _dot_tn(a, b):
    return lax.dot_general(a, b, (((0,), (0,)), ((), ())), preferred_element_type=F32)


def _lo_mask(rows):
    return lax.broadcasted_iota(jnp.int32, (rows, LANES), 1) < (LANES // 2)


def _group64_mean(x2):
    rows, n = x2.shape
    lo = _lo_mask(rows)
    outs = []
    for j in range(n // LANES):
        blk = x2[:, j * LANES:(j + 1) * LANES]
        s_lo = jnp.sum(jnp.where(lo, blk, 0.0), axis=-1, keepdims=True)
        s_hi = jnp.sum(jnp.where(lo, 0.0, blk), axis=-1, keepdims=True)
        outs.append(jnp.where(lo, s_lo, s_hi))
    out = outs[0] if len(outs) == 1 else jnp.concatenate(outs, axis=-1)
    return out * (1.0 / 64.0)


def _group64_rmsnorm(x, g):
    return x * lax.rsqrt(_group64_mean(x * x) + EPS) * g


def _rmsnorm(x, g):
    return x * lax.rsqrt(jnp.mean(x * x, axis=-1, keepdims=True) + EPS) * g


def _rope(x, cos, sin):
    rows, n = x.shape
    lane = lax.broadcasted_iota(jnp.int32, (rows, LANES), 1)
    first = (lane % (2 * ROPE_PAIRS)) < ROPE_PAIRS
    outs = []
    for j in range(n // LANES):
        blk = x[:, j * LANES:(j + 1) * LANES]
        partner = jnp.where(first, pltpu.roll(blk, LANES - ROPE_PAIRS, 1), pltpu.roll(blk, ROPE_PAIRS, 1))
        outs.append(blk * cos + partner * sin)
    return jnp.concatenate(outs, axis=-1)


def _mod_kernel(cond_ref, w_ref, b_ref, o_ref):
    s = _silu(cond_ref[...]).astype(BF16)
    o_ref[0] = _dot(s, w_ref[0].astype(BF16)) + b_ref[0]


def _modulation(cond, w_mod, b_mod):
    depth, d, n = w_mod.shape
    rows = cond.shape[0]
    tn = n // 4
    return pl.pallas_call(
        _mod_kernel,
        grid=(depth, n // tn),
        in_specs=[
            _const_spec((rows, d)),
            pl.BlockSpec((1, d, tn), lambda l, j: (l, 0, j)),
            pl.BlockSpec((1, 1, tn), lambda l, j: (l, 0, j)),
        ],
        out_specs=pl.BlockSpec((1, rows, tn), lambda l, j: (l, 0, j)),
        out_shape=jax.ShapeDtypeStruct((depth, rows, n), F32),
        compiler_params=_cparams(("parallel", "parallel")),
        name="adaln_mod",
    )(cond, w_mod, b_mod.reshape(depth, 1, n))


def _proj_kernel(*refs, has_rope):
    if has_rope:
        (x_ref, mod_ref, g_ref, w_ref, sg_ref, ws_ref, sb_ref, qg_ref, kg_ref, cos_ref, sin_ref,
         ya_ref, qr_ref, kr_ref, vr_ref, gr_ref, qa_ref, ka_ref, va_ref) = refs
    else:
        (x_ref, mod_ref, g_ref, w_ref, sg_ref, ws_ref, sb_ref, qg_ref, kg_ref,
         ya_ref, qr_ref, kr_ref, vr_ref, gr_ref, qa_ref, ka_ref, va_ref, kf_ref, vf_ref) = refs
    rows = x_ref.shape[0]
    m = mod_ref[0]
    hb = (_rmsnorm(x_ref[...], g_ref[...]) * (1.0 + m[1:2, :]) + m[0:1, :]).astype(BF16)
    wa = ya_ref.shape[1]
    wb = qr_ref.shape[1]
    wc = qa_ref.shape[1]

    def proj(c0, width):
        return _dot(hb, w_ref[:, c0:c0 + width])

    za = jax.nn.gelu(proj(0, 2 * wa))
    u = za[:, :wa]
    v = _rmsnorm(za[:, wa:], sg_ref[...]).astype(BF16)
    lane = lax.broadcasted_iota(jnp.int32, (CHUNK, wa), 1)
    dg = wa // G_A
    for c in range(rows // CHUNK):
        rs = slice(c * CHUNK, (c + 1) * CHUNK)
        s = _dot(ws_ref[G_A - 1], v[rs])
        for g in range(G_A - 2, -1, -1):
            s = jnp.where(lane < (g + 1) * dg, _dot(ws_ref[g], v[rs]), s)
        ya_ref[rs, :] = (u[rs] * (s + sb_ref[...])).astype(BF16)

    c0 = 2 * wa
    z = proj(c0, 2 * wb)
    qr_ref[...] = z[:, :wb].astype(BF16)
    kr_ref[...] = z[:, wb:] * (DK_B ** -0.5)
    z = proj(c0 + 2 * wb, 2 * wb)
    vr_ref[...] = z[:, :wb].astype(BF16)
    gr_ref[...] = _silu(z[:, wb:])

    c0 = c0 + 4 * wb
    qn = _group64_rmsnorm(proj(c0, wc), qg_ref[...])
    if has_rope:
        qn = _rope(qn, cos_ref[...], sin_ref[...])
    qa_ref[...] = (qn * (HD_C ** -0.5)).astype(BF16)
    kn = _group64_rmsnorm(proj(c0 + wc, wc), kg_ref[...])
    if has_rope:
        kn = _rope(kn, cos_ref[...], sin_ref[...])
    else:
        kf_ref[...] = kn
    ka_ref[...] = kn.astype(BF16)
    z = proj(c0 + 2 * wc, wc)
    if not has_rope:
        vf_ref[...] = z
    va_ref[...] = z.astype(BF16)


def _projection(x, mods, seq_len, g, w_bf16, sgu_g, sgu_w, sgu_bias, qg, kg, rope):
    t, d = x.shape
    rows = PROJ_ROWS
    wa = sgu_g.shape[1]
    wb = wa
    wc = qg.shape[1]
    has_rope = rope is not None
    tiles_per_seq = max(1, seq_len // rows)
    shared_mod = mods.shape[0] == 1
    mod_idx = (lambda i: (0, 0, 0)) if shared_mod else (lambda i: (i // tiles_per_seq, 0, 0))
    row_spec = lambda w: pl.BlockSpec((rows, w), lambda i: (i, 0))
    in_specs = [row_spec(d), pl.BlockSpec((1, 6, d), mod_idx), _const_spec((1, d)), _const_spec(w_bf16.shape),
                _const_spec((1, wa)), _const_spec(sgu_w.shape), _const_spec(sgu_bias.shape),
                _const_spec((1, wc)), _const_spec((1, wc))]
    args = [x, mods, g, w_bf16, sgu_g, sgu_w, sgu_bias, qg, kg]
    out_w = [(wa, BF16), (wb, BF16), (wb, F32), (wb, BF16), (wb, F32), (wc, BF16), (wc, BF16), (wc, BF16)]
    if has_rope:
        tab_spec = pl.BlockSpec((rows, LANES), lambda i: (i % tiles_per_seq, 0))
        in_specs += [tab_spec, tab_spec]
        args += list(rope)
    else:
        out_w += [(wc, F32), (wc, F32)]
    return pl.pallas_call(
        functools.partial(_proj_kernel, has_rope=has_rope),
        grid=(t // rows,),
        in_specs=in_specs,
        out_specs=[row_spec(w) for w, _ in out_w],
        out_shape=[jax.ShapeDtypeStruct((t, w), dt) for w, dt in out_w],
        compiler_params=_cparams(("parallel",)),
        name="projection",
    )(*args)


def _log_sigmoid(x):
    y = -x
    return -(jnp.maximum(y, 0.0) + jnp.log1p(jnp.exp(-jnp.abs(y))))


def _ret_kernel(*refs, n_chunks, has_state):
    if has_state:
        (q_ref, k_ref, v_ref, gate_ref, r0f_ref, r0b_ref, lgp_ref, lgh_ref, norm_ref,
         y_ref, rf_ref, rb_ref, d_scr, dec_scr, kv_scr, rs_scr, o_scr) = refs
    else:
        (q_ref, k_ref, v_ref, gate_ref, lgp_ref, lgh_ref, norm_ref,
         y_ref, rf_ref, rb_ref, d_scr, dec_scr, kv_scr, rs_scr, o_scr) = refs
    bt = q_ref.shape[0]
    n_pairs = H_B // 2
    half = LANES // 2
    lgp = _log_sigmoid(lgp_ref[...])
    lgh = _log_sigmoid(lgh_ref[...])
    row = lax.broadcasted_iota(jnp.int32, (CHUNK, LANES), 0).astype(F32)
    col = lax.broadcasted_iota(jnp.int32, (CHUNK, LANES), 1).astype(F32)
    dist = row - col
    for h in range(H_B):
        lf = lgh[h:h + 1, :]
        lb = lgh[H_B + h:H_B + h + 1, :]
        d_f = jnp.where(dist >= 0, jnp.exp(lf * jnp.maximum(dist, 0.0)), 0.0)
        d_b = jnp.where(dist <= 0, jnp.exp(lb * jnp.maximum(-dist, 0.0)), 0.0)
        d_scr[h] = d_f + d_b
    cdec = []
    for p in range(n_pairs):
        lf = lgp[p:p + 1, :]
        lb = lgp[n_pairs + p:n_pairs + p + 1, :]
        dec_scr[0, p] = jnp.exp(lf * (row + 1.0))
        dec_scr[1, p] = jnp.exp(lf * (CHUNK - 1.0 - row))
        dec_scr[2, p] = jnp.exp(lb * (CHUNK - row))
        dec_scr[3, p] = jnp.exp(lb * row)
        cdec.append((jnp.exp(lf * float(CHUNK)), jnp.exp(lb * float(CHUNK))))
    lo = _lo_mask(CHUNK)
    blockdiag = (row < half) == (col < half)

    for b in range(bt):
        def intra(c, carry):
            rows = pl.ds(pl.multiple_of(c * CHUNK, CHUNK), CHUNK)
            qc, kc, vc = q_ref[b, rows, :], k_ref[b, rows, :], v_ref[b, rows, :]
            for p in range(n_pairs):
                sl = slice(p * LANES, (p + 1) * LANES)
                q128, k128, vb = qc[:, sl], kc[:, sl], vc[:, sl]
                kb = k128.astype(BF16)
                zero = jnp.zeros_like(q128)
                in_lo = (_dot_nt(jnp.where(lo, q128, zero), kb) * d_scr[2 * p]).astype(BF16)
                in_hi = (_dot_nt(jnp.where(lo, zero, q128), kb) * d_scr[2 * p + 1]).astype(BF16)
                o_scr[rows, sl] = jnp.where(lo, _dot(in_lo, vb), _dot(in_hi, vb))
                for d in range(2):
                    kv = _dot_tn((k128 * dec_scr[2 * d + 1, p]).astype(BF16), vb)
                    kv_scr[d, c, p] = jnp.where(blockdiag, kv, 0.0)
            return carry

        lax.fori_loop(0, n_chunks, intra, 0, unroll=2)

        for p in range(n_pairs):
            for d, out_ref in enumerate((rf_ref, rb_ref)):
                if has_state:
                    r0_ref = (r0f_ref, r0b_ref)[d]
                    zero = jnp.zeros((half, half), F32)
                    top = jnp.concatenate([r0_ref[b, 2 * p], zero], axis=1)
                    bot = jnp.concatenate([zero, r0_ref[b, 2 * p + 1]], axis=1)
                    r = jnp.concatenate([top, bot], axis=0)
                else:
                    r = jnp.zeros((LANES, LANES), F32)
                order = range(n_chunks) if d == 0 else range(n_chunks - 1, -1, -1)
                for c in order:
                    rs_scr[d, c, p] = r.astype(BF16)
                    r = r * cdec[p][d] + kv_scr[d, c, p]
                out_ref[b, 2 * p] = r[:half, :half]
                out_ref[b, 2 * p + 1] = r[half:, half:]

        def cross(c, carry):
            rows = pl.ds(pl.multiple_of(c * CHUNK, CHUNK), CHUNK)
            qc = q_ref[b, rows, :]
            outs = []
            for p in range(n_pairs):
                sl = slice(p * LANES, (p + 1) * LANES)
                q128 = qc[:, sl]
                outs.append(o_scr[rows, sl]
                            + _dot(q128, rs_scr[0, c, p]) * dec_scr[0, p]
                            + _dot(q128, rs_scr[1, c, p]) * dec_scr[2, p])
            o = _group64_rmsnorm(jnp.concatenate(outs, axis=-1), norm_ref[...])
            y_ref[b, rows, :] = (gate_ref[b, rows, :] * o).astype(BF16)
            return carry

        lax.fori_loop(0, n_chunks, cross, 0, unroll=2)


def _mixer_b(q, k, v, gate, lg_pair, lg_head, norm, state):
    b, l, wb = q.shape
    bt = max(1, RET_ROWS // l)
    n_chunks = l // CHUNK
    has_state = state is not None
    tile = pl.BlockSpec((bt, l, wb), lambda i: (i, 0, 0))
    in_specs = [tile, tile, tile, tile]
    args = [q, k, v, gate]
    if has_state:
        r0f, r0b, layer = state
        sspec = pl.BlockSpec((bt, None, H_B, DK_B, DK_B), lambda i: (i, layer, 0, 0, 0))
        in_specs += [sspec, sspec]
        args += [r0f, r0b]
    in_specs += [_const_spec(lg_pair.shape), _const_spec(lg_head.shape), _const_spec((1, wb))]
    args += [lg_pair, lg_head, norm]
    rspec = pl.BlockSpec((bt, H_B, DK_B, DK_B), lambda i: (i, 0, 0, 0))
    rshape = jax.ShapeDtypeStruct((b, H_B, DK_B, DK_B), F32)
    return pl.pallas_call(
        functools.partial(_ret_kernel, n_chunks=n_chunks, has_state=has_state),
        grid=(b // bt,),
        in_specs=in_specs,
        out_specs=[tile, rspec, rspec],
        out_shape=[jax.ShapeDtypeStruct((b, l, wb), BF16), rshape, rshape],
        scratch_shapes=[
            pltpu.VMEM((H_B, CHUNK, LANES), F32),
            pltpu.VMEM((4, H_B // 2, CHUNK, LANES), F32),
            pltpu.VMEM((2, n_chunks, H_B // 2, LANES, LANES), F32),
            pltpu.VMEM((2, n_chunks, H_B // 2, LANES, LANES), BF16),
            pltpu.VMEM((l, wb), F32),
        ],
        compiler_params=_cparams(("parallel",)),
        name="retention",
    )(*args)


def _attn_kernel(*refs, has_cache, lam_init):
    if has_cache:
        (q_ref, k_ref, v_ref, ck_ref, cv_ref, dl_ref, dn_ref, y_ref, s_scr, a_scr, k_scr, v_scr) = refs
        n_past = ck_ref.shape[0]

        @pl.when(pl.program_id(1) == 0)
        def _gather_keys():
            k_scr[:n_past, :] = ck_ref[...].astype(BF16)
            v_scr[:n_past, :] = cv_ref[...].astype(BF16)
            k_scr[n_past:, :] = k_ref[0]
            v_scr[n_past:, :] = v_ref[0]

        keys = lambda sl: k_scr[:, sl]
        vals = lambda sl: v_scr[:, sl]
    else:
        (q_ref, k_ref, v_ref, dl_ref, dn_ref, y_ref, s_scr, a_scr) = refs
        keys = lambda sl: k_ref[0, :, sl]
        vals = lambda sl: v_ref[0, :, sl]
    tq = q_ref.shape[1]
    dl = dl_ref[...]
    lam = (jnp.exp(jnp.sum(dl[0:1] * dl[1:2], axis=-1, keepdims=True))
           - jnp.exp(jnp.sum(dl[2:3] * dl[3:4], axis=-1, keepdims=True)) + lam_init)
    lo = _lo_mask(tq)
    for h in range(H_C):
        sl = slice(h * LANES, (h + 1) * LANES)
        buf = h % 2
        q128 = q_ref[0, :, sl]
        k128 = keys(sl)
        zero = jnp.zeros_like(q128)
        s_scr[buf, 0] = _dot_nt(jnp.where(lo, q128, zero), k128)
        s_scr[buf, 1] = _dot_nt(jnp.where(lo, zero, q128), k128)

        def softmax_rows(r, carry):
            rows = pl.ds(pl.multiple_of(r * BF16_ROWS, BF16_ROWS), BF16_ROWS)
            s0 = s_scr[buf, 0, rows, :]
            s1 = s_scr[buf, 1, rows, :]
            e0 = jnp.exp(s0 - jnp.max(s0, axis=-1, keepdims=True))
            e1 = jnp.exp(s1 - jnp.max(s1, axis=-1, keepdims=True))
            c0 = 1.0 / jnp.sum(e0, axis=-1, keepdims=True)
            c1 = lam / jnp.sum(e1, axis=-1, keepdims=True)
            a_scr[buf, rows, :] = (e0 * c0 - e1 * c1).astype(BF16)
            return carry

        lax.fori_loop(0, tq // BF16_ROWS, softmax_rows, 0, unroll=2)
        o = _dot(a_scr[buf], vals(sl))
        y_ref[0, :, sl] = (_rmsnorm(o, dn_ref[...]) * (1.0 - lam_init)).astype(BF16)


def _mixer_c(q, k, v, dl, dn, lam_init, cache):
    b, l, wc = q.shape
    has_cache = cache is not None
    tq = min(l, ATTN_Q_ROWS)
    q_spec = pl.BlockSpec((1, tq, wc), lambda i, j: (i, j, 0))
    kv_spec = pl.BlockSpec((1, l, wc), lambda i, j: (i, 0, 0))
    in_specs = [q_spec, kv_spec, kv_spec]
    args = [q, k, v]
    n_keys = l
    if has_cache:
        ck, cv, layer = cache
        n_past = ck.shape[2]
        n_keys += n_past
        cspec = pl.BlockSpec((None, None, n_past, wc), lambda i, j: (i, layer, 0, 0))
        in_specs += [cspec, cspec]
        args += [ck, cv]
    in_specs += [_const_spec(dl.shape), _const_spec((1, LANES))]
    args += [dl, dn]
    scratch = [pltpu.VMEM((2, 2, tq, n_keys), F32), pltpu.VMEM((2, tq, n_keys), BF16)]
    if has_cache:
        scratch += [pltpu.VMEM((n_keys, wc), BF16), pltpu.VMEM((n_keys, wc), BF16)]
    return pl.pallas_call(
        functools.partial(_attn_kernel, has_cache=has_cache, lam_init=lam_init),
        grid=(b, l // tq),
        in_specs=in_specs,
        out_specs=q_spec,
        out_shape=jax.ShapeDtypeStruct((b, l, wc), BF16),
        scratch_shapes=scratch,
        compiler_params=_cparams(("parallel", "arbitrary")),
        name="diff_attention",
    )(*args)


def _mlp_kernel(ya_ref, yb_ref, yc_ref, x_ref, mod_ref, g_ref, wo_ref, up_ref, cw_ref, cb_ref, down_ref,
                o_ref, h_scr, g_scr, *, seq_len):
    rows = x_ref.shape[0]
    m = mod_ref[0]
    y = None
    k0 = 0
    for y_ref in (ya_ref, yb_ref, yc_ref):
        kw = y_ref.shape[1]
        part = _dot(y_ref[...], wo_ref[k0:k0 + kw, :])
        y = part if y is None else y + part
        k0 += kw
    x1 = x_ref[...] + m[2:3, :] * y
    o_ref[...] = x1
    h_scr[...] = (_rmsnorm(x1, g_ref[...]) * (1.0 + m[4:5, :]) + m[3:4, :]).astype(BF16)

    d_ff = down_ref.shape[0]
    fc = MXU_DIM
    pos = lax.broadcasted_iota(jnp.int32, (rows, fc), 0) % seq_len
    has_prev = pos > 0
    has_next = pos < seq_len - 1

    def conv(c0):
        u = _dot(h_scr[...], up_ref[:, c0:c0 + fc])
        w = cw_ref[:, c0:c0 + fc]
        prev = jnp.where(has_prev, pltpu.roll(u, 1, 0), 0.0)
        nxt = jnp.where(has_next, pltpu.roll(u, rows - 1, 0), 0.0)
        return prev * w[0:1, :] + u * w[1:2, :] + nxt * w[2:3, :] + cb_ref[:, c0:c0 + fc]

    for j in range(d_ff // fc):
        a = conv(j * fc)
        b = conv(d_ff + j * fc)
        g_scr[:, j * fc:(j + 1) * fc] = (_silu(a) * b).astype(BF16)

    o_ref[...] = o_ref[...] + m[5:6, :] * _dot(g_scr[...], down_ref[...])


def _mlp(ya, yb, yc, x, mods, seq_len, g, w_out, up, cw, cb, down):
    t, d = x.shape
    rows = MLP_ROWS
    tiles_per_seq = max(1, seq_len // rows)
    shared_mod = mods.shape[0] == 1
    mod_idx = (lambda i: (0, 0, 0)) if shared_mod else (lambda i: (i // tiles_per_seq, 0, 0))
    row_spec = lambda a: pl.BlockSpec((rows, a.shape[1]), lambda i: (i, 0))
    return pl.pallas_call(
        functools.partial(_mlp_kernel, seq_len=seq_len),
        grid=(t // rows,),
        in_specs=[row_spec(ya), row_spec(yb), row_spec(yc), row_spec(x), pl.BlockSpec((1, 6, d), mod_idx),
                  _const_spec((1, d)), _resident_spec(w_out.shape), _resident_spec(up.shape),
                  _resident_spec(cw.shape), _resident_spec(cb.shape), _resident_spec(down.shape)],
        out_specs=row_spec(x),
        out_shape=jax.ShapeDtypeStruct((t, d), F32),
        scratch_shapes=[pltpu.VMEM((rows, d), BF16), pltpu.VMEM((rows, down.shape[0]), BF16)],
        compiler_params=_cparams(("parallel",)),
        name="out_proj_mlp",
    )(ya, yb, yc, x, mods, g, w_out, up, cw, cb, down)


def _rope_tables(l):
    rows = l // GRID_W
    t_row = jnp.repeat(jnp.arange(rows, dtype=F32), GRID_W)
    t_col = jnp.tile(jnp.arange(GRID_W, dtype=F32), rows)
    inv = ROPE_BASE ** (-jnp.arange(ROPE_PAIRS, dtype=F32) / ROPE_PAIRS)
    ar, ac = t_row[:, None] * inv, t_col[:, None] * inv
    cos = jnp.concatenate([jnp.cos(ar), jnp.cos(ar), jnp.cos(ac), jnp.cos(ac)], axis=-1)
    sin = jnp.concatenate([-jnp.sin(ar), jnp.sin(ar), -jnp.sin(ac), jnp.sin(ac)], axis=-1)
    return jnp.tile(cos, (1, 2)), jnp.tile(sin, (1, 2))


def kernel(x_prompt, x_sample, c, cache_k, cache_v, state_ret_fwd, state_ret_bwd, c_ctx, norm1, w_mod, b_mod,
           w_in, sgu_norm, sgu_w, sgu_b, ret_logit_fwd, ret_logit_bwd, ret_norm, q_norm, k_norm, diff_lam,
           diff_norm, w_out, norm2, ffn_up, ffn_conv, ffn_conv_b, ffn_down):
    depth, d_model, _ = w_mod.shape
    batch, seq, _ = x_prompt.shape
    dec_batch, dec_seq, _ = x_sample.shape
    w_a = sgu_norm.shape[1]
    w_b = H_B * ret_norm.shape[2]
    w_c = 2 * H_C * HD_C
    past = cache_k.shape[2]

    pad = (-(dec_batch + 1)) % 8
    cond = jnp.concatenate([c, c_ctx[None, :], jnp.zeros((pad, d_model), F32)], axis=0)
    mods = _modulation(cond, w_mod, b_mod).reshape(depth, cond.shape[0], 6, d_model)

    w_in_b = w_in.astype(BF16)
    w_out_b = w_out.astype(BF16)
    sgu_w_b = sgu_w.astype(BF16)
    up_b = ffn_up.astype(BF16)
    down_b = ffn_down.astype(BF16)
    sgu_bias = jnp.repeat(jnp.swapaxes(sgu_b, 1, 2), w_a // G_A, axis=2)
    logits = jnp.stack([ret_logit_fwd, ret_logit_bwd], axis=1)
    lg_pair = jnp.repeat(logits, DK_B, axis=2).reshape(depth, 2 * (H_B // 2), LANES)
    lg_head = jnp.broadcast_to(logits.reshape(depth, 2 * H_B, 1), (depth, 2 * H_B, LANES))
    qg = jnp.tile(q_norm, (1, 2 * H_C)).reshape(depth, 1, w_c)
    kg = jnp.tile(k_norm, (1, 2 * H_C)).reshape(depth, 1, w_c)
    rope = _rope_tables(dec_seq)
    cache_k2 = cache_k.reshape(dec_batch, depth, past, w_c)
    cache_v2 = cache_v.reshape(dec_batch, depth, past, w_c)

    def layer(x, l, nb, sl, mods_l, is_sample):
        lam_init = 0.8 - 0.6 * math.exp(-0.3 * l)
        outs = _projection(x, mods_l, sl, norm1[l][None, :], w_in_b[l], sgu_norm[l][None, :], sgu_w_b[l],
                           sgu_bias[l], qg[l], kg[l], rope if is_sample else None)
        ya, qr, kr, vr, gr, qa, ka, va = outs[:8]
        seq3 = lambda a: a.reshape(nb, sl, a.shape[1])
        state = (state_ret_fwd, state_ret_bwd, l) if is_sample else None
        yb, rf, rb = _mixer_b(seq3(qr), seq3(kr), seq3(vr), seq3(gr), lg_pair[l], lg_head[l],
                              ret_norm[l].reshape(1, w_b), state)
        cache = (cache_k2, cache_v2, l) if is_sample else None
        yc = _mixer_c(seq3(qa), seq3(ka), seq3(va), diff_lam[l], diff_norm[l][None, :], lam_init, cache)
        x = _mlp(ya, yb.reshape(nb * sl, w_b), yc.reshape(nb * sl, w_c), x, mods_l, sl, norm2[l][None, :],
                 w_out_b[l], up_b[l], ffn_conv[l], ffn_conv_b[l][None, :], down_b[l])
        return x, outs[8:], rf, rb

    y_prompt = x_prompt.reshape(batch * seq, d_model)
    ks, vs, rfs, rbs = [], [], [], []
    for l in range(depth):
        y_prompt, (k_l, v_l), rf_l, rb_l = layer(y_prompt, l, batch, seq, mods[l, dec_batch:dec_batch + 1], False)
        ks.append(k_l.reshape(batch, seq, H_C, 2, HD_C))
        vs.append(v_l.reshape(batch, seq, H_C, 2 * HD_C))
        rfs.append(rf_l)
        rbs.append(rb_l)

    y_sample = x_sample.reshape(dec_batch * dec_seq, d_model)
    for l in range(depth):
        y_sample, _, _, _ = layer(y_sample, l, dec_batch, dec_seq, mods[l, :dec_batch], True)

    return (y_prompt.reshape(batch, seq, d_model), y_sample.reshape(dec_batch, dec_seq, d_model),
            jnp.stack(ks, axis=1), jnp.stack(vs, axis=1), jnp.stack(rfs, axis=1), jnp.stack(rbs, axis=1))
```

```python
import functools
import math

import jax
import jax.numpy as jnp
from jax import lax
from jax.experimental import pallas as pl
from jax.experimental.pallas import tpu as pltpu

F32 = jnp.float32
BF16 = jnp.bfloat16

GRID_W = 64
CHUNK = 128
EPS = 1e-6
ROPE_BASE = 10000.0
G_A = 4
H_B = 4
DK_B = 64
H_C = 4
HD_C = 64
ROPE_PAIRS = HD_C // 4
LANES = 128
MXU_DIM = 256
BF16_ROWS = 16
PROJ_ROWS = 512
MLP_ROWS = 1024
ATTN_Q_ROWS = 256
RET_ROWS = 1024
VMEM_LIMIT = 60 * 1024 * 1024


def _cparams(sem):
    return pltpu.CompilerParams(dimension_semantics=sem, vmem_limit_bytes=VMEM_LIMIT)


def _const_spec(shape):
    n = len(shape)
    return pl.BlockSpec(shape, lambda *_: (0,) * n)


def _resident_spec(shape):
    n = len(shape)
    return pl.BlockSpec(shape, lambda *_: (0,) * n, pipeline_mode=pl.Buffered(1))


def _silu(x):
    return x * jax.nn.sigmoid(x)


def _dot(a, b):
    return jnp.dot(a, b, preferred_element_type=F32)


def _dot_nt(a, b):
    return lax.dot_general(a, b, (((1,), (1,)), ((), ())), preferred_element_type=F32)


def _dot_tn(a, b):
    return lax.dot_general(a, b, (((0,), (0,)), ((), ())), preferred_element_type=F32)


def _lo_mask(rows):
    return lax.broadcasted_iota(jnp.int32, (rows, LANES), 1) < (LANES // 2)


def _group64_mean(x2):
    rows, n = x2.shape
    lo = _lo_mask(rows)
    outs = []
    for j in range(n // LANES):
        blk = x2[:, j * LANES:(j + 1) * LANES]
        s_lo = jnp.sum(jnp.where(lo, blk, 0.0), axis=-1, keepdims=True)
        s_hi = jnp.sum(jnp.where(lo, 0.0, blk), axis=-1, keepdims=True)
        outs.append(jnp.where(lo, s_lo, s_hi))
    out = outs[0] if len(outs) == 1 else jnp.concatenate(outs, axis=-1)
    return out * (1.0 / 64.0)


def _group64_rmsnorm(x, g):
    return x * lax.rsqrt(_group64_mean(x * x) + EPS) * g


def _rmsnorm(x, g):
    return x * lax.rsqrt(jnp.mean(x * x, axis=-1, keepdims=True) + EPS) * g


def _rope(x, cos, sin):
    rows, n = x.shape
    lane = lax.broadcasted_iota(jnp.int32, (rows, LANES), 1)
    first = (lane % (2 * ROPE_PAIRS)) < ROPE_PAIRS
    outs = []
    for j in range(n // LANES):
        blk = x[:, j * LANES:(j + 1) * LANES]
        partner = jnp.where(first, pltpu.roll(blk, LANES - ROPE_PAIRS, 1), pltpu.roll(blk, ROPE_PAIRS, 1))
        outs.append(blk * cos + partner * sin)
    return jnp.concatenate(outs, axis=-1)


def _mod_kernel(cond_ref, w_ref, b_ref, o_ref):
    s = _silu(cond_ref[...]).astype(BF16)
    o_ref[0] = _dot(s, w_ref[0].astype(BF16)) + b_ref[0]


def _modulation(cond, w_mod, b_mod):
    depth, d, n = w_mod.shape
    rows = cond.shape[0]
    tn = n // 4
    return pl.pallas_call(
        _mod_kernel,
        grid=(depth, n // tn),
        in_specs=[
            _const_spec((rows, d)),
            pl.BlockSpec((1, d, tn), lambda l, j: (l, 0, j)),
            pl.BlockSpec((1, 1, tn), lambda l, j: (l, 0, j)),
        ],
        out_specs=pl.BlockSpec((1, rows, tn), lambda l, j: (l, 0, j)),
        out_shape=jax.ShapeDtypeStruct((depth, rows, n), F32),
        compiler_params=_cparams(("parallel", "parallel")),
        name="adaln_mod",
    )(cond, w_mod, b_mod.reshape(depth, 1, n))


def _proj_kernel(*refs, has_rope):
    if has_rope:
        (x_ref, mod_ref, g_ref, w_ref, sg_ref, ws_ref, sb_ref, qg_ref, kg_ref, cos_ref, sin_ref,
         ya_ref, qr_ref, kr_ref, vr_ref, gr_ref, qa_ref, ka_ref, va_ref) = refs
    else:
        (x_ref, mod_ref, g_ref, w_ref, sg_ref, ws_ref, sb_ref, qg_ref, kg_ref,
         ya_ref, qr_ref, kr_ref, vr_ref, gr_ref, qa_ref, ka_ref, va_ref, kf_ref, vf_ref) = refs
    rows = x_ref.shape[0]
    m = mod_ref[0]
    hb = (_rmsnorm(x_ref[...], g_ref[...]) * (1.0 + m[1:2, :]) + m[0:1, :]).astype(BF16)
    wa = ya_ref.shape[1]
    wb = qr_ref.shape[1]
    wc = qa_ref.shape[1]

    def proj(c0, width):
        return _dot(hb, w_ref[:, c0:c0 + width])

    za = jax.nn.gelu(proj(0, 2 * wa))
    u = za[:, :wa]
    v = _rmsnorm(za[:, wa:], sg_ref[...]).astype(BF16)
    lane = lax.broadcasted_iota(jnp.int32, (CHUNK, wa), 1)
    dg = wa // G_A
    for c in range(rows // CHUNK):
        rs = slice(c * CHUNK, (c + 1) * CHUNK)
        s = _dot(ws_ref[G_A - 1], v[rs])
        for g in range(G_A - 2, -1, -1):
            s = jnp.where(lane < (g + 1) * dg, _dot(ws_ref[g], v[rs]), s)
        ya_ref[rs, :] = (u[rs] * (s + sb_ref[...])).astype(BF16)

    c0 = 2 * wa
    z = proj(c0, 2 * wb)
    qr_ref[...] = z[:, :wb].astype(BF16)
    kr_ref[...] = z[:, wb:] * (DK_B ** -0.5)
    z = proj(c0 + 2 * wb, 2 * wb)
    vr_ref[...] = z[:, :wb].astype(BF16)
    gr_ref[...] = _silu(z[:, wb:])

    c0 = c0 + 4 * wb
    qn = _group64_rmsnorm(proj(c0, wc), qg_ref[...])
    if has_rope:
        qn = _rope(qn, cos_ref[...], sin_ref[...])
    qa_ref[...] = (qn * (HD_C ** -0.5)).astype(BF16)
    kn = _group64_rmsnorm(proj(c0 + wc, wc), kg_ref[...])
    if has_rope:
        kn = _rope(kn, cos_ref[...], sin_ref[...])
    else:
        kf_ref[...] = kn
    ka_ref[...] = kn.astype(BF16)
    z = proj(c0 + 2 * wc, wc)
    if not has_rope:
        vf_ref[...] = z
    va_ref[...] = z.astype(BF16)


def _projection(x, mods, seq_len, g, w_bf16, sgu_g, sgu_w, sgu_bias, qg, kg, rope):
    t, d = x.shape
    rows = PROJ_ROWS
    wa = sgu_g.shape[1]
    wb = wa
    wc = qg.shape[1]
    has_rope = rope is not None
    tiles_per_seq = max(1, seq_len // rows)
    shared_mod = mods.shape[0] == 1
    mod_idx = (lambda i: (0, 0, 0)) if shared_mod else (lambda i: (i // tiles_per_seq, 0, 0))
    row_spec = lambda w: pl.BlockSpec((rows, w), lambda i: (i, 0))
    in_specs = [row_spec(d), pl.BlockSpec((1, 6, d), mod_idx), _const_spec((1, d)), _const_spec(w_bf16.shape),
                _const_spec((1, wa)), _const_spec(sgu_w.shape), _const_spec(sgu_bias.shape),
                _const_spec((1, wc)), _const_spec((1, wc))]
    args = [x, mods, g, w_bf16, sgu_g, sgu_w, sgu_bias, qg, kg]
    out_w = [(wa, BF16), (wb, BF16), (wb, F32), (wb, BF16), (wb, F32), (wc, BF16), (wc, BF16), (wc, BF16)]
    if has_rope:
        tab_spec = pl.BlockSpec((rows, LANES), lambda i: (i % tiles_per_seq, 0))
        in_specs += [tab_spec, tab_spec]
        args += list(rope)
    else:
        out_w += [(wc, F32), (wc, F32)]
    return pl.pallas_call(
        functools.partial(_proj_kernel, has_rope=has_rope),
        grid=(t // rows,),
        in_specs=in_specs,
        out_specs=[row_spec(w) for w, _ in out_w],
        out_shape=[jax.ShapeDtypeStruct((t, w), dt) for w, dt in out_w],
        compiler_params=_cparams(("parallel",)),
        name="projection",
    )(*args)


def _log_sigmoid(x):
    y = -x
    return -(jnp.maximum(y, 0.0) + jnp.log1p(jnp.exp(-jnp.abs(y))))


def _ret_kernel(*refs, n_chunks, has_state):
    if has_state:
        (q_ref, k_ref, v_ref, gate_ref, r0f_ref, r0b_ref, lgp_ref, lgh_ref, norm_ref,
         y_ref, rf_ref, rb_ref, d_scr, dec_scr, kv_scr, rs_scr, o_scr) = refs
    else:
        (q_ref, k_ref, v_ref, gate_ref, lgp_ref, lgh_ref, norm_ref,
         y_ref, rf_ref, rb_ref, d_scr, dec_scr, kv_scr, rs_scr, o_scr) = refs
    bt = q_ref.shape[0]
    n_pairs = H_B // 2
    half = LANES // 2
    lgp = _log_sigmoid(lgp_ref[...])
    lgh = _log_sigmoid(lgh_ref[...])
    row = lax.broadcasted_iota(jnp.int32, (CHUNK, LANES), 0).astype(F32)
    col = lax.broadcasted_iota(jnp.int32, (CHUNK, LANES), 1).astype(F32)
    dist = row - col
    for h in range(H_B):
        lf = lgh[h:h + 1, :]
        lb = lgh[H_B + h:H_B + h + 1, :]
        d_f = jnp.where(dist >= 0, jnp.exp(lf * jnp.maximum(dist, 0.0)), 0.0)
        d_b = jnp.where(dist <= 0, jnp.exp(lb * jnp.maximum(-dist, 0.0)), 0.0)
        d_scr[h] = d_f + d_b
    cdec = []
    for p in range(n_pairs):
        lf = lgp[p:p + 1, :]
        lb = lgp[n_pairs + p:n_pairs + p + 1, :]
        dec_scr[0, p] = jnp.exp(lf * (row + 1.0))
        dec_scr[1, p] = jnp.exp(lf * (CHUNK - 1.0 - row))
        dec_scr[2, p] = jnp.exp(lb * (CHUNK - row))
        dec_scr[3, p] = jnp.exp(lb * row)
        cdec.append((jnp.exp(lf * float(CHUNK)), jnp.exp(lb * float(CHUNK))))
    lo = _lo_mask(CHUNK)
    blockdiag = (row < half) == (col < half)

    for b in range(bt):
        def intra(c, carry):
            rows = pl.ds(pl.multiple_of(c * CHUNK, CHUNK), CHUNK)
            qc, kc, vc = q_ref[b, rows, :], k_ref[b, rows, :], v_ref[b, rows, :]
            for p in range(n_pairs):
                sl = slice(p * LANES, (p + 1) * LANES)
                q128, k128, vb = qc[:, sl], kc[:, sl], vc[:, sl]
                kb = k128.astype(BF16)
                zero = jnp.zeros_like(q128)
                in_lo = (_dot_nt(jnp.where(lo, q128, zero), kb) * d_scr[2 * p]).astype(BF16)
                in_hi = (_dot_nt(jnp.where(lo, zero, q128), kb) * d_scr[2 * p + 1]).astype(BF16)
                o_scr[rows, sl] = jnp.where(lo, _dot(in_lo, vb), _dot(in_hi, vb))
                for d in range(2):
                    kv = _dot_tn((k128 * dec_scr[2 * d + 1, p]).astype(BF16), vb)
                    kv_scr[d, c, p] = jnp.where(blockdiag, kv, 0.0)
            return carry

        lax.fori_loop(0, n_chunks, intra, 0, unroll=2)

        for p in range(n_pairs):
            for d, out_ref in enumerate((rf_ref, rb_ref)):
                if has_state:
                    r0_ref = (r0f_ref, r0b_ref)[d]
                    zero = jnp.zeros((half, half), F32)
                    top = jnp.concatenate([r0_ref[b, 2 * p], zero], axis=1)
                    bot = jnp.concatenate([zero, r0_ref[b, 2 * p + 1]], axis=1)
                    r = jnp.concatenate([top, bot], axis=0)
                else:
                    r = jnp.zeros((LANES, LANES), F32)
                order = range(n_chunks) if d == 0 else range(n_chunks - 1, -1, -1)
                for c in order:
                    rs_scr[d, c, p] = r.astype(BF16)
                    r = r * cdec[p][d] + kv_scr[d, c, p]
                out_ref[b, 2 * p] = r[:half, :half]
                out_ref[b, 2 * p + 1] = r[half:, half:]

        def cross(c, carry):
            rows = pl.ds(pl.multiple_of(c * CHUNK, CHUNK), CHUNK)
            qc = q_ref[b, rows, :]
            outs = []
            for p in range(n_pairs):
                sl = slice(p * LANES, (p + 1) * LANES)
                q128 = qc[:, sl]
                outs.append(o_scr[rows, sl]
                            + _dot(q128, rs_scr[0, c, p]) * dec_scr[0, p]
                            + _dot(q128, rs_scr[1, c, p]) * dec_scr[2, p])
            o = _group64_rmsnorm(jnp.concatenate(outs, axis=-1), norm_ref[...])
            y_ref[b, rows, :] = (gate_ref[b, rows, :] * o).astype(BF16)
            return carry

        lax.fori_loop(0, n_chunks, cross, 0, unroll=2)


def _mixer_b(q, k, v, gate, lg_pair, lg_head, norm, state):
    b, l, wb = q.shape
    bt = max(1, RET_ROWS // l)
    n_chunks = l // CHUNK
    has_state = state is not None
    tile = pl.BlockSpec((bt, l, wb), lambda i: (i, 0, 0))
    in_specs = [tile, tile, tile, tile]
    args = [q, k, v, gate]
    if has_state:
        r0f, r0b, layer = state
        sspec = pl.BlockSpec((bt, None, H_B, DK_B, DK_B), lambda i: (i, layer, 0, 0, 0))
        in_specs += [sspec, sspec]
        args += [r0f, r0b]
    in_specs += [_const_spec(lg_pair.shape), _const_spec(lg_head.shape), _const_spec((1, wb))]
    args += [lg_pair, lg_head, norm]
    rspec = pl.BlockSpec((bt, H_B, DK_B, DK_B), lambda i: (i, 0, 0, 0))
    rshape = jax.ShapeDtypeStruct((b, H_B, DK_B, DK_B), F32)
    return pl.pallas_call(
        functools.partial(_ret_kernel, n_chunks=n_chunks, has_state=has_state),
        grid=(b // bt,),
        in_specs=in_specs,
        out_specs=[tile, rspec, rspec],
        out_shape=[jax.ShapeDtypeStruct((b, l, wb), BF16), rshape, rshape],
        scratch_shapes=[
            pltpu.VMEM((H_B, CHUNK, LANES), F32),
            pltpu.VMEM((4, H_B // 2, CHUNK, LANES), F32),
            pltpu.VMEM((2, n_chunks, H_B // 2, LANES, LANES), F32),
            pltpu.VMEM((2, n_chunks, H_B // 2, LANES, LANES), BF16),
            pltpu.VMEM((l, wb), F32),
        ],
        compiler_params=_cparams(("parallel",)),
        name="retention",
    )(*args)


def _attn_kernel(*refs, has_cache, lam_init):
    if has_cache:
        (q_ref, k_ref, v_ref, ck_ref, cv_ref, dl_ref, dn_ref, y_ref, k_scr, v_scr) = refs
        n_past = ck_ref.shape[0]

        @pl.when(pl.program_id(1) == 0)
        def _gather_keys():
            k_scr[:n_past, :] = ck_ref[...].astype(BF16)
            v_scr[:n_past, :] = cv_ref[...].astype(BF16)
            k_scr[n_past:, :] = k_ref[0]
            v_scr[n_past:, :] = v_ref[0]

        keys = lambda sl: k_scr[:, sl]
        vals = lambda sl: v_scr[:, sl]
    else:
        (q_ref, k_ref, v_ref, dl_ref, dn_ref, y_ref) = refs
        keys = lambda sl: k_ref[0, :, sl]
        vals = lambda sl: v_ref[0, :, sl]
    tq = q_ref.shape[1]
    dl = dl_ref[...]
    lam = (jnp.exp(jnp.sum(dl[0:1] * dl[1:2], axis=-1, keepdims=True))
           - jnp.exp(jnp.sum(dl[2:3] * dl[3:4], axis=-1, keepdims=True)) + lam_init)
    lo = _lo_mask(tq)
    for h in range(H_C):
        sl = slice(h * LANES, (h + 1) * LANES)
        q128 = q_ref[0, :, sl]
        zero = jnp.zeros_like(q128)
        qq = jnp.concatenate([jnp.where(lo, q128, zero), jnp.where(lo, zero, q128)], axis=0)
        st = _dot_nt(keys(sl), qq)
        e = jnp.exp(st - jnp.max(st, axis=0, keepdims=True))
        l = jnp.sum(e, axis=0, keepdims=True)
        w = e * jnp.concatenate([1.0 / l[:, :tq], lam / l[:, tq:]], axis=1)
        o = _dot_tn((w[:, :tq] - w[:, tq:]).astype(BF16), vals(sl))
        y_ref[0, :, sl] = (_rmsnorm(o, dn_ref[...]) * (1.0 - lam_init)).astype(BF16)


def _mixer_c(q, k, v, dl, dn, lam_init, cache):
    b, l, wc = q.shape
    has_cache = cache is not None
    tq = min(l, ATTN_Q_ROWS)
    q_spec = pl.BlockSpec((1, tq, wc), lambda i, j: (i, j, 0))
    kv_spec = pl.BlockSpec((1, l, wc), lambda i, j: (i, 0, 0))
    in_specs = [q_spec, kv_spec, kv_spec]
    args = [q, k, v]
    n_keys = l
    if has_cache:
        ck, cv, layer = cache
        n_past = ck.shape[2]
        n_keys += n_past
        cspec = pl.BlockSpec((None, None, n_past, wc), lambda i, j: (i, layer, 0, 0))
        in_specs += [cspec, cspec]
        args += [ck, cv]
    in_specs += [_const_spec(dl.shape), _const_spec((1, LANES))]
    args += [dl, dn]
    scratch = [pltpu.VMEM((n_keys, wc), BF16), pltpu.VMEM((n_keys, wc), BF16)] if has_cache else []
    return pl.pallas_call(
        functools.partial(_attn_kernel, has_cache=has_cache, lam_init=lam_init),
        grid=(b, l // tq),
        in_specs=in_specs,
        out_specs=q_spec,
        out_shape=jax.ShapeDtypeStruct((b, l, wc), BF16),
        scratch_shapes=scratch,
        compiler_params=_cparams(("parallel", "arbitrary")),
        name="diff_attention",
    )(*args)


def _mlp_kernel(ya_ref, yb_ref, yc_ref, x_ref, mod_ref, g_ref, wo_ref, up_ref, cw_ref, cb_ref, down_ref,
                o_ref, h_scr, g_scr, *, seq_len):
    rows = x_ref.shape[0]
    m = mod_ref[0]
    y = None
    k0 = 0
    for y_ref in (ya_ref, yb_ref, yc_ref):
        kw = y_ref.shape[1]
        part = _dot(y_ref[...], wo_ref[k0:k0 + kw, :])
        y = part if y is None else y + part
        k0 += kw
    x1 = x_ref[...] + m[2:3, :] * y
    o_ref[...] = x1
    h_scr[...] = (_rmsnorm(x1, g_ref[...]) * (1.0 + m[4:5, :]) + m[3:4, :]).astype(BF16)

    d_ff = down_ref.shape[0]
    fc = MXU_DIM
    pos = lax.broadcasted_iota(jnp.int32, (rows, fc), 0) % seq_len
    has_prev = pos > 0
    has_next = pos < seq_len - 1

    def conv(c0):
        u = _dot(h_scr[...], up_ref[:, c0:c0 + fc])
        w = cw_ref[:, c0:c0 + fc]
        prev = jnp.where(has_prev, pltpu.roll(u, 1, 0), 0.0)
        nxt = jnp.where(has_next, pltpu.roll(u, rows - 1, 0), 0.0)
        return prev * w[0:1, :] + u * w[1:2, :] + nxt * w[2:3, :] + cb_ref[:, c0:c0 + fc]

    for j in range(d_ff // fc):
        a = conv(j * fc)
        b = conv(d_ff + j * fc)
        g_scr[:, j * fc:(j + 1) * fc] = (_silu(a) * b).astype(BF16)

    o_ref[...] = o_ref[...] + m[5:6, :] * _dot(g_scr[...], down_ref[...])


def _mlp(ya, yb, yc, x, mods, seq_len, g, w_out, up, cw, cb, down):
    t, d = x.shape
    rows = MLP_ROWS
    tiles_per_seq = max(1, seq_len // rows)
    shared_mod = mods.shape[0] == 1
    mod_idx = (lambda i: (0, 0, 0)) if shared_mod else (lambda i: (i // tiles_per_seq, 0, 0))
    row_spec = lambda a: pl.BlockSpec((rows, a.shape[1]), lambda i: (i, 0))
    return pl.pallas_call(
        functools.partial(_mlp_kernel, seq_len=seq_len),
        grid=(t // rows,),
        in_specs=[row_spec(ya), row_spec(yb), row_spec(yc), row_spec(x), pl.BlockSpec((1, 6, d), mod_idx),
                  _const_spec((1, d)), _resident_spec(w_out.shape), _resident_spec(up.shape),
                  _resident_spec(cw.shape), _resident_spec(cb.shape), _resident_spec(down.shape)],
        out_specs=row_spec(x),
        out_shape=jax.ShapeDtypeStruct((t, d), F32),
        scratch_shapes=[pltpu.VMEM((rows, d), BF16), pltpu.VMEM((rows, down.shape[0]), BF16)],
        compiler_params=_cparams(("parallel",)),
        name="out_proj_mlp",
    )(ya, yb, yc, x, mods, g, w_out, up, cw, cb, down)


def _rope_tables(l):
    rows = l // GRID_W
    t_row = jnp.repeat(jnp.arange(rows, dtype=F32), GRID_W)
    t_col = jnp.tile(jnp.arange(GRID_W, dtype=F32), rows)
    inv = ROPE_BASE ** (-jnp.arange(ROPE_PAIRS, dtype=F32) / ROPE_PAIRS)
    ar, ac = t_row[:, None] * inv, t_col[:, None] * inv
    cos = jnp.concatenate([jnp.cos(ar), jnp.cos(ar), jnp.cos(ac), jnp.cos(ac)], axis=-1)
    sin = jnp.concatenate([-jnp.sin(ar), jnp.sin(ar), -jnp.sin(ac), jnp.sin(ac)], axis=-1)
    return jnp.tile(cos, (1, 2)), jnp.tile(sin, (1, 2))


def kernel(x_prompt, x_sample, c, cache_k, cache_v, state_ret_fwd, state_ret_bwd, c_ctx, norm1, w_mod, b_mod,
           w_in, sgu_norm, sgu_w, sgu_b, ret_logit_fwd, ret_logit_bwd, ret_norm, q_norm, k_norm, diff_lam,
           diff_norm, w_out, norm2, ffn_up, ffn_conv, ffn_conv_b, ffn_down):
    depth, d_model, _ = w_mod.shape
    batch, seq, _ = x_prompt.shape
    dec_batch, dec_seq, _ = x_sample.shape
    w_a = sgu_norm.shape[1]
    w_b = H_B * ret_norm.shape[2]
    w_c = 2 * H_C * HD_C
    past = cache_k.shape[2]

    pad = (-(dec_batch + 1)) % 8
    cond = jnp.concatenate([c, c_ctx[None, :], jnp.zeros((pad, d_model), F32)], axis=0)
    mods = _modulation(cond, w_mod, b_mod).reshape(depth, cond.shape[0], 6, d_model)

    w_in_b = w_in.astype(BF16)
    w_out_b = w_out.astype(BF16)
    sgu_w_b = sgu_w.astype(BF16)
    up_b = ffn_up.astype(BF16)
    down_b = ffn_down.astype(BF16)
    sgu_bias = jnp.repeat(jnp.swapaxes(sgu_b, 1, 2), w_a // G_A, axis=2)
    logits = jnp.stack([ret_logit_fwd, ret_logit_bwd], axis=1)
    lg_pair = jnp.repeat(logits, DK_B, axis=2).reshape(depth, 2 * (H_B // 2), LANES)
    lg_head = jnp.broadcast_to(logits.reshape(depth, 2 * H_B, 1), (depth, 2 * H_B, LANES))
    qg = jnp.tile(q_norm, (1, 2 * H_C)).reshape(depth, 1, w_c)
    kg = jnp.tile(k_norm, (1, 2 * H_C)).reshape(depth, 1, w_c)
    rope = _rope_tables(dec_seq)
    cache_k2 = cache_k.reshape(dec_batch, depth, past, w_c)
    cache_v2 = cache_v.reshape(dec_batch, depth, past, w_c)

    def layer(x, l, nb, sl, mods_l, is_sample):
        lam_init = 0.8 - 0.6 * math.exp(-0.3 * l)
        outs = _projection(x, mods_l, sl, norm1[l][None, :], w_in_b[l], sgu_norm[l][None, :], sgu_w_b[l],
                           sgu_bias[l], qg[l], kg[l], rope if is_sample else None)
        ya, qr, kr, vr, gr, qa, ka, va = outs[:8]
        seq3 = lambda a: a.reshape(nb, sl, a.shape[1])
        state = (state_ret_fwd, state_ret_bwd, l) if is_sample else None
        yb, rf, rb = _mixer_b(seq3(qr), seq3(kr), seq3(vr), seq3(gr), lg_pair[l], lg_head[l],
                              ret_norm[l].reshape(1, w_b), state)
        cache = (cache_k2, cache_v2, l) if is_sample else None
        yc = _mixer_c(seq3(qa), seq3(ka), seq3(va), diff_lam[l], diff_norm[l][None, :], lam_init, cache)
        x = _mlp(ya, yb.reshape(nb * sl, w_b), yc.reshape(nb * sl, w_c), x, mods_l, sl, norm2[l][None, :],
                 w_out_b[l], up_b[l], ffn_conv[l], ffn_conv_b[l][None, :], down_b[l])
        return x, outs[8:], rf, rb

    y_prompt = x_prompt.reshape(batch * seq, d_model)
    ks, vs, rfs, rbs = [], [], [], []
    for l in range(depth):
        y_prompt, (k_l, v_l), rf_l, rb_l = layer(y_prompt, l, batch, seq, mods[l, dec_batch:dec_batch + 1], False)
        ks.append(k_l.reshape(batch, seq, H_C, 2, HD_C))
        vs.append(v_l.reshape(batch, seq, H_C, 2 * HD_C))
        rfs.append(rf_l)
        rbs.append(rb_l)

    y_sample = x_sample.reshape(dec_batch * dec_seq, d_model)
    for l in range(depth):
        y_sample, _, _, _ = layer(y_sample, l, dec_batch, dec_seq, mods[l, :dec_batch], True)

    return (y_prompt.reshape(batch, seq, d_model), y_sample.reshape(dec_batch, dec_seq, d_model),
            jnp.stack(ks, axis=1), jnp.stack(vs, axis=1), jnp.stack(rfs, axis=1), jnp.stack(rbs, axis=1))
```

```python
import functools
import math

import jax
import jax.numpy as jnp
from jax import lax
from jax.experimental import pallas as pl
from jax.experimental.pallas import tpu as pltpu

F32 = jnp.float32
BF16 = jnp.bfloat16

GRID_W = 64
CHUNK = 128
EPS = 1e-6
ROPE_BASE = 10000.0
G_A = 4
H_B = 4
DK_B = 64
H_C = 4
HD_C = 64
ROPE_PAIRS = HD_C // 4
LANES = 128
MXU_DIM = 256
PROJ_ROWS = 512
MLP_ROWS = 1024
ATTN_Q_ROWS = 256
RET_ROWS = 1024
VMEM_LIMIT = 60 * 1024 * 1024


def _cparams(sem):
    return pltpu.CompilerParams(dimension_semantics=sem, vmem_limit_bytes=VMEM_LIMIT)


def _const_spec(shape):
    n = len(shape)
    return pl.BlockSpec(shape, lambda *_: (0,) * n)


def _layer_spec(arr, layer, resident=False):
    n = arr.ndim - 1
    mode = dict(pipeline_mode=pl.Buffered(1)) if resident else {}
    return pl.BlockSpec((None,) + arr.shape[1:], lambda *_: (layer,) + (0,) * n, **mode)


def _silu(x):
    return x * jax.nn.sigmoid(x)


def _dot(a, b):
    return jnp.dot(a, b, preferred_element_type=F32)


def _dot_nt(a, b):
    return lax.dot_general(a, b, (((1,), (1,)), ((), ())), preferred_element_type=F32)


def _dot_tn(a, b):
    return lax.dot_general(a, b, (((0,), (0,)), ((), ())), preferred_element_type=F32)


def _lo_mask(rows):
    return lax.broadcasted_iota(jnp.int32, (rows, LANES), 1) < (LANES // 2)


def _split_halves(x):
    lo = _lo_mask(x.shape[0])
    zero = jnp.zeros_like(x)
    return jnp.concatenate([jnp.where(lo, x, zero), jnp.where(lo, zero, x)], axis=0)


def _group64_mean(x2):
    rows, n = x2.shape
    lo = _lo_mask(rows)
    outs = []
    for j in range(n // LANES):
        blk = x2[:, j * LANES:(j + 1) * LANES]
        s_lo = jnp.sum(jnp.where(lo, blk, 0.0), axis=-1, keepdims=True)
        s_hi = jnp.sum(jnp.where(lo, 0.0, blk), axis=-1, keepdims=True)
        outs.append(jnp.where(lo, s_lo, s_hi))
    out = outs[0] if len(outs) == 1 else jnp.concatenate(outs, axis=-1)
    return out * (1.0 / 64.0)


def _group64_rmsnorm(x, g):
    return x * lax.rsqrt(_group64_mean(x * x) + EPS) * g


def _rmsnorm(x, g):
    return x * lax.rsqrt(jnp.mean(x * x, axis=-1, keepdims=True) + EPS) * g


def _rope(x, cos, sin):
    rows, n = x.shape
    lane = lax.broadcasted_iota(jnp.int32, (rows, LANES), 1)
    first = (lane % (2 * ROPE_PAIRS)) < ROPE_PAIRS
    outs = []
    for j in range(n // LANES):
        blk = x[:, j * LANES:(j + 1) * LANES]
        partner = jnp.where(first, pltpu.roll(blk, LANES - ROPE_PAIRS, 1), pltpu.roll(blk, ROPE_PAIRS, 1))
        outs.append(blk * cos + partner * sin)
    return jnp.concatenate(outs, axis=-1)


def _mod_kernel(cond_ref, w_ref, b_ref, o_ref):
    s = _silu(cond_ref[...]).astype(BF16)
    o_ref[0] = _dot(s, w_ref[0].astype(BF16)) + b_ref[0]


def _modulation(cond, w_mod, b_mod):
    depth, d, n = w_mod.shape
    rows = cond.shape[0]
    tn = n // 4
    return pl.pallas_call(
        _mod_kernel,
        grid=(depth, n // tn),
        in_specs=[
            _const_spec((rows, d)),
            pl.BlockSpec((1, d, tn), lambda l, j: (l, 0, j)),
            pl.BlockSpec((1, 1, tn), lambda l, j: (l, 0, j)),
        ],
        out_specs=pl.BlockSpec((1, rows, tn), lambda l, j: (l, 0, j)),
        out_shape=jax.ShapeDtypeStruct((depth, rows, n), F32),
        compiler_params=_cparams(("parallel", "parallel")),
        name="adaln_mod",
    )(cond, w_mod, b_mod.reshape(depth, 1, n))


def _mod_spec(mods, layer, row_of_step):
    return pl.BlockSpec((None, 1) + mods.shape[2:], lambda i: (layer, row_of_step(i), 0, 0))


def _proj_kernel(*refs, has_rope, seq_len):
    if has_rope:
        (x_ref, mod_ref, g_ref, w_ref, sg_ref, ws_ref, sb_ref, qg_ref, kg_ref, cos_ref, sin_ref,
         ya_ref, qr_ref, kr_ref, vr_ref, gr_ref, qa_ref, ka_ref, va_ref) = refs
    else:
        (x_ref, mod_ref, g_ref, w_ref, sg_ref, ws_ref, sb_ref, qg_ref, kg_ref,
         ya_ref, qr_ref, kr_ref, vr_ref, gr_ref, qa_ref, ka_ref, va_ref, kt_ref, vf_ref) = refs
    rows = x_ref.shape[0]
    m = mod_ref[0]
    hb = (_rmsnorm(x_ref[...], g_ref[...]) * (1.0 + m[1:2, :]) + m[0:1, :]).astype(BF16)
    wa = ya_ref.shape[1]
    wb = qr_ref.shape[1]
    wc = qa_ref.shape[1]

    def proj(c0, width):
        return _dot(hb, w_ref[:, c0:c0 + width])

    c0 = 2 * wa + 4 * wb
    qn = _group64_rmsnorm(proj(c0, wc), qg_ref[...])
    if has_rope:
        qn = _rope(qn, cos_ref[...], sin_ref[...])
    qa_ref[...] = (qn * (HD_C ** -0.5)).astype(BF16)
    kn = _group64_rmsnorm(proj(c0 + wc, wc), kg_ref[...])
    if has_rope:
        kn = _rope(kn, cos_ref[...], sin_ref[...])
    else:
        for s in range(rows // seq_len):
            kt_ref[s] = kn[s * seq_len:(s + 1) * seq_len, :].T
    ka_ref[...] = kn.astype(BF16)
    z = proj(c0 + 2 * wc, wc)
    if not has_rope:
        for h in range(H_C):
            vf_ref[:, h, :] = z[:, h * LANES:(h + 1) * LANES]
    va_ref[...] = z.astype(BF16)

    za = jax.nn.gelu(proj(0, 2 * wa))
    u = za[:, :wa]
    v = _rmsnorm(za[:, wa:], sg_ref[...]).astype(BF16)
    lane = lax.broadcasted_iota(jnp.int32, (CHUNK, wa), 1)
    dg = wa // G_A
    for c in range(rows // CHUNK):
        rs = slice(c * CHUNK, (c + 1) * CHUNK)
        s = _dot(ws_ref[G_A - 1], v[rs])
        for g in range(G_A - 2, -1, -1):
            s = jnp.where(lane < (g + 1) * dg, _dot(ws_ref[g], v[rs]), s)
        ya_ref[rs, :] = (u[rs] * (s + sb_ref[...])).astype(BF16)

    c0 = 2 * wa
    z = proj(c0, 2 * wb)
    qr_ref[...] = z[:, :wb].astype(BF16)
    kr_ref[...] = z[:, wb:] * (DK_B ** -0.5)
    z = proj(c0 + 2 * wb, 2 * wb)
    vr_ref[...] = z[:, :wb].astype(BF16)
    gr_ref[...] = _silu(z[:, wb:])


def _projection(x, mods, mod_row, layer, seq_len, g, w_bf16, sgu_g, sgu_w, sgu_bias, qg, kg, rope):
    t, d = x.shape
    rows = PROJ_ROWS
    wa = sgu_g.shape[2]
    wb = wa
    wc = qg.shape[2]
    has_rope = rope is not None
    tiles_per_seq = max(1, seq_len // rows)
    row_spec = lambda w: pl.BlockSpec((rows, w), lambda i: (i, 0))
    params = [g, w_bf16, sgu_g, sgu_w, sgu_bias, qg, kg]
    in_specs = [row_spec(d), _mod_spec(mods, layer, lambda i: mod_row(i // tiles_per_seq))]
    in_specs += [_layer_spec(p, layer) for p in params]
    args = [x, mods] + params
    outs = [(wa, BF16), (wb, BF16), (wb, F32), (wb, BF16), (wb, F32), (wc, BF16), (wc, BF16), (wc, BF16)]
    out_specs = [row_spec(w) for w, _ in outs]
    out_shape = [jax.ShapeDtypeStruct((t, w), dt) for w, dt in outs]
    if has_rope:
        tab_spec = pl.BlockSpec((rows, LANES), lambda i: (i % tiles_per_seq, 0))
        in_specs += [tab_spec, tab_spec]
        args += list(rope)
    else:
        seqs = rows // seq_len
        out_specs += [pl.BlockSpec((seqs, wc, seq_len), lambda i: (i, 0, 0)),
                      pl.BlockSpec((rows, H_C, LANES), lambda i: (i, 0, 0))]
        out_shape += [jax.ShapeDtypeStruct((t // seq_len, wc, seq_len), F32),
                      jax.ShapeDtypeStruct((t, H_C, LANES), F32)]
    return pl.pallas_call(
        functools.partial(_proj_kernel, has_rope=has_rope, seq_len=seq_len),
        grid=(t // rows,),
        in_specs=in_specs,
        out_specs=out_specs,
        out_shape=out_shape,
        compiler_params=_cparams(("parallel",)),
        name="projection",
    )(*args)


def _log_sigmoid(x):
    y = -x
    return -(jnp.maximum(y, 0.0) + jnp.log1p(jnp.exp(-jnp.abs(y))))


def _ret_kernel(*refs, n_chunks, has_state):
    if has_state:
        (q_ref, k_ref, v_ref, gate_ref, r0f_ref, r0b_ref, lgp_ref, lgh_ref, norm_ref,
         y_ref, rf_ref, rb_ref, d_scr, dec_scr, kv_scr, rs_scr, o_scr) = refs
    else:
        (q_ref, k_ref, v_ref, gate_ref, lgp_ref, lgh_ref, norm_ref,
         y_ref, rf_ref, rb_ref, d_scr, dec_scr, kv_scr, rs_scr, o_scr) = refs
    bt = q_ref.shape[0]
    n_pairs = H_B // 2
    half = LANES // 2
    row = lax.broadcasted_iota(jnp.int32, (CHUNK, LANES), 0).astype(F32)
    col = lax.broadcasted_iota(jnp.int32, (CHUNK, LANES), 1).astype(F32)

    @pl.when(pl.program_id(0) == 0)
    def _build_decay_tables():
        lgp = _log_sigmoid(lgp_ref[...])
        lgh = _log_sigmoid(lgh_ref[...])
        dist = row - col
        for h in range(H_B):
            lf = lgh[h:h + 1, :]
            lb = lgh[H_B + h:H_B + h + 1, :]
            d_f = jnp.where(dist >= 0, jnp.exp(lf * jnp.maximum(dist, 0.0)), 0.0)
            d_b = jnp.where(dist <= 0, jnp.exp(lb * jnp.maximum(-dist, 0.0)), 0.0)
            d_scr[h // 2, (h % 2) * CHUNK:(h % 2 + 1) * CHUNK, :] = d_f + d_b
        for p in range(n_pairs):
            lf = lgp[p:p + 1, :]
            lb = lgp[n_pairs + p:n_pairs + p + 1, :]
            dec_scr[0, p] = jnp.exp(lf * (row + 1.0))
            dec_scr[1, p] = jnp.exp(lf * (CHUNK - 1.0 - row))
            dec_scr[2, p] = jnp.exp(lb * (CHUNK - row))
            dec_scr[3, p] = jnp.exp(lb * row)
            dec_scr[4, p] = jnp.broadcast_to(jnp.exp(lf * float(CHUNK)), (CHUNK, LANES))
            dec_scr[5, p] = jnp.broadcast_to(jnp.exp(lb * float(CHUNK)), (CHUNK, LANES))

    lo = _lo_mask(CHUNK)
    blockdiag = (row < half) == (col < half)

    for b in range(bt):
        def intra(c, carry):
            rows = pl.ds(pl.multiple_of(c * CHUNK, CHUNK), CHUNK)
            qc, kc, vc = q_ref[b, rows, :], k_ref[b, rows, :], v_ref[b, rows, :]
            for p in range(n_pairs):
                sl = slice(p * LANES, (p + 1) * LANES)
                k128, vb = kc[:, sl], vc[:, sl]
                inner = (_dot_nt(_split_halves(qc[:, sl]), k128.astype(BF16)) * d_scr[p]).astype(BF16)
                oo = _dot(inner, vb)
                o_scr[rows, sl] = jnp.where(lo, oo[:CHUNK], oo[CHUNK:])
                kk = jnp.concatenate([k128 * dec_scr[1, p], k128 * dec_scr[3, p]], axis=1).astype(BF16)
                kv = _dot_tn(kk, vb)
                kv_scr[0, c, p] = jnp.where(blockdiag, kv[:LANES], 0.0)
                kv_scr[1, c, p] = jnp.where(blockdiag, kv[LANES:], 0.0)
            return carry

        lax.fori_loop(0, n_chunks, intra, 0, unroll=2)

        for p in range(n_pairs):
            for d, out_ref in enumerate((rf_ref, rb_ref)):
                if has_state:
                    r0_ref = (r0f_ref, r0b_ref)[d]
                    zero = jnp.zeros((half, half), F32)
                    top = jnp.concatenate([r0_ref[b, 2 * p], zero], axis=1)
                    bot = jnp.concatenate([zero, r0_ref[b, 2 * p + 1]], axis=1)
                    r = jnp.concatenate([top, bot], axis=0)
                else:
                    r = jnp.zeros((LANES, LANES), F32)
                order = range(n_chunks) if d == 0 else range(n_chunks - 1, -1, -1)
                for c in order:
                    rs_scr[c, p, :, d * LANES:(d + 1) * LANES] = r.astype(BF16)
                    r = r * dec_scr[4 + d, p] + kv_scr[d, c, p]
                out_ref[b, 2 * p] = r[:half, :half]
                out_ref[b, 2 * p + 1] = r[half:, half:]

        def cross(c, carry):
            rows = pl.ds(pl.multiple_of(c * CHUNK, CHUNK), CHUNK)
            qc = q_ref[b, rows, :]
            outs = []
            for p in range(n_pairs):
                sl = slice(p * LANES, (p + 1) * LANES)
                t = _dot(qc[:, sl], rs_scr[c, p])
                outs.append(o_scr[rows, sl] + t[:, :LANES] * dec_scr[0, p] + t[:, LANES:] * dec_scr[2, p])
            o = _group64_rmsnorm(jnp.concatenate(outs, axis=-1), norm_ref[...])
            y_ref[b, rows, :] = (gate_ref[b, rows, :] * o).astype(BF16)
            return carry

        lax.fori_loop(0, n_chunks, cross, 0, unroll=2)


def _mixer_b(q, k, v, gate, layer, lg_pair, lg_head, norm, state):
    b, l, wb = q.shape
    bt = max(1, RET_ROWS // l)
    n_chunks = l // CHUNK
    has_state = state is not None
    tile = pl.BlockSpec((bt, l, wb), lambda i: (i, 0, 0))
    in_specs = [tile, tile, tile, tile]
    args = [q, k, v, gate]
    if has_state:
        sspec = pl.BlockSpec((bt, None, H_B, DK_B, DK_B), lambda i: (i, layer, 0, 0, 0))
        in_specs += [sspec, sspec]
        args += list(state)
    params = [lg_pair, lg_head, norm]
    in_specs += [_layer_spec(p, layer) for p in params]
    args += params
    rspec = pl.BlockSpec((bt, H_B, DK_B, DK_B), lambda i: (i, 0, 0, 0))
    rshape = jax.ShapeDtypeStruct((b, H_B, DK_B, DK_B), F32)
    n_pairs = H_B // 2
    return pl.pallas_call(
        functools.partial(_ret_kernel, n_chunks=n_chunks, has_state=has_state),
        grid=(b // bt,),
        in_specs=in_specs,
        out_specs=[tile, rspec, rspec],
        out_shape=[jax.ShapeDtypeStruct((b, l, wb), BF16), rshape, rshape],
        scratch_shapes=[
            pltpu.VMEM((n_pairs, 2 * CHUNK, LANES), F32),
            pltpu.VMEM((6, n_pairs, CHUNK, LANES), F32),
            pltpu.VMEM((2, n_chunks, n_pairs, LANES, LANES), F32),
            pltpu.VMEM((n_chunks, n_pairs, LANES, 2 * LANES), BF16),
            pltpu.VMEM((l, wb), F32),
        ],
        compiler_params=_cparams(("arbitrary",)),
        name="retention",
    )(*args)


def _attn_kernel(*refs, has_cache, lam_init):
    if has_cache:
        (q_ref, k_ref, v_ref, ck_ref, cv_ref, dl_ref, dn_ref, y_ref, k_scr, v_scr) = refs
        n_past = ck_ref.shape[1]

        @pl.when(pl.program_id(1) == 0)
        def _gather_keys():
            k_scr[:n_past, :] = ck_ref[...].T.astype(BF16)
            for h in range(H_C):
                v_scr[:n_past, h * LANES:(h + 1) * LANES] = cv_ref[:, h, :].astype(BF16)
            k_scr[n_past:, :] = k_ref[0]
            v_scr[n_past:, :] = v_ref[0]

        keys = lambda sl: k_scr[:, sl]
        vals = lambda sl: v_scr[:, sl]
    else:
        (q_ref, k_ref, v_ref, dl_ref, dn_ref, y_ref) = refs
        keys = lambda sl: k_ref[0, :, sl]
        vals = lambda sl: v_ref[0, :, sl]
    tq = q_ref.shape[1]
    dl = dl_ref[...]
    lam = (jnp.exp(jnp.sum(dl[0:1] * dl[1:2], axis=-1, keepdims=True))
           - jnp.exp(jnp.sum(dl[2:3] * dl[3:4], axis=-1, keepdims=True)) + lam_init)
    for h in range(H_C):
        sl = slice(h * LANES, (h + 1) * LANES)
        st = _dot_nt(keys(sl), _split_halves(q_ref[0, :, sl]))
        e = jnp.exp(st - jnp.max(st, axis=0, keepdims=True))
        l = jnp.sum(e, axis=0, keepdims=True)
        w = e * jnp.concatenate([1.0 / l[:, :tq], lam / l[:, tq:]], axis=1)
        o = _dot_tn((w[:, :tq] - w[:, tq:]).astype(BF16), vals(sl))
        y_ref[0, :, sl] = (_rmsnorm(o, dn_ref[...]) * (1.0 - lam_init)).astype(BF16)


def _mixer_c(q, k, v, layer, dl, dn, lam_init, cache):
    b, l, wc = q.shape
    has_cache = cache is not None
    tq = min(l, ATTN_Q_ROWS)
    q_spec = pl.BlockSpec((1, tq, wc), lambda i, j: (i, j, 0))
    kv_spec = pl.BlockSpec((1, l, wc), lambda i, j: (i, 0, 0))
    in_specs = [q_spec, kv_spec, kv_spec]
    args = [q, k, v]
    n_keys = l
    if has_cache:
        ck_t, cv = cache
        n_past = cv.shape[2]
        n_keys += n_past
        in_specs += [pl.BlockSpec((None, None, wc, n_past), lambda i, j: (i, layer, 0, 0)),
                     pl.BlockSpec((None, None, n_past, H_C, LANES), lambda i, j: (i, layer, 0, 0, 0))]
        args += [ck_t, cv]
    in_specs += [_layer_spec(dl, layer), _layer_spec(dn, layer)]
    args += [dl, dn]
    scratch = [pltpu.VMEM((n_keys, wc), BF16), pltpu.VMEM((n_keys, wc), BF16)] if has_cache else []
    return pl.pallas_call(
        functools.partial(_attn_kernel, has_cache=has_cache, lam_init=lam_init),
        grid=(b, l // tq),
        in_specs=in_specs,
        out_specs=q_spec,
        out_shape=jax.ShapeDtypeStruct((b, l, wc), BF16),
        scratch_shapes=scratch,
        compiler_params=_cparams(("parallel", "arbitrary")),
        name="diff_attention",
    )(*args)


def _mlp_kernel(ya_ref, yb_ref, yc_ref, x_ref, mod_ref, g_ref, wo_ref, up_ref, cw_ref, cb_ref, down_ref,
                o_ref, h_scr, g_scr, *, seq_len):
    rows = x_ref.shape[0]
    m = mod_ref[0]
    y = None
    k0 = 0
    for y_ref in (ya_ref, yb_ref, yc_ref):
        kw = y_ref.shape[1]
        part = _dot(y_ref[...], wo_ref[k0:k0 + kw, :])
        y = part if y is None else y + part
        k0 += kw
    x1 = x_ref[...] + m[2:3, :] * y
    o_ref[...] = x1
    h_scr[...] = (_rmsnorm(x1, g_ref[...]) * (1.0 + m[4:5, :]) + m[3:4, :]).astype(BF16)

    d_ff = down_ref.shape[0]
    fc = MXU_DIM
    pos = lax.broadcasted_iota(jnp.int32, (rows, fc), 0) % seq_len
    has_prev = pos > 0
    has_next = pos < seq_len - 1

    def conv(c0):
        u = _dot(h_scr[...], up_ref[:, c0:c0 + fc])
        w = cw_ref[:, c0:c0 + fc]
        prev = jnp.where(has_prev, pltpu.roll(u, 1, 0), 0.0)
        nxt = jnp.where(has_next, pltpu.roll(u, rows - 1, 0), 0.0)
        return prev * w[0:1, :] + u * w[1:2, :] + nxt * w[2:3, :] + cb_ref[:, c0:c0 + fc]

    for j in range(d_ff // fc):
        a = conv(j * fc)
        b = conv(d_ff + j * fc)
        g_scr[:, j * fc:(j + 1) * fc] = (_silu(a) * b).astype(BF16)

    o_ref[...] = o_ref[...] + m[5:6, :] * _dot(g_scr[...], down_ref[...])


def _mlp(ya, yb, yc, x, mods, mod_row, layer, seq_len, g, w_out, up, cw, cb, down):
    t, d = x.shape
    rows = MLP_ROWS
    tiles_per_seq = max(1, seq_len // rows)
    row_spec = lambda a: pl.BlockSpec((rows, a.shape[1]), lambda i: (i, 0))
    weights = [w_out, up, cw, cb, down]
    return pl.pallas_call(
        functools.partial(_mlp_kernel, seq_len=seq_len),
        grid=(t // rows,),
        in_specs=[row_spec(ya), row_spec(yb), row_spec(yc), row_spec(x),
                  _mod_spec(mods, layer, lambda i: mod_row(i // tiles_per_seq)), _layer_spec(g, layer)]
                 + [_layer_spec(w, layer, resident=True) for w in weights],
        out_specs=row_spec(x),
        out_shape=jax.ShapeDtypeStruct((t, d), F32),
        scratch_shapes=[pltpu.VMEM((rows, d), BF16), pltpu.VMEM((rows, down.shape[1]), BF16)],
        compiler_params=_cparams(("parallel",)),
        name="out_proj_mlp",
    )(ya, yb, yc, x, mods, g, *weights)


def _rope_tables(l):
    rows = l // GRID_W
    t_row = jnp.repeat(jnp.arange(rows, dtype=F32), GRID_W)
    t_col = jnp.tile(jnp.arange(GRID_W, dtype=F32), rows)
    inv = ROPE_BASE ** (-jnp.arange(ROPE_PAIRS, dtype=F32) / ROPE_PAIRS)
    ar, ac = t_row[:, None] * inv, t_col[:, None] * inv
    cos = jnp.concatenate([jnp.cos(ar), jnp.cos(ar), jnp.cos(ac), jnp.cos(ac)], axis=-1)
    sin = jnp.concatenate([-jnp.sin(ar), jnp.sin(ar), -jnp.sin(ac), jnp.sin(ac)], axis=-1)
    return jnp.tile(cos, (1, 2)), jnp.tile(sin, (1, 2))


def kernel(x_prompt, x_sample, c, cache_k, cache_v, state_ret_fwd, state_ret_bwd, c_ctx, norm1, w_mod, b_mod,
           w_in, sgu_norm, sgu_w, sgu_b, ret_logit_fwd, ret_logit_bwd, ret_norm, q_norm, k_norm, diff_lam,
           diff_norm, w_out, norm2, ffn_up, ffn_conv, ffn_conv_b, ffn_down):
    depth, d_model, _ = w_mod.shape
    batch, seq, _ = x_prompt.shape
    dec_batch, dec_seq, _ = x_sample.shape
    w_a = sgu_norm.shape[1]
    w_b = H_B * ret_norm.shape[2]
    w_c = 2 * H_C * HD_C
    past = cache_k.shape[2]

    pad = (-(dec_batch + 1)) % 8
    cond = jnp.concatenate([c, c_ctx[None, :], jnp.zeros((pad, d_model), F32)], axis=0)
    mods = _modulation(cond, w_mod, b_mod).reshape(depth, cond.shape[0], 6, d_model)

    w_in_b = w_in.astype(BF16)
    w_out_b = w_out.astype(BF16)
    sgu_w_b = sgu_w.astype(BF16)
    up_b = ffn_up.astype(BF16)
    down_b = ffn_down.astype(BF16)
    row3 = lambda a: a.reshape(depth, 1, -1)
    sgu_bias = jnp.repeat(jnp.swapaxes(sgu_b, 1, 2), w_a // G_A, axis=2)
    logits = jnp.stack([ret_logit_fwd, ret_logit_bwd], axis=1)
    lg_pair = jnp.repeat(logits, DK_B, axis=2).reshape(depth, 2 * (H_B // 2), LANES)
    lg_head = jnp.broadcast_to(logits.reshape(depth, 2 * H_B, 1), (depth, 2 * H_B, LANES))
    qg = row3(jnp.tile(q_norm, (1, 2 * H_C)))
    kg = row3(jnp.tile(k_norm, (1, 2 * H_C)))
    norm1_r, norm2_r, sgu_g, ret_g, diff_g = row3(norm1), row3(norm2), row3(sgu_norm), row3(ret_norm), row3(diff_norm)
    conv_b = row3(ffn_conv_b)
    rope = _rope_tables(dec_seq)
    cache_kt = jnp.transpose(cache_k, (0, 1, 3, 4, 5, 2)).reshape(dec_batch, depth, w_c, past)

    def layer(x, l, nb, sl, mod_row, is_sample):
        lam_init = 0.8 - 0.6 * math.exp(-0.3 * l)
        outs = _projection(x, mods, mod_row, l, sl, norm1_r, w_in_b, sgu_g, sgu_w_b, sgu_bias, qg, kg,
                           rope if is_sample else None)
        ya, qr, kr, vr, gr, qa, ka, va = outs[:8]
        seq3 = lambda a: a.reshape(nb, sl, a.shape[1])
        state = (state_ret_fwd, state_ret_bwd) if is_sample else None
        yb, rf, rb = _mixer_b(seq3(qr), seq3(kr), seq3(vr), seq3(gr), l, lg_pair, lg_head, ret_g, state)
        cache = (cache_kt, cache_v) if is_sample else None
        yc = _mixer_c(seq3(qa), seq3(ka), seq3(va), l, diff_lam, diff_g, lam_init, cache)
        x = _mlp(ya, yb.reshape(nb * sl, w_b), yc.reshape(nb * sl, w_c), x, mods, mod_row, l, sl, norm2_r,
                 w_out_b, up_b, ffn_conv, conv_b, down_b)
        return x, outs[8:], rf, rb

    y_prompt = x_prompt.reshape(batch * seq, d_model)
    kts, vs, rfs, rbs = [], [], [], []
    for l in range(depth):
        y_prompt, (kt_l, v_l), rf_l, rb_l = layer(y_prompt, l, batch, seq, lambda b: dec_batch, False)
        kts.append(kt_l.reshape(batch, H_C, 2, HD_C, seq))
        vs.append(v_l.reshape(batch, seq, H_C, 2 * HD_C))
        rfs.append(rf_l)
        rbs.append(rb_l)

    y_sample = x_sample.reshape(dec_batch * dec_seq, d_model)
    for l in range(depth):
        y_sample, _, _, _ = layer(y_sample, l, dec_batch, dec_seq, lambda b: b, True)

    new_k = jnp.transpose(jnp.stack(kts, axis=1), (0, 1, 5, 2, 3, 4))
    return (y_prompt.reshape(batch, seq, d_model), y_sample.reshape(dec_batch, dec_seq, d_model),
            new_k, jnp.stack(vs, axis=1), jnp.stack(rfs, axis=1), jnp.stack(rbs, axis=1))
```

```python
import functools
import math

import jax
import jax.numpy as jnp
from jax import lax
from jax.experimental import pallas as pl
from jax.experimental.pallas import tpu as pltpu

F32 = jnp.float32
BF16 = jnp.bfloat16

GRID_W = 64
CHUNK = 128
EPS = 1e-6
ROPE_BASE = 10000.0
G_A = 4
H_B = 4
DK_B = 64
H_C = 4
HD_C = 64
ROPE_PAIRS = HD_C // 4
LANES = 128
MXU_DIM = 256
PROJ_ROWS = 512
MLP_ROWS = 1024
ATTN_Q_ROWS = 256
RET_ROWS = 1024
VMEM_LIMIT = 60 * 1024 * 1024


def _cparams(sem):
    return pltpu.CompilerParams(dimension_semantics=sem, vmem_limit_bytes=VMEM_LIMIT)


def _const_spec(shape):
    n = len(shape)
    return pl.BlockSpec(shape, lambda *_: (0,) * n)


def _layer_spec(arr, layer, resident=False):
    n = arr.ndim - 1
    mode = dict(pipeline_mode=pl.Buffered(1)) if resident else {}
    return pl.BlockSpec((None,) + arr.shape[1:], lambda *_: (layer,) + (0,) * n, **mode)


def _silu(x):
    return x * jax.nn.sigmoid(x)


def _dot(a, b):
    return jnp.dot(a, b, preferred_element_type=F32)


def _dot_nt(a, b):
    return lax.dot_general(a, b, (((1,), (1,)), ((), ())), preferred_element_type=F32)


def _dot_tn(a, b):
    return lax.dot_general(a, b, (((0,), (0,)), ((), ())), preferred_element_type=F32)


def _lo_mask(rows):
    return lax.broadcasted_iota(jnp.int32, (rows, LANES), 1) < (LANES // 2)


def _split_halves(x):
    lo = _lo_mask(x.shape[0])
    zero = jnp.zeros_like(x)
    return jnp.concatenate([jnp.where(lo, x, zero), jnp.where(lo, zero, x)], axis=0)


def _group64_mean(x2):
    rows, n = x2.shape
    lo = _lo_mask(rows)
    outs = []
    for j in range(n // LANES):
        blk = x2[:, j * LANES:(j + 1) * LANES]
        s_lo = jnp.sum(jnp.where(lo, blk, 0.0), axis=-1, keepdims=True)
        s_hi = jnp.sum(jnp.where(lo, 0.0, blk), axis=-1, keepdims=True)
        outs.append(jnp.where(lo, s_lo, s_hi))
    out = outs[0] if len(outs) == 1 else jnp.concatenate(outs, axis=-1)
    return out * (1.0 / 64.0)


def _group64_rmsnorm(x, g):
    return x * lax.rsqrt(_group64_mean(x * x) + EPS) * g


def _rmsnorm(x, g):
    return x * lax.rsqrt(jnp.mean(x * x, axis=-1, keepdims=True) + EPS) * g


def _rope(x, cos, sin):
    rows, n = x.shape
    lane = lax.broadcasted_iota(jnp.int32, (rows, LANES), 1)
    first = (lane % (2 * ROPE_PAIRS)) < ROPE_PAIRS
    outs = []
    for j in range(n // LANES):
        blk = x[:, j * LANES:(j + 1) * LANES]
        partner = jnp.where(first, pltpu.roll(blk, LANES - ROPE_PAIRS, 1), pltpu.roll(blk, ROPE_PAIRS, 1))
        outs.append(blk * cos + partner * sin)
    return jnp.concatenate(outs, axis=-1)


def _mod_kernel(cond_ref, w_ref, b_ref, o_ref):
    s = _silu(cond_ref[...]).astype(BF16)
    o_ref[0] = _dot(s, w_ref[0].astype(BF16)) + b_ref[0]


def _modulation(cond, w_mod, b_mod):
    depth, d, n = w_mod.shape
    rows = cond.shape[0]
    tn = n // 4
    return pl.pallas_call(
        _mod_kernel,
        grid=(depth, n // tn),
        in_specs=[
            _const_spec((rows, d)),
            pl.BlockSpec((1, d, tn), lambda l, j: (l, 0, j)),
            pl.BlockSpec((1, 1, tn), lambda l, j: (l, 0, j)),
        ],
        out_specs=pl.BlockSpec((1, rows, tn), lambda l, j: (l, 0, j)),
        out_shape=jax.ShapeDtypeStruct((depth, rows, n), F32),
        compiler_params=_cparams(("parallel", "parallel")),
        name="adaln_mod",
    )(cond, w_mod, b_mod.reshape(depth, 1, n))


def _mod_spec(mods, layer, row_of_step):
    return pl.BlockSpec((None, 1) + mods.shape[2:], lambda i: (layer, row_of_step(i), 0, 0))


def _proj_kernel(*refs, has_rope, seq_len):
    if has_rope:
        (x_ref, mod_ref, g_ref, w_ref, sg_ref, ws_ref, sb_ref, qg_ref, kg_ref, cos_ref, sin_ref,
         ya_ref, qr_ref, kr_ref, vr_ref, gr_ref, qa_ref, ka_ref, va_ref) = refs
    else:
        (x_ref, mod_ref, g_ref, w_ref, sg_ref, ws_ref, sb_ref, qg_ref, kg_ref,
         ya_ref, qr_ref, kr_ref, vr_ref, gr_ref, qa_ref, ka_ref, va_ref, kt_ref, vf_ref) = refs
    rows = x_ref.shape[0]
    m = mod_ref[0]
    hb = (_rmsnorm(x_ref[...], g_ref[...]) * (1.0 + m[1:2, :]) + m[0:1, :]).astype(BF16)
    wa = ya_ref.shape[1]
    wb = qr_ref.shape[1]
    wc = qa_ref.shape[1]

    def proj(c0, width):
        return _dot(hb, w_ref[:, c0:c0 + width])

    def emit_q(z):
        qn = _group64_rmsnorm(z, qg_ref[...])
        if has_rope:
            qn = _rope(qn, cos_ref[...], sin_ref[...])
        qa_ref[...] = (qn * (HD_C ** -0.5)).astype(BF16)

    def emit_k(z):
        kn = _group64_rmsnorm(z, kg_ref[...])
        if has_rope:
            kn = _rope(kn, cos_ref[...], sin_ref[...])
        else:
            for s in range(rows // seq_len):
                kt_ref[s] = kn[s * seq_len:(s + 1) * seq_len, :].T
        ka_ref[...] = kn.astype(BF16)

    def emit_v(z):
        if not has_rope:
            for h in range(H_C):
                vf_ref[:, h, :] = z[:, h * LANES:(h + 1) * LANES]
        va_ref[...] = z.astype(BF16)

    def sgu_prepare(z):
        za = jax.nn.gelu(z)
        return za[:, :wa], _rmsnorm(za[:, wa:], sg_ref[...]).astype(BF16)

    def sgu_mix(u, v):
        lane = lax.broadcasted_iota(jnp.int32, (CHUNK, wa), 1)
        dg = wa // G_A
        for c in range(rows // CHUNK):
            rs = slice(c * CHUNK, (c + 1) * CHUNK)
            s = _dot(ws_ref[G_A - 1], v[rs])
            for g in range(G_A - 2, -1, -1):
                s = jnp.where(lane < (g + 1) * dg, _dot(ws_ref[g], v[rs]), s)
            ya_ref[rs, :] = (u[rs] * (s + sb_ref[...])).astype(BF16)

    def emit_ret_qk(z):
        qr_ref[...] = z[:, :wb].astype(BF16)
        kr_ref[...] = z[:, wb:] * (DK_B ** -0.5)

    def emit_ret_vg(z):
        vr_ref[...] = z[:, :wb].astype(BF16)
        gr_ref[...] = _silu(z[:, wb:])

    cb = 2 * wa
    cc = cb + 4 * wb
    z_a = proj(0, 2 * wa)
    z_q = proj(cc, wc)
    u, v = sgu_prepare(z_a)
    z_k = proj(cc + wc, wc)
    sgu_mix(u, v)
    emit_q(z_q)
    z_v = proj(cc + 2 * wc, wc)
    emit_k(z_k)
    z_r = proj(cb, 2 * wb)
    emit_v(z_v)
    z_g = proj(cb + 2 * wb, 2 * wb)
    emit_ret_qk(z_r)
    emit_ret_vg(z_g)


def _projection(x, mods, mod_row, layer, seq_len, g, w_bf16, sgu_g, sgu_w, sgu_bias, qg, kg, rope):
    t, d = x.shape
    rows = PROJ_ROWS
    wa = sgu_g.shape[2]
    wb = wa
    wc = qg.shape[2]
    has_rope = rope is not None
    tiles_per_seq = max(1, seq_len // rows)
    row_spec = lambda w: pl.BlockSpec((rows, w), lambda i: (i, 0))
    params = [g, w_bf16, sgu_g, sgu_w, sgu_bias, qg, kg]
    in_specs = [row_spec(d), _mod_spec(mods, layer, lambda i: mod_row(i // tiles_per_seq))]
    in_specs += [_layer_spec(p, layer) for p in params]
    args = [x, mods] + params
    outs = [(wa, BF16), (wb, BF16), (wb, F32), (wb, BF16), (wb, F32), (wc, BF16), (wc, BF16), (wc, BF16)]
    out_specs = [row_spec(w) for w, _ in outs]
    out_shape = [jax.ShapeDtypeStruct((t, w), dt) for w, dt in outs]
    if has_rope:
        tab_spec = pl.BlockSpec((rows, LANES), lambda i: (i % tiles_per_seq, 0))
        in_specs += [tab_spec, tab_spec]
        args += list(rope)
    else:
        seqs = rows // seq_len
        out_specs += [pl.BlockSpec((seqs, wc, seq_len), lambda i: (i, 0, 0)),
                      pl.BlockSpec((rows, H_C, LANES), lambda i: (i, 0, 0))]
        out_shape += [jax.ShapeDtypeStruct((t // seq_len, wc, seq_len), F32),
                      jax.ShapeDtypeStruct((t, H_C, LANES), F32)]
    return pl.pallas_call(
        functools.partial(_proj_kernel, has_rope=has_rope, seq_len=seq_len),
        grid=(t // rows,),
        in_specs=in_specs,
        out_specs=out_specs,
        out_shape=out_shape,
        compiler_params=_cparams(("parallel",)),
        name="projection",
    )(*args)


def _log_sigmoid(x):
    y = -x
    return -(jnp.maximum(y, 0.0) + jnp.log1p(jnp.exp(-jnp.abs(y))))


def _ret_kernel(*refs, n_chunks, has_state):
    if has_state:
        (q_ref, k_ref, v_ref, gate_ref, r0f_ref, r0b_ref, lgp_ref, lgh_ref, norm_ref,
         y_ref, rf_ref, rb_ref, d_scr, dec_scr, kv_scr, rs_scr, o_scr) = refs
    else:
        (q_ref, k_ref, v_ref, gate_ref, lgp_ref, lgh_ref, norm_ref,
         y_ref, rf_ref, rb_ref, d_scr, dec_scr, kv_scr, rs_scr, o_scr) = refs
    bt = q_ref.shape[0]
    n_pairs = H_B // 2
    half = LANES // 2
    row = lax.broadcasted_iota(jnp.int32, (CHUNK, LANES), 0).astype(F32)
    col = lax.broadcasted_iota(jnp.int32, (CHUNK, LANES), 1).astype(F32)

    @pl.when(pl.program_id(0) == 0)
    def _build_decay_tables():
        lgp = _log_sigmoid(lgp_ref[...])
        lgh = _log_sigmoid(lgh_ref[...])
        dist = row - col
        for h in range(H_B):
            lf = lgh[h:h + 1, :]
            lb = lgh[H_B + h:H_B + h + 1, :]
            d_f = jnp.where(dist >= 0, jnp.exp(lf * jnp.maximum(dist, 0.0)), 0.0)
            d_b = jnp.where(dist <= 0, jnp.exp(lb * jnp.maximum(-dist, 0.0)), 0.0)
            d_scr[h // 2, (h % 2) * CHUNK:(h % 2 + 1) * CHUNK, :] = d_f + d_b
        for p in range(n_pairs):
            lf = lgp[p:p + 1, :]
            lb = lgp[n_pairs + p:n_pairs + p + 1, :]
            dec_scr[0, p] = jnp.exp(lf * (row + 1.0))
            dec_scr[1, p] = jnp.exp(lf * (CHUNK - 1.0 - row))
            dec_scr[2, p] = jnp.exp(lb * (CHUNK - row))
            dec_scr[3, p] = jnp.exp(lb * row)
            dec_scr[4, p] = jnp.broadcast_to(jnp.exp(lf * float(CHUNK)), (CHUNK, LANES))
            dec_scr[5, p] = jnp.broadcast_to(jnp.exp(lb * float(CHUNK)), (CHUNK, LANES))

    lo = _lo_mask(CHUNK)
    blockdiag = (row < half) == (col < half)

    for b in range(bt):
        def intra(c, carry):
            rows = pl.ds(pl.multiple_of(c * CHUNK, CHUNK), CHUNK)
            qc, kc, vc = q_ref[b, rows, :], k_ref[b, rows, :], v_ref[b, rows, :]
            for p in range(n_pairs):
                sl = slice(p * LANES, (p + 1) * LANES)
                k128, vb = kc[:, sl], vc[:, sl]
                inner = (_dot_nt(_split_halves(qc[:, sl]), k128.astype(BF16)) * d_scr[p]).astype(BF16)
                oo = _dot(inner, vb)
                o_scr[rows, sl] = jnp.where(lo, oo[:CHUNK], oo[CHUNK:])
                kk = jnp.concatenate([k128 * dec_scr[1, p], k128 * dec_scr[3, p]], axis=1).astype(BF16)
                kv = _dot_tn(kk, vb)
                kv_scr[0, c, p] = jnp.where(blockdiag, kv[:LANES], 0.0)
                kv_scr[1, c, p] = jnp.where(blockdiag, kv[LANES:], 0.0)
            return carry

        lax.fori_loop(0, n_chunks, intra, 0, unroll=2)

        for p in range(n_pairs):
            for d, out_ref in enumerate((rf_ref, rb_ref)):
                if has_state:
                    r0_ref = (r0f_ref, r0b_ref)[d]
                    zero = jnp.zeros((half, half), F32)
                    top = jnp.concatenate([r0_ref[b, 2 * p], zero], axis=1)
                    bot = jnp.concatenate([zero, r0_ref[b, 2 * p + 1]], axis=1)
                    r = jnp.concatenate([top, bot], axis=0)
                else:
                    r = jnp.zeros((LANES, LANES), F32)
                order = range(n_chunks) if d == 0 else range(n_chunks - 1, -1, -1)
                for c in order:
                    rs_scr[c, p, :, d * LANES:(d + 1) * LANES] = r.astype(BF16)
                    r = r * dec_scr[4 + d, p] + kv_scr[d, c, p]
                out_ref[b, 2 * p] = r[:half, :half]
                out_ref[b, 2 * p + 1] = r[half:, half:]

        def cross(c, carry):
            rows = pl.ds(pl.multiple_of(c * CHUNK, CHUNK), CHUNK)
            qc = q_ref[b, rows, :]
            outs = []
            for p in range(n_pairs):
                sl = slice(p * LANES, (p + 1) * LANES)
                t = _dot(qc[:, sl], rs_scr[c, p])
                outs.append(o_scr[rows, sl] + t[:, :LANES] * dec_scr[0, p] + t[:, LANES:] * dec_scr[2, p])
            o = _group64_rmsnorm(jnp.concatenate(outs, axis=-1), norm_ref[...])
            y_ref[b, rows, :] = (gate_ref[b, rows, :] * o).astype(BF16)
            return carry

        lax.fori_loop(0, n_chunks, cross, 0, unroll=2)


def _mixer_b(q, k, v, gate, layer, lg_pair, lg_head, norm, state):
    b, l, wb = q.shape
    bt = max(1, RET_ROWS // l)
    n_chunks = l // CHUNK
    has_state = state is not None
    tile = pl.BlockSpec((bt, l, wb), lambda i: (i, 0, 0))
    in_specs = [tile, tile, tile, tile]
    args = [q, k, v, gate]
    if has_state:
        sspec = pl.BlockSpec((bt, None, H_B, DK_B, DK_B), lambda i: (i, layer, 0, 0, 0))
        in_specs += [sspec, sspec]
        args += list(state)
    params = [lg_pair, lg_head, norm]
    in_specs += [_layer_spec(p, layer) for p in params]
    args += params
    rspec = pl.BlockSpec((bt, H_B, DK_B, DK_B), lambda i: (i, 0, 0, 0))
    rshape = jax.ShapeDtypeStruct((b, H_B, DK_B, DK_B), F32)
    n_pairs = H_B // 2
    return pl.pallas_call(
        functools.partial(_ret_kernel, n_chunks=n_chunks, has_state=has_state),
        grid=(b // bt,),
        in_specs=in_specs,
        out_specs=[tile, rspec, rspec],
        out_shape=[jax.ShapeDtypeStruct((b, l, wb), BF16), rshape, rshape],
        scratch_shapes=[
            pltpu.VMEM((n_pairs, 2 * CHUNK, LANES), F32),
            pltpu.VMEM((6, n_pairs, CHUNK, LANES), F32),
            pltpu.VMEM((2, n_chunks, n_pairs, LANES, LANES), F32),
            pltpu.VMEM((n_chunks, n_pairs, LANES, 2 * LANES), BF16),
            pltpu.VMEM((l, wb), F32),
        ],
        compiler_params=_cparams(("arbitrary",)),
        name="retention",
    )(*args)


def _attn_kernel(*refs, has_cache, lam_init):
    if has_cache:
        (q_ref, k_ref, v_ref, ck_ref, cv_ref, dl_ref, dn_ref, y_ref, k_scr, v_scr) = refs
        n_past = ck_ref.shape[1]

        @pl.when(pl.program_id(1) == 0)
        def _gather_keys():
            k_scr[:n_past, :] = ck_ref[...].T.astype(BF16)
            for h in range(H_C):
                v_scr[:n_past, h * LANES:(h + 1) * LANES] = cv_ref[:, h, :].astype(BF16)
            k_scr[n_past:, :] = k_ref[0]
            v_scr[n_past:, :] = v_ref[0]

        keys = lambda sl: k_scr[:, sl]
        vals = lambda sl: v_scr[:, sl]
    else:
        (q_ref, k_ref, v_ref, dl_ref, dn_ref, y_ref) = refs
        keys = lambda sl: k_ref[0, :, sl]
        vals = lambda sl: v_ref[0, :, sl]
    tq = q_ref.shape[1]
    dl = dl_ref[...]
    lam = (jnp.exp(jnp.sum(dl[0:1] * dl[1:2], axis=-1, keepdims=True))
           - jnp.exp(jnp.sum(dl[2:3] * dl[3:4], axis=-1, keepdims=True)) + lam_init)
    head = lambda h: slice(h * LANES, (h + 1) * LANES)
    scores = lambda h: _dot_nt(keys(head(h)), _split_halves(q_ref[0, :, head(h)]))
    st_next = scores(0)
    for h in range(H_C):
        sl = head(h)
        st = st_next
        if h + 1 < H_C:
            st_next = scores(h + 1)
        e = jnp.exp(st - jnp.max(st, axis=0, keepdims=True))
        l = jnp.sum(e, axis=0, keepdims=True)
        w = e * jnp.concatenate([1.0 / l[:, :tq], lam / l[:, tq:]], axis=1)
        o = _dot_tn((w[:, :tq] - w[:, tq:]).astype(BF16), vals(sl))
        y_ref[0, :, sl] = (_rmsnorm(o, dn_ref[...]) * (1.0 - lam_init)).astype(BF16)


def _mixer_c(q, k, v, layer, dl, dn, lam_init, cache):
    b, l, wc = q.shape
    has_cache = cache is not None
    tq = min(l, ATTN_Q_ROWS)
    q_spec = pl.BlockSpec((1, tq, wc), lambda i, j: (i, j, 0))
    kv_spec = pl.BlockSpec((1, l, wc), lambda i, j: (i, 0, 0))
    in_specs = [q_spec, kv_spec, kv_spec]
    args = [q, k, v]
    n_keys = l
    if has_cache:
        ck_t, cv = cache
        n_past = cv.shape[2]
        n_keys += n_past
        in_specs += [pl.BlockSpec((None, None, wc, n_past), lambda i, j: (i, layer, 0, 0)),
                     pl.BlockSpec((None, None, n_past, H_C, LANES), lambda i, j: (i, layer, 0, 0, 0))]
        args += [ck_t, cv]
    in_specs += [_layer_spec(dl, layer), _layer_spec(dn, layer)]
    args += [dl, dn]
    scratch = [pltpu.VMEM((n_keys, wc), BF16), pltpu.VMEM((n_keys, wc), BF16)] if has_cache else []
    return pl.pallas_call(
        functools.partial(_attn_kernel, has_cache=has_cache, lam_init=lam_init),
        grid=(b, l // tq),
        in_specs=in_specs,
        out_specs=q_spec,
        out_shape=jax.ShapeDtypeStruct((b, l, wc), BF16),
        scratch_shapes=scratch,
        compiler_params=_cparams(("parallel", "arbitrary")),
        name="diff_attention",
    )(*args)


def _mlp_kernel(ya_ref, yb_ref, yc_ref, x_ref, mod_ref, g_ref, wo_ref, up_ref, cw_ref, cb_ref, down_ref,
                o_ref, h_scr, g_scr, *, seq_len):
    rows = x_ref.shape[0]
    m = mod_ref[0]
    y = None
    k0 = 0
    for y_ref in (ya_ref, yb_ref, yc_ref):
        kw = y_ref.shape[1]
        part = _dot(y_ref[...], wo_ref[k0:k0 + kw, :])
        y = part if y is None else y + part
        k0 += kw
    x1 = x_ref[...] + m[2:3, :] * y
    o_ref[...] = x1
    h_scr[...] = (_rmsnorm(x1, g_ref[...]) * (1.0 + m[4:5, :]) + m[3:4, :]).astype(BF16)

    d_ff = down_ref.shape[0]
    fc = MXU_DIM
    pos = lax.broadcasted_iota(jnp.int32, (rows, fc), 0) % seq_len
    has_prev = pos > 0
    has_next = pos < seq_len - 1

    def conv(c0):
        u = _dot(h_scr[...], up_ref[:, c0:c0 + fc])
        w = cw_ref[:, c0:c0 + fc]
        prev = jnp.where(has_prev, pltpu.roll(u, 1, 0), 0.0)
        nxt = jnp.where(has_next, pltpu.roll(u, rows - 1, 0), 0.0)
        return prev * w[0:1, :] + u * w[1:2, :] + nxt * w[2:3, :] + cb_ref[:, c0:c0 + fc]

    for j in range(d_ff // fc):
        a = conv(j * fc)
        b = conv(d_ff + j * fc)
        g_scr[:, j * fc:(j + 1) * fc] = (_silu(a) * b).astype(BF16)

    o_ref[...] = o_ref[...] + m[5:6, :] * _dot(g_scr[...], down_ref[...])


def _mlp(ya, yb, yc, x, mods, mod_row, layer, seq_len, g, w_out, up, cw, cb, down):
    t, d = x.shape
    rows = MLP_ROWS
    tiles_per_seq = max(1, seq_len // rows)
    row_spec = lambda a: pl.BlockSpec((rows, a.shape[1]), lambda i: (i, 0))
    weights = [w_out, up, cw, cb, down]
    return pl.pallas_call(
        functools.partial(_mlp_kernel, seq_len=seq_len),
        grid=(t // rows,),
        in_specs=[row_spec(ya), row_spec(yb), row_spec(yc), row_spec(x),
                  _mod_spec(mods, layer, lambda i: mod_row(i // tiles_per_seq)), _layer_spec(g, layer)]
                 + [_layer_spec(w, layer, resident=True) for w in weights],
        out_specs=row_spec(x),
        out_shape=jax.ShapeDtypeStruct((t, d), F32),
        scratch_shapes=[pltpu.VMEM((rows, d), BF16), pltpu.VMEM((rows, down.shape[1]), BF16)],
        compiler_params=_cparams(("parallel",)),
        name="out_proj_mlp",
    )(ya, yb, yc, x, mods, g, *weights)


def _rope_tables(l):
    rows = l // GRID_W
    t_row = jnp.repeat(jnp.arange(rows, dtype=F32), GRID_W)
    t_col = jnp.tile(jnp.arange(GRID_W, dtype=F32), rows)
    inv = ROPE_BASE ** (-jnp.arange(ROPE_PAIRS, dtype=F32) / ROPE_PAIRS)
    ar, ac = t_row[:, None] * inv, t_col[:, None] * inv
    cos = jnp.concatenate([jnp.cos(ar), jnp.cos(ar), jnp.cos(ac), jnp.cos(ac)], axis=-1)
    sin = jnp.concatenate([-jnp.sin(ar), jnp.sin(ar), -jnp.sin(ac), jnp.sin(ac)], axis=-1)
    return jnp.tile(cos, (1, 2)), jnp.tile(sin, (1, 2))


def kernel(x_prompt, x_sample, c, cache_k, cache_v, state_ret_fwd, state_ret_bwd, c_ctx, norm1, w_mod, b_mod,
           w_in, sgu_norm, sgu_w, sgu_b, ret_logit_fwd, ret_logit_bwd, ret_norm, q_norm, k_norm, diff_lam,
           diff_norm, w_out, norm2, ffn_up, ffn_conv, ffn_conv_b, ffn_down):
    depth, d_model, _ = w_mod.shape
    batch, seq, _ = x_prompt.shape
    dec_batch, dec_seq, _ = x_sample.shape
    w_a = sgu_norm.shape[1]
    w_b = H_B * ret_norm.shape[2]
    w_c = 2 * H_C * HD_C
    past = cache_k.shape[2]

    pad = (-(dec_batch + 1)) % 8
    cond = jnp.concatenate([c, c_ctx[None, :], jnp.zeros((pad, d_model), F32)], axis=0)
    mods = _modulation(cond, w_mod, b_mod).reshape(depth, cond.shape[0], 6, d_model)

    w_in_b = w_in.astype(BF16)
    w_out_b = w_out.astype(BF16)
    sgu_w_b = sgu_w.astype(BF16)
    up_b = ffn_up.astype(BF16)
    down_b = ffn_down.astype(BF16)
    row3 = lambda a: a.reshape(depth, 1, -1)
    sgu_bias = jnp.repeat(jnp.swapaxes(sgu_b, 1, 2), w_a // G_A, axis=2)
    logits = jnp.stack([ret_logit_fwd, ret_logit_bwd], axis=1)
    lg_pair = jnp.repeat(logits, DK_B, axis=2).reshape(depth, 2 * (H_B // 2), LANES)
    lg_head = jnp.broadcast_to(logits.reshape(depth, 2 * H_B, 1), (depth, 2 * H_B, LANES))
    qg = row3(jnp.tile(q_norm, (1, 2 * H_C)))
    kg = row3(jnp.tile(k_norm, (1, 2 * H_C)))
    norm1_r, norm2_r, sgu_g, ret_g, diff_g = row3(norm1), row3(norm2), row3(sgu_norm), row3(ret_norm), row3(diff_norm)
    conv_b = row3(ffn_conv_b)
    rope = _rope_tables(dec_seq)
    cache_kt = jnp.transpose(cache_k, (0, 1, 3, 4, 5, 2)).reshape(dec_batch, depth, w_c, past)

    def layer(x, l, nb, sl, mod_row, is_sample):
        lam_init = 0.8 - 0.6 * math.exp(-0.3 * l)
        outs = _projection(x, mods, mod_row, l, sl, norm1_r, w_in_b, sgu_g, sgu_w_b, sgu_bias, qg, kg,
                           rope if is_sample else None)
        ya, qr, kr, vr, gr, qa, ka, va = outs[:8]
        seq3 = lambda a: a.reshape(nb, sl, a.shape[1])
        state = (state_ret_fwd, state_ret_bwd) if is_sample else None
        yb, rf, rb = _mixer_b(seq3(qr), seq3(kr), seq3(vr), seq3(gr), l, lg_pair, lg_head, ret_g, state)
        cache = (cache_kt, cache_v) if is_sample else None
        yc = _mixer_c(seq3(qa), seq3(ka), seq3(va), l, diff_lam, diff_g, lam_init, cache)
        x = _mlp(ya, yb.reshape(nb * sl, w_b), yc.reshape(nb * sl, w_c), x, mods, mod_row, l, sl, norm2_r,
                 w_out_b, up_b, ffn_conv, conv_b, down_b)
        return x, outs[8:], rf, rb

    y_prompt = x_prompt.reshape(batch * seq, d_model)
    kts, vs, rfs, rbs = [], [], [], []
    for l in range(depth):
        y_prompt, (kt_l, v_l), rf_l, rb_l = layer(y_prompt, l, batch, seq, lambda b: dec_batch, False)
        kts.append(kt_l.reshape(batch, H_C, 2, HD_C, seq))
        vs.append(v_l.reshape(batch, seq, H_C, 2 * HD_C))
        rfs.append(rf_l)
        rbs.append(rb_l)

    y_sample = x_sample.reshape(dec_batch * dec_seq, d_model)
    for l in range(depth):
        y_sample, _, _, _ = layer(y_sample, l, dec_batch, dec_seq, lambda b: b, True)

    new_k = jnp.transpose(jnp.stack(kts, axis=1), (0, 1, 5, 2, 3, 4))
    return (y_prompt.reshape(batch, seq, d_model), y_sample.reshape(dec_batch, dec_seq, d_model),
            new_k, jnp.stack(vs, axis=1), jnp.stack(rfs, axis=1), jnp.stack(rbs, axis=1))
```

```python
import functools
import math

import jax
import jax.numpy as jnp
from jax import lax
from jax.experimental import pallas as pl
from jax.experimental.pallas import tpu as pltpu

F32 = jnp.float32
BF16 = jnp.bfloat16

GRID_W = 64
CHUNK = 128
EPS = 1e-6
ROPE_BASE = 10000.0
G_A = 4
H_B = 4
DK_B = 64
H_C = 4
HD_C = 64
ROPE_PAIRS = HD_C // 4
LANES = 128
MXU_DIM = 256
PROJ_ROWS = 512
MLP_ROWS = 1024
ATTN_Q_ROWS = 256
RET_ROWS = 1024
RET_UNROLL = 4
VMEM_LIMIT = 60 * 1024 * 1024


def _cparams(sem):
    return pltpu.CompilerParams(dimension_semantics=sem, vmem_limit_bytes=VMEM_LIMIT)


def _const_spec(shape):
    n = len(shape)
    return pl.BlockSpec(shape, lambda *_: (0,) * n)


def _layer_spec(arr, layer, resident=False):
    n = arr.ndim - 1
    mode = dict(pipeline_mode=pl.Buffered(1)) if resident else {}
    return pl.BlockSpec((None,) + arr.shape[1:], lambda *_: (layer,) + (0,) * n, **mode)


def _silu(x):
    return x * jax.nn.sigmoid(x)


def _dot(a, b):
    return jnp.dot(a, b, preferred_element_type=F32)


def _dot_nt(a, b):
    return lax.dot_general(a, b, (((1,), (1,)), ((), ())), preferred_element_type=F32)


def _dot_tn(a, b):
    return lax.dot_general(a, b, (((0,), (0,)), ((), ())), preferred_element_type=F32)


def _lo_mask(rows):
    return lax.broadcasted_iota(jnp.int32, (rows, LANES), 1) < (LANES // 2)


def _split_halves(x):
    lo = _lo_mask(x.shape[0])
    zero = jnp.zeros_like(x)
    return jnp.concatenate([jnp.where(lo, x, zero), jnp.where(lo, zero, x)], axis=0)


def _group64_mean(x2):
    rows, n = x2.shape
    lo = _lo_mask(rows)
    outs = []
    for j in range(n // LANES):
        blk = x2[:, j * LANES:(j + 1) * LANES]
        s_lo = jnp.sum(jnp.where(lo, blk, 0.0), axis=-1, keepdims=True)
        s_hi = jnp.sum(jnp.where(lo, 0.0, blk), axis=-1, keepdims=True)
        outs.append(jnp.where(lo, s_lo, s_hi))
    out = outs[0] if len(outs) == 1 else jnp.concatenate(outs, axis=-1)
    return out * (1.0 / 64.0)


def _group64_rmsnorm(x, g):
    return x * lax.rsqrt(_group64_mean(x * x) + EPS) * g


def _rmsnorm(x, g):
    return x * lax.rsqrt(jnp.mean(x * x, axis=-1, keepdims=True) + EPS) * g


def _rope(x, cos, sin):
    rows, n = x.shape
    lane = lax.broadcasted_iota(jnp.int32, (rows, LANES), 1)
    first = (lane % (2 * ROPE_PAIRS)) < ROPE_PAIRS
    outs = []
    for j in range(n // LANES):
        blk = x[:, j * LANES:(j + 1) * LANES]
        partner = jnp.where(first, pltpu.roll(blk, LANES - ROPE_PAIRS, 1), pltpu.roll(blk, ROPE_PAIRS, 1))
        outs.append(blk * cos + partner * sin)
    return jnp.concatenate(outs, axis=-1)


def _mod_kernel(cond_ref, w_ref, b_ref, o_ref):
    s = _silu(cond_ref[...]).astype(BF16)
    o_ref[0] = _dot(s, w_ref[0].astype(BF16)) + b_ref[0]


def _modulation(cond, w_mod, b_mod):
    depth, d, n = w_mod.shape
    rows = cond.shape[0]
    tn = n // 4
    return pl.pallas_call(
        _mod_kernel,
        grid=(depth, n // tn),
        in_specs=[
            _const_spec((rows, d)),
            pl.BlockSpec((1, d, tn), lambda l, j: (l, 0, j)),
            pl.BlockSpec((1, 1, tn), lambda l, j: (l, 0, j)),
        ],
        out_specs=pl.BlockSpec((1, rows, tn), lambda l, j: (l, 0, j)),
        out_shape=jax.ShapeDtypeStruct((depth, rows, n), F32),
        compiler_params=_cparams(("parallel", "parallel")),
        name="adaln_mod",
    )(cond, w_mod, b_mod.reshape(depth, 1, n))


def _mod_spec(mods, layer, row_of_step):
    return pl.BlockSpec((None, 1) + mods.shape[2:], lambda i: (layer, row_of_step(i), 0, 0))


def _proj_kernel(*refs, has_rope, seq_len, n_prev):
    if has_rope:
        (x_ref, mod_ref, g_ref, w_ref, sg_ref, ws_ref, sb_ref, qg_ref, kg_ref, cos_ref, sin_ref,
         ya_ref, qr_ref, kr_ref, vr_ref, gr_ref, qa_ref, ka_ref, va_ref) = refs
    elif n_prev:
        (x_ref, mod_ref, g_ref, w_ref, sg_ref, ws_ref, sb_ref, qg_ref, kg_ref, kt_prev_ref, vf_prev_ref,
         ya_ref, qr_ref, kr_ref, vr_ref, gr_ref, qa_ref, ka_ref, va_ref, kt_ref, vf_ref) = refs
        kt_ref[:, :n_prev] = kt_prev_ref[...]
        vf_ref[:, :n_prev] = vf_prev_ref[...]
    else:
        (x_ref, mod_ref, g_ref, w_ref, sg_ref, ws_ref, sb_ref, qg_ref, kg_ref,
         ya_ref, qr_ref, kr_ref, vr_ref, gr_ref, qa_ref, ka_ref, va_ref, kt_ref, vf_ref) = refs
    rows = x_ref.shape[0]
    m = mod_ref[0]
    hb = (_rmsnorm(x_ref[...], g_ref[...]) * (1.0 + m[1:2, :]) + m[0:1, :]).astype(BF16)
    wa = ya_ref.shape[1]
    wb = qr_ref.shape[1]
    wc = qa_ref.shape[1]

    def proj(c0, width):
        return _dot(hb, w_ref[:, c0:c0 + width])

    def emit_q(z):
        qn = _group64_rmsnorm(z, qg_ref[...])
        if has_rope:
            qn = _rope(qn, cos_ref[...], sin_ref[...])
        qa_ref[...] = (qn * (HD_C ** -0.5)).astype(BF16)

    def emit_k(z):
        kn = _group64_rmsnorm(z, kg_ref[...])
        if has_rope:
            kn = _rope(kn, cos_ref[...], sin_ref[...])
        else:
            for s in range(rows // seq_len):
                kt_ref[s, n_prev] = kn[s * seq_len:(s + 1) * seq_len, :].T
        ka_ref[...] = kn.astype(BF16)

    def emit_v(z):
        if not has_rope:
            for s in range(rows // seq_len):
                for h in range(H_C):
                    vf_ref[s, n_prev, :, h, :] = z[s * seq_len:(s + 1) * seq_len, h * LANES:(h + 1) * LANES]
        va_ref[...] = z.astype(BF16)

    def sgu_prepare(z):
        za = jax.nn.gelu(z)
        return za[:, :wa], _rmsnorm(za[:, wa:], sg_ref[...]).astype(BF16)

    def sgu_mix(u, v):
        lane = lax.broadcasted_iota(jnp.int32, (CHUNK, wa), 1)
        dg = wa // G_A
        for c in range(rows // CHUNK):
            rs = slice(c * CHUNK, (c + 1) * CHUNK)
            s = _dot(ws_ref[G_A - 1], v[rs])
            for g in range(G_A - 2, -1, -1):
                s = jnp.where(lane < (g + 1) * dg, _dot(ws_ref[g], v[rs]), s)
            ya_ref[rs, :] = (u[rs] * (s + sb_ref[...])).astype(BF16)

    def emit_ret_qk(z):
        qr_ref[...] = z[:, :wb].astype(BF16)
        kr_ref[...] = z[:, wb:] * (DK_B ** -0.5)

    def emit_ret_vg(z):
        vr_ref[...] = z[:, :wb].astype(BF16)
        gr_ref[...] = _silu(z[:, wb:])

    cb = 2 * wa
    cc = cb + 4 * wb
    z_a = proj(0, 2 * wa)
    z_q = proj(cc, wc)
    u, v = sgu_prepare(z_a)
    z_k = proj(cc + wc, wc)
    sgu_mix(u, v)
    emit_q(z_q)
    z_v = proj(cc + 2 * wc, wc)
    emit_k(z_k)
    z_r = proj(cb, 2 * wb)
    emit_v(z_v)
    z_g = proj(cb + 2 * wb, 2 * wb)
    emit_ret_qk(z_r)
    emit_ret_vg(z_g)


def _stacked_spec(lead, layers, tail):
    return pl.BlockSpec((lead, layers) + tail, lambda i: (i, 0) + (0,) * len(tail))


def _projection(x, mods, mod_row, layer, seq_len, g, w_bf16, sgu_g, sgu_w, sgu_bias, qg, kg, rope, prev_cache):
    t, d = x.shape
    rows = PROJ_ROWS
    wa = sgu_g.shape[2]
    wb = wa
    wc = qg.shape[2]
    has_rope = rope is not None
    tiles_per_seq = max(1, seq_len // rows)
    row_spec = lambda w: pl.BlockSpec((rows, w), lambda i: (i, 0))
    params = [g, w_bf16, sgu_g, sgu_w, sgu_bias, qg, kg]
    in_specs = [row_spec(d), _mod_spec(mods, layer, lambda i: mod_row(i // tiles_per_seq))]
    in_specs += [_layer_spec(p, layer) for p in params]
    args = [x, mods] + params
    outs = [(wa, BF16), (wb, BF16), (wb, F32), (wb, BF16), (wb, F32), (wc, BF16), (wc, BF16), (wc, BF16)]
    out_specs = [row_spec(w) for w, _ in outs]
    out_shape = [jax.ShapeDtypeStruct((t, w), dt) for w, dt in outs]
    if has_rope:
        tab_spec = pl.BlockSpec((rows, LANES), lambda i: (i % tiles_per_seq, 0))
        in_specs += [tab_spec, tab_spec]
        args += list(rope)
    else:
        seqs = rows // seq_len
        k_tail, v_tail = (wc, seq_len), (seq_len, H_C, LANES)
        if layer:
            in_specs += [_stacked_spec(seqs, layer, k_tail), _stacked_spec(seqs, layer, v_tail)]
            args += list(prev_cache)
        out_specs += [_stacked_spec(seqs, layer + 1, k_tail), _stacked_spec(seqs, layer + 1, v_tail)]
        out_shape += [jax.ShapeDtypeStruct((t // seq_len, layer + 1) + k_tail, F32),
                      jax.ShapeDtypeStruct((t // seq_len, layer + 1) + v_tail, F32)]
    return pl.pallas_call(
        functools.partial(_proj_kernel, has_rope=has_rope, seq_len=seq_len, n_prev=0 if has_rope else layer),
        grid=(t // rows,),
        in_specs=in_specs,
        out_specs=out_specs,
        out_shape=out_shape,
        compiler_params=_cparams(("parallel",)),
        name="projection",
    )(*args)


def _log_sigmoid(x):
    y = -x
    return -(jnp.maximum(y, 0.0) + jnp.log1p(jnp.exp(-jnp.abs(y))))


def _ret_kernel(*refs, n_chunks, has_state, n_prev):
    if has_state:
        (q_ref, k_ref, v_ref, gate_ref, r0f_ref, r0b_ref, lgp_ref, lgh_ref, norm_ref,
         y_ref, d_scr, dec_scr, kv_scr, rs_scr, o_scr) = refs
        state_outs = None
    elif n_prev:
        (q_ref, k_ref, v_ref, gate_ref, rf_prev_ref, rb_prev_ref, lgp_ref, lgh_ref, norm_ref,
         y_ref, rf_ref, rb_ref, d_scr, dec_scr, kv_scr, rs_scr, o_scr) = refs
        rf_ref[:, :n_prev] = rf_prev_ref[...]
        rb_ref[:, :n_prev] = rb_prev_ref[...]
        state_outs = (rf_ref, rb_ref)
    else:
        (q_ref, k_ref, v_ref, gate_ref, lgp_ref, lgh_ref, norm_ref,
         y_ref, rf_ref, rb_ref, d_scr, dec_scr, kv_scr, rs_scr, o_scr) = refs
        state_outs = (rf_ref, rb_ref)
    bt = q_ref.shape[0]
    n_pairs = H_B // 2
    half = LANES // 2
    row = lax.broadcasted_iota(jnp.int32, (CHUNK, LANES), 0).astype(F32)
    col = lax.broadcasted_iota(jnp.int32, (CHUNK, LANES), 1).astype(F32)

    @pl.when(pl.program_id(0) == 0)
    def _build_decay_tables():
        lgp = _log_sigmoid(lgp_ref[...])
        lgh = _log_sigmoid(lgh_ref[...])
        dist = row - col
        for h in range(H_B):
            lf = lgh[h:h + 1, :]
            lb = lgh[H_B + h:H_B + h + 1, :]
            d_f = jnp.where(dist >= 0, jnp.exp(lf * jnp.maximum(dist, 0.0)), 0.0)
            d_b = jnp.where(dist <= 0, jnp.exp(lb * jnp.maximum(-dist, 0.0)), 0.0)
            d_scr[h // 2, (h % 2) * CHUNK:(h % 2 + 1) * CHUNK, :] = d_f + d_b
        for p in range(n_pairs):
            lf = lgp[p:p + 1, :]
            lb = lgp[n_pairs + p:n_pairs + p + 1, :]
            dec_scr[0, p] = jnp.exp(lf * (row + 1.0))
            dec_scr[1, p] = jnp.exp(lf * (CHUNK - 1.0 - row))
            dec_scr[2, p] = jnp.exp(lb * (CHUNK - row))
            dec_scr[3, p] = jnp.exp(lb * row)
            dec_scr[4, p] = jnp.broadcast_to(jnp.exp(lf * float(CHUNK)), (CHUNK, LANES))
            dec_scr[5, p] = jnp.broadcast_to(jnp.exp(lb * float(CHUNK)), (CHUNK, LANES))

    lo = _lo_mask(CHUNK)
    blockdiag = (row < half) == (col < half)

    for b in range(bt):
        def intra(c, carry):
            rows = pl.ds(pl.multiple_of(c * CHUNK, CHUNK), CHUNK)
            qc, kc, vc = q_ref[b, rows, :], k_ref[b, rows, :], v_ref[b, rows, :]
            for p in range(n_pairs):
                sl = slice(p * LANES, (p + 1) * LANES)
                k128, vb = kc[:, sl], vc[:, sl]
                inner = (_dot_nt(_split_halves(qc[:, sl]), k128.astype(BF16)) * d_scr[p]).astype(BF16)
                oo = _dot(inner, vb)
                o_scr[rows, sl] = jnp.where(lo, oo[:CHUNK], oo[CHUNK:])
                kk = jnp.concatenate([k128 * dec_scr[1, p], k128 * dec_scr[3, p]], axis=1).astype(BF16)
                kv = _dot_tn(kk, vb)
                kv_scr[0, c, p] = jnp.where(blockdiag, kv[:LANES], 0.0)
                kv_scr[1, c, p] = jnp.where(blockdiag, kv[LANES:], 0.0)
            return carry

        lax.fori_loop(0, n_chunks, intra, 0, unroll=min(n_chunks, RET_UNROLL))

        for p in range(n_pairs):
            for d in range(2):
                if has_state:
                    r0_ref = (r0f_ref, r0b_ref)[d]
                    zero = jnp.zeros((half, half), F32)
                    top = jnp.concatenate([r0_ref[b, 2 * p], zero], axis=1)
                    bot = jnp.concatenate([zero, r0_ref[b, 2 * p + 1]], axis=1)
                    r = jnp.concatenate([top, bot], axis=0)
                else:
                    r = jnp.zeros((LANES, LANES), F32)
                order = range(n_chunks) if d == 0 else range(n_chunks - 1, -1, -1)
                for c in order:
                    rs_scr[c, p, :, d * LANES:(d + 1) * LANES] = r.astype(BF16)
                    r = r * dec_scr[4 + d, p] + kv_scr[d, c, p]
                if state_outs is not None:
                    state_outs[d][b, n_prev, 2 * p] = r[:half, :half]
                    state_outs[d][b, n_prev, 2 * p + 1] = r[half:, half:]

        def cross(c, carry):
            rows = pl.ds(pl.multiple_of(c * CHUNK, CHUNK), CHUNK)
            qc = q_ref[b, rows, :]
            outs = []
            for p in range(n_pairs):
                sl = slice(p * LANES, (p + 1) * LANES)
                t = _dot(qc[:, sl], rs_scr[c, p])
                outs.append(o_scr[rows, sl] + t[:, :LANES] * dec_scr[0, p] + t[:, LANES:] * dec_scr[2, p])
            o = _group64_rmsnorm(jnp.concatenate(outs, axis=-1), norm_ref[...])
            y_ref[b, rows, :] = (gate_ref[b, rows, :] * o).astype(BF16)
            return carry

        lax.fori_loop(0, n_chunks, cross, 0, unroll=min(n_chunks, RET_UNROLL))


def _mixer_b(q, k, v, gate, layer, lg_pair, lg_head, norm, state, prev_states):
    b, l, wb = q.shape
    bt = max(1, RET_ROWS // l)
    n_chunks = l // CHUNK
    has_state = state is not None
    tile = pl.BlockSpec((bt, l, wb), lambda i: (i, 0, 0))
    in_specs = [tile, tile, tile, tile]
    args = [q, k, v, gate]
    out_specs = [tile]
    out_shape = [jax.ShapeDtypeStruct((b, l, wb), BF16)]
    s_tail = (H_B, DK_B, DK_B)
    if has_state:
        sspec = pl.BlockSpec((bt, None) + s_tail, lambda i: (i, layer, 0, 0, 0))
        in_specs += [sspec, sspec]
        args += list(state)
    else:
        if layer:
            in_specs += [_stacked_spec(bt, layer, s_tail)] * 2
            args += list(prev_states)
        out_specs += [_stacked_spec(bt, layer + 1, s_tail)] * 2
        out_shape += [jax.ShapeDtypeStruct((b, layer + 1) + s_tail, F32)] * 2
    params = [lg_pair, lg_head, norm]
    in_specs += [_layer_spec(p, layer) for p in params]
    args += params
    n_pairs = H_B // 2
    return pl.pallas_call(
        functools.partial(_ret_kernel, n_chunks=n_chunks, has_state=has_state, n_prev=0 if has_state else layer),
        grid=(b // bt,),
        in_specs=in_specs,
        out_specs=out_specs,
        out_shape=out_shape,
        scratch_shapes=[
            pltpu.VMEM((n_pairs, 2 * CHUNK, LANES), F32),
            pltpu.VMEM((6, n_pairs, CHUNK, LANES), F32),
            pltpu.VMEM((2, n_chunks, n_pairs, LANES, LANES), F32),
            pltpu.VMEM((n_chunks, n_pairs, LANES, 2 * LANES), BF16),
            pltpu.VMEM((l, wb), F32),
        ],
        compiler_params=_cparams(("arbitrary",)),
        name="retention",
    )(*args)


def _attn_kernel(*refs, has_cache, lam_init):
    if has_cache:
        (q_ref, k_ref, v_ref, ck_ref, cv_ref, dl_ref, dn_ref, y_ref, k_scr, v_scr) = refs
        n_past = ck_ref.shape[1]

        @pl.when(pl.program_id(1) == 0)
        def _gather_keys():
            k_scr[:n_past, :] = ck_ref[...].T.astype(BF16)
            for h in range(H_C):
                v_scr[:n_past, h * LANES:(h + 1) * LANES] = cv_ref[:, h, :].astype(BF16)
            k_scr[n_past:, :] = k_ref[0]
            v_scr[n_past:, :] = v_ref[0]

        keys = lambda sl: k_scr[:, sl]
        vals = lambda sl: v_scr[:, sl]
    else:
        (q_ref, k_ref, v_ref, dl_ref, dn_ref, y_ref) = refs
        keys = lambda sl: k_ref[0, :, sl]
        vals = lambda sl: v_ref[0, :, sl]
    tq = q_ref.shape[1]
    dl = dl_ref[...]
    lam = (jnp.exp(jnp.sum(dl[0:1] * dl[1:2], axis=-1, keepdims=True))
           - jnp.exp(jnp.sum(dl[2:3] * dl[3:4], axis=-1, keepdims=True)) + lam_init)
    head = lambda h: slice(h * LANES, (h + 1) * LANES)
    scores = lambda h: _dot_nt(keys(head(h)), _split_halves(q_ref[0, :, head(h)]))
    st_next = scores(0)
    for h in range(H_C):
        sl = head(h)
        st = st_next
        if h + 1 < H_C:
            st_next = scores(h + 1)
        e = jnp.exp(st - jnp.max(st, axis=0, keepdims=True))
        l = jnp.sum(e, axis=0, keepdims=True)
        w = e * jnp.concatenate([1.0 / l[:, :tq], lam / l[:, tq:]], axis=1)
        o = _dot_tn((w[:, :tq] - w[:, tq:]).astype(BF16), vals(sl))
        y_ref[0, :, sl] = (_rmsnorm(o, dn_ref[...]) * (1.0 - lam_init)).astype(BF16)


def _mixer_c(q, k, v, layer, dl, dn, lam_init, cache):
    b, l, wc = q.shape
    has_cache = cache is not None
    tq = min(l, ATTN_Q_ROWS)
    q_spec = pl.BlockSpec((1, tq, wc), lambda i, j: (i, j, 0))
    kv_spec = pl.BlockSpec((1, l, wc), lambda i, j: (i, 0, 0))
    in_specs = [q_spec, kv_spec, kv_spec]
    args = [q, k, v]
    n_keys = l
    if has_cache:
        ck_t, cv = cache
        n_past = cv.shape[2]
        n_keys += n_past
        in_specs += [pl.BlockSpec((None, None, wc, n_past), lambda i, j: (i, layer, 0, 0)),
                     pl.BlockSpec((None, None, n_past, H_C, LANES), lambda i, j: (i, layer, 0, 0, 0))]
        args += [ck_t, cv]
    in_specs += [_layer_spec(dl, layer), _layer_spec(dn, layer)]
    args += [dl, dn]
    scratch = [pltpu.VMEM((n_keys, wc), BF16), pltpu.VMEM((n_keys, wc), BF16)] if has_cache else []
    return pl.pallas_call(
        functools.partial(_attn_kernel, has_cache=has_cache, lam_init=lam_init),
        grid=(b, l // tq),
        in_specs=in_specs,
        out_specs=q_spec,
        out_shape=jax.ShapeDtypeStruct((b, l, wc), BF16),
        scratch_shapes=scratch,
        compiler_params=_cparams(("parallel", "arbitrary")),
        name="diff_attention",
    )(*args)


def _mlp_kernel(ya_ref, yb_ref, yc_ref, x_ref, mod_ref, g_ref, wo_ref, up_ref, cw_ref, cb_ref, down_ref,
                o_ref, h_scr, g_scr, *, seq_len):
    rows = x_ref.shape[0]
    m = mod_ref[0]
    y = None
    k0 = 0
    for y_ref in (ya_ref, yb_ref, yc_ref):
        kw = y_ref.shape[1]
        part = _dot(y_ref[...], wo_ref[k0:k0 + kw, :])
        y = part if y is None else y + part
        k0 += kw
    x1 = x_ref[...] + m[2:3, :] * y
    o_ref[...] = x1
    h_scr[...] = (_rmsnorm(x1, g_ref[...]) * (1.0 + m[4:5, :]) + m[3:4, :]).astype(BF16)

    d_ff = down_ref.shape[0]
    fc = MXU_DIM
    pos = lax.broadcasted_iota(jnp.int32, (rows, fc), 0) % seq_len
    has_prev = pos > 0
    has_next = pos < seq_len - 1

    def conv(c0):
        u = _dot(h_scr[...], up_ref[:, c0:c0 + fc])
        w = cw_ref[:, c0:c0 + fc]
        prev = jnp.where(has_prev, pltpu.roll(u, 1, 0), 0.0)
        nxt = jnp.where(has_next, pltpu.roll(u, rows - 1, 0), 0.0)
        return prev * w[0:1, :] + u * w[1:2, :] + nxt * w[2:3, :] + cb_ref[:, c0:c0 + fc]

    for j in range(d_ff // fc):
        a = conv(j * fc)
        b = conv(d_ff + j * fc)
        g_scr[:, j * fc:(j + 1) * fc] = (_silu(a) * b).astype(BF16)

    o_ref[...] = o_ref[...] + m[5:6, :] * _dot(g_scr[...], down_ref[...])


def _mlp(ya, yb, yc, x, mods, mod_row, layer, seq_len, g, w_out, up, cw, cb, down):
    t, d = x.shape
    rows = MLP_ROWS
    tiles_per_seq = max(1, seq_len // rows)
    row_spec = lambda a: pl.BlockSpec((rows, a.shape[1]), lambda i: (i, 0))
    weights = [w_out, up, cw, cb, down]
    return pl.pallas_call(
        functools.partial(_mlp_kernel, seq_len=seq_len),
        grid=(t // rows,),
        in_specs=[row_spec(ya), row_spec(yb), row_spec(yc), row_spec(x),
                  _mod_spec(mods, layer, lambda i: mod_row(i // tiles_per_seq)), _layer_spec(g, layer)]
                 + [_layer_spec(w, layer, resident=True) for w in weights],
        out_specs=row_spec(x),
        out_shape=jax.ShapeDtypeStruct((t, d), F32),
        scratch_shapes=[pltpu.VMEM((rows, d), BF16), pltpu.VMEM((rows, down.shape[1]), BF16)],
        compiler_params=_cparams(("parallel",)),
        name="out_proj_mlp",
    )(ya, yb, yc, x, mods, g, *weights)


def _rope_tables(l):
    rows = l // GRID_W
    t_row = jnp.repeat(jnp.arange(rows, dtype=F32), GRID_W)
    t_col = jnp.tile(jnp.arange(GRID_W, dtype=F32), rows)
    inv = ROPE_BASE ** (-jnp.arange(ROPE_PAIRS, dtype=F32) / ROPE_PAIRS)
    ar, ac = t_row[:, None] * inv, t_col[:, None] * inv
    cos = jnp.concatenate([jnp.cos(ar), jnp.cos(ar), jnp.cos(ac), jnp.cos(ac)], axis=-1)
    sin = jnp.concatenate([-jnp.sin(ar), jnp.sin(ar), -jnp.sin(ac), jnp.sin(ac)], axis=-1)
    return jnp.tile(cos, (1, 2)), jnp.tile(sin, (1, 2))


def kernel(x_prompt, x_sample, c, cache_k, cache_v, state_ret_fwd, state_ret_bwd, c_ctx, norm1, w_mod, b_mod,
           w_in, sgu_norm, sgu_w, sgu_b, ret_logit_fwd, ret_logit_bwd, ret_norm, q_norm, k_norm, diff_lam,
           diff_norm, w_out, norm2, ffn_up, ffn_conv, ffn_conv_b, ffn_down):
    depth, d_model, _ = w_mod.shape
    batch, seq, _ = x_prompt.shape
    dec_batch, dec_seq, _ = x_sample.shape
    w_a = sgu_norm.shape[1]
    w_b = H_B * ret_norm.shape[2]
    w_c = 2 * H_C * HD_C
    past = cache_k.shape[2]

    pad = (-(dec_batch + 1)) % 8
    cond = jnp.concatenate([c, c_ctx[None, :], jnp.zeros((pad, d_model), F32)], axis=0)
    mods = _modulation(cond, w_mod, b_mod).reshape(depth, cond.shape[0], 6, d_model)

    w_in_b = w_in.astype(BF16)
    w_out_b = w_out.astype(BF16)
    sgu_w_b = sgu_w.astype(BF16)
    up_b = ffn_up.astype(BF16)
    down_b = ffn_down.astype(BF16)
    row3 = lambda a: a.reshape(depth, 1, -1)
    sgu_bias = jnp.repeat(jnp.swapaxes(sgu_b, 1, 2), w_a // G_A, axis=2)
    logits = jnp.stack([ret_logit_fwd, ret_logit_bwd], axis=1)
    lg_pair = jnp.repeat(logits, DK_B, axis=2).reshape(depth, 2 * (H_B // 2), LANES)
    lg_head = jnp.broadcast_to(logits.reshape(depth, 2 * H_B, 1), (depth, 2 * H_B, LANES))
    qg = row3(jnp.tile(q_norm, (1, 2 * H_C)))
    kg = row3(jnp.tile(k_norm, (1, 2 * H_C)))
    norm1_r, norm2_r, sgu_g, ret_g, diff_g = row3(norm1), row3(norm2), row3(sgu_norm), row3(ret_norm), row3(diff_norm)
    conv_b = row3(ffn_conv_b)
    rope = _rope_tables(dec_seq)
    cache_kt = jnp.transpose(cache_k, (0, 1, 3, 4, 5, 2)).reshape(dec_batch, depth, w_c, past)

    def layer(x, l, nb, sl, mod_row, is_sample, new_cache, new_states):
        lam_init = 0.8 - 0.6 * math.exp(-0.3 * l)
        outs = _projection(x, mods, mod_row, l, sl, norm1_r, w_in_b, sgu_g, sgu_w_b, sgu_bias, qg, kg,
                           rope if is_sample else None, new_cache)
        ya, qr, kr, vr, gr, qa, ka, va = outs[:8]
        seq3 = lambda a: a.reshape(nb, sl, a.shape[1])
        state = (state_ret_fwd, state_ret_bwd) if is_sample else None
        yb, *states = _mixer_b(seq3(qr), seq3(kr), seq3(vr), seq3(gr), l, lg_pair, lg_head, ret_g, state,
                               new_states)
        cache = (cache_kt, cache_v) if is_sample else None
        yc = _mixer_c(seq3(qa), seq3(ka), seq3(va), l, diff_lam, diff_g, lam_init, cache)
        x = _mlp(ya, yb.reshape(nb * sl, w_b), yc.reshape(nb * sl, w_c), x, mods, mod_row, l, sl, norm2_r,
                 w_out_b, up_b, ffn_conv, conv_b, down_b)
        return x, outs[8:], states

    y_prompt = x_prompt.reshape(batch * seq, d_model)
    new_cache, new_states = None, None
    for l in range(depth):
        y_prompt, new_cache, new_states = layer(y_prompt, l, batch, seq, lambda b: dec_batch, False,
                                                new_cache, new_states)

    y_sample = x_sample.reshape(dec_batch * dec_seq, d_model)
    for l in range(depth):
        y_sample, _, _ = layer(y_sample, l, dec_batch, dec_seq, lambda b: b, True, None, None)

    new_kt, new_v = new_cache
    new_k = jnp.transpose(new_kt.reshape(batch, depth, H_C, 2, HD_C, seq), (0, 1, 5, 2, 3, 4))
    return (y_prompt.reshape(batch, seq, d_model), y_sample.reshape(dec_batch, dec_seq, d_model),
            new_k, new_v, new_states[0], new_states[1])
```

```python
import functools
import math

import jax
import jax.numpy as jnp
from jax import lax
from jax.experimental import pallas as pl
from jax.experimental.pallas import tpu as pltpu

F32 = jnp.float32
BF16 = jnp.bfloat16

GRID_W = 64
CHUNK = 128
EPS = 1e-6
ROPE_BASE = 10000.0
G_A = 4
H_B = 4
DK_B = 64
H_C = 4
HD_C = 64
ROPE_PAIRS = HD_C // 4
LANES = 128
MXU_DIM = 256
PROJ_ROWS = 512
MLP_ROWS = 1024
ATTN_Q_ROWS = 256
ATTN_SEQ_ROWS = 1024
RET_ROWS = 1024
RET_CHUNK = 256
RET_UNROLL = 4
VMEM_LIMIT = 60 * 1024 * 1024


def _cparams(sem):
    return pltpu.CompilerParams(dimension_semantics=sem, vmem_limit_bytes=VMEM_LIMIT)


def _const_spec(shape):
    n = len(shape)
    return pl.BlockSpec(shape, lambda *_: (0,) * n)


def _layer_spec(arr, layer, resident=False):
    n = arr.ndim - 1
    mode = dict(pipeline_mode=pl.Buffered(1)) if resident else {}
    return pl.BlockSpec((None,) + arr.shape[1:], lambda *_: (layer,) + (0,) * n, **mode)


def _silu(x):
    return x * jax.nn.sigmoid(x)


def _dot(a, b):
    return jnp.dot(a, b, preferred_element_type=F32)


def _dot_nt(a, b):
    return lax.dot_general(a, b, (((1,), (1,)), ((), ())), preferred_element_type=F32)


def _dot_tn(a, b):
    return lax.dot_general(a, b, (((0,), (0,)), ((), ())), preferred_element_type=F32)


def _lo_mask(rows):
    return lax.broadcasted_iota(jnp.int32, (rows, LANES), 1) < (LANES // 2)


def _split_halves(x):
    lo = _lo_mask(x.shape[0])
    zero = jnp.zeros_like(x)
    return jnp.concatenate([jnp.where(lo, x, zero), jnp.where(lo, zero, x)], axis=0)


def _group64_mean(x2):
    rows, n = x2.shape
    lo = _lo_mask(rows)
    outs = []
    for j in range(n // LANES):
        blk = x2[:, j * LANES:(j + 1) * LANES]
        s_lo = jnp.sum(jnp.where(lo, blk, 0.0), axis=-1, keepdims=True)
        s_hi = jnp.sum(jnp.where(lo, 0.0, blk), axis=-1, keepdims=True)
        outs.append(jnp.where(lo, s_lo, s_hi))
    out = outs[0] if len(outs) == 1 else jnp.concatenate(outs, axis=-1)
    return out * (1.0 / 64.0)


def _group64_rmsnorm(x, g):
    return x * lax.rsqrt(_group64_mean(x * x) + EPS) * g


def _rmsnorm(x, g):
    return x * lax.rsqrt(jnp.mean(x * x, axis=-1, keepdims=True) + EPS) * g


def _rope(x, cos, sin):
    rows, n = x.shape
    lane = lax.broadcasted_iota(jnp.int32, (rows, LANES), 1)
    first = (lane % (2 * ROPE_PAIRS)) < ROPE_PAIRS
    outs = []
    for j in range(n // LANES):
        blk = x[:, j * LANES:(j + 1) * LANES]
        partner = jnp.where(first, pltpu.roll(blk, LANES - ROPE_PAIRS, 1), pltpu.roll(blk, ROPE_PAIRS, 1))
        outs.append(blk * cos + partner * sin)
    return jnp.concatenate(outs, axis=-1)


def _mod_kernel(cond_ref, w_ref, b_ref, o_ref):
    s = _silu(cond_ref[...]).astype(BF16)
    o_ref[0] = _dot(s, w_ref[0].astype(BF16)) + b_ref[0]


def _modulation(cond, w_mod, b_mod):
    depth, d, n = w_mod.shape
    rows = cond.shape[0]
    tn = n // 4
    return pl.pallas_call(
        _mod_kernel,
        grid=(depth, n // tn),
        in_specs=[
            _const_spec((rows, d)),
            pl.BlockSpec((1, d, tn), lambda l, j: (l, 0, j)),
            pl.BlockSpec((1, 1, tn), lambda l, j: (l, 0, j)),
        ],
        out_specs=pl.BlockSpec((1, rows, tn), lambda l, j: (l, 0, j)),
        out_shape=jax.ShapeDtypeStruct((depth, rows, n), F32),
        compiler_params=_cparams(("parallel", "parallel")),
        name="adaln_mod",
    )(cond, w_mod, b_mod.reshape(depth, 1, n))


def _mod_spec(mods, layer, row_of_step):
    return pl.BlockSpec((None, 1) + mods.shape[2:], lambda i: (layer, row_of_step(i), 0, 0))


def _proj_kernel(*refs, has_rope, seq_len, n_prev):
    if has_rope:
        (x_ref, mod_ref, g_ref, w_ref, sg_ref, ws_ref, sb_ref, qg_ref, kg_ref, cos_ref, sin_ref,
         ya_ref, qr_ref, kr_ref, vr_ref, gr_ref, qa_ref, ka_ref, va_ref) = refs
    elif n_prev:
        (x_ref, mod_ref, g_ref, w_ref, sg_ref, ws_ref, sb_ref, qg_ref, kg_ref, kt_prev_ref, vf_prev_ref,
         ya_ref, qr_ref, kr_ref, vr_ref, gr_ref, qa_ref, ka_ref, va_ref, kt_ref, vf_ref) = refs
        kt_ref[:, :n_prev] = kt_prev_ref[...]
        vf_ref[:, :n_prev] = vf_prev_ref[...]
    else:
        (x_ref, mod_ref, g_ref, w_ref, sg_ref, ws_ref, sb_ref, qg_ref, kg_ref,
         ya_ref, qr_ref, kr_ref, vr_ref, gr_ref, qa_ref, ka_ref, va_ref, kt_ref, vf_ref) = refs
    rows = x_ref.shape[0]
    m = mod_ref[0]
    hb = (_rmsnorm(x_ref[...], g_ref[...]) * (1.0 + m[1:2, :]) + m[0:1, :]).astype(BF16)
    wa = ya_ref.shape[1]
    wb = qr_ref.shape[1]
    wc = qa_ref.shape[1]

    def proj(c0, width):
        return _dot(hb, w_ref[:, c0:c0 + width])

    def emit_q(z):
        qn = _group64_rmsnorm(z, qg_ref[...])
        if has_rope:
            qn = _rope(qn, cos_ref[...], sin_ref[...])
        qa_ref[...] = (qn * (HD_C ** -0.5)).astype(BF16)

    def emit_k(z):
        kn = _group64_rmsnorm(z, kg_ref[...])
        if has_rope:
            kn = _rope(kn, cos_ref[...], sin_ref[...])
        else:
            for s in range(rows // seq_len):
                kt_ref[s, n_prev] = kn[s * seq_len:(s + 1) * seq_len, :].T
        ka_ref[...] = kn.astype(BF16)

    def emit_v(z):
        if not has_rope:
            for s in range(rows // seq_len):
                for h in range(H_C):
                    vf_ref[s, n_prev, :, h, :] = z[s * seq_len:(s + 1) * seq_len, h * LANES:(h + 1) * LANES]
        va_ref[...] = z.astype(BF16)

    def sgu_prepare(z):
        za = jax.nn.gelu(z)
        return za[:, :wa], _rmsnorm(za[:, wa:], sg_ref[...]).astype(BF16)

    def sgu_mix(u, v):
        lane = lax.broadcasted_iota(jnp.int32, (CHUNK, wa), 1)
        dg = wa // G_A
        for c in range(rows // CHUNK):
            rs = slice(c * CHUNK, (c + 1) * CHUNK)
            s = _dot(ws_ref[G_A - 1], v[rs])
            for g in range(G_A - 2, -1, -1):
                s = jnp.where(lane < (g + 1) * dg, _dot(ws_ref[g], v[rs]), s)
            ya_ref[rs, :] = (u[rs] * (s + sb_ref[...])).astype(BF16)

    def emit_ret_qk(z):
        qr_ref[...] = z[:, :wb].astype(BF16)
        kr_ref[...] = z[:, wb:] * (DK_B ** -0.5)

    def emit_ret_vg(z):
        vr_ref[...] = z[:, :wb].astype(BF16)
        gr_ref[...] = _silu(z[:, wb:])

    cb = 2 * wa
    cc = cb + 4 * wb
    z_a = proj(0, 2 * wa)
    z_q = proj(cc, wc)
    u, v = sgu_prepare(z_a)
    z_k = proj(cc + wc, wc)
    sgu_mix(u, v)
    emit_q(z_q)
    z_v = proj(cc + 2 * wc, wc)
    emit_k(z_k)
    z_r = proj(cb, 2 * wb)
    emit_v(z_v)
    z_g = proj(cb + 2 * wb, 2 * wb)
    emit_ret_qk(z_r)
    emit_ret_vg(z_g)


def _stacked_spec(lead, layers, tail):
    return pl.BlockSpec((lead, layers) + tail, lambda i: (i, 0) + (0,) * len(tail))


def _projection(x, mods, mod_row, layer, seq_len, g, w_bf16, sgu_g, sgu_w, sgu_bias, qg, kg, rope, prev_cache):
    t, d = x.shape
    rows = 2 * PROJ_ROWS if rope is not None else PROJ_ROWS
    wa = sgu_g.shape[2]
    wb = wa
    wc = qg.shape[2]
    has_rope = rope is not None
    tiles_per_seq = max(1, seq_len // rows)
    row_spec = lambda w: pl.BlockSpec((rows, w), lambda i: (i, 0))
    params = [g, w_bf16, sgu_g, sgu_w, sgu_bias, qg, kg]
    in_specs = [row_spec(d), _mod_spec(mods, layer, lambda i: mod_row(i // tiles_per_seq))]
    in_specs += [_layer_spec(p, layer) for p in params]
    args = [x, mods] + params
    outs = [(wa, BF16), (wb, BF16), (wb, F32), (wb, BF16), (wb, F32), (wc, BF16), (wc, BF16), (wc, BF16)]
    out_specs = [row_spec(w) for w, _ in outs]
    out_shape = [jax.ShapeDtypeStruct((t, w), dt) for w, dt in outs]
    if has_rope:
        tab_spec = pl.BlockSpec((rows, LANES), lambda i: (i % tiles_per_seq, 0))
        in_specs += [tab_spec, tab_spec]
        args += list(rope)
    else:
        seqs = rows // seq_len
        k_tail, v_tail = (wc, seq_len), (seq_len, H_C, LANES)
        if layer:
            in_specs += [_stacked_spec(seqs, layer, k_tail), _stacked_spec(seqs, layer, v_tail)]
            args += list(prev_cache)
        out_specs += [_stacked_spec(seqs, layer + 1, k_tail), _stacked_spec(seqs, layer + 1, v_tail)]
        out_shape += [jax.ShapeDtypeStruct((t // seq_len, layer + 1) + k_tail, F32),
                      jax.ShapeDtypeStruct((t // seq_len, layer + 1) + v_tail, F32)]
    return pl.pallas_call(
        functools.partial(_proj_kernel, has_rope=has_rope, seq_len=seq_len, n_prev=0 if has_rope else layer),
        grid=(t // rows,),
        in_specs=in_specs,
        out_specs=out_specs,
        out_shape=out_shape,
        compiler_params=_cparams(("parallel",)),
        name="projection",
    )(*args)


def _log_sigmoid(x):
    y = -x
    return -(jnp.maximum(y, 0.0) + jnp.log1p(jnp.exp(-jnp.abs(y))))


def _ret_kernel(*refs, chunk, n_chunks, has_state, n_prev):
    if has_state:
        (q_ref, k_ref, v_ref, gate_ref, r0f_ref, r0b_ref, lgp_ref, lgh_ref, norm_ref,
         y_ref, d_scr, dec_scr, kv_scr, rs_scr, o_scr) = refs
        state_outs = None
    elif n_prev:
        (q_ref, k_ref, v_ref, gate_ref, rf_prev_ref, rb_prev_ref, lgp_ref, lgh_ref, norm_ref,
         y_ref, rf_ref, rb_ref, d_scr, dec_scr, kv_scr, rs_scr, o_scr) = refs
        rf_ref[:, :n_prev] = rf_prev_ref[...]
        rb_ref[:, :n_prev] = rb_prev_ref[...]
        state_outs = (rf_ref, rb_ref)
    else:
        (q_ref, k_ref, v_ref, gate_ref, lgp_ref, lgh_ref, norm_ref,
         y_ref, rf_ref, rb_ref, d_scr, dec_scr, kv_scr, rs_scr, o_scr) = refs
        state_outs = (rf_ref, rb_ref)
    bt = q_ref.shape[0]
    n_pairs = H_B // 2
    half = LANES // 2
    cs = chunk

    @pl.when(pl.program_id(0) == 0)
    def _build_decay_tables():
        lgp = _log_sigmoid(lgp_ref[...])
        lgh = _log_sigmoid(lgh_ref[...])
        dist = (lax.broadcasted_iota(jnp.int32, (cs, cs), 0) - lax.broadcasted_iota(jnp.int32, (cs, cs), 1)).astype(F32)
        for h in range(H_B):
            lf = lgh[h:h + 1, :]
            lb = lgh[H_B + h:H_B + h + 1, :]
            d_f = jnp.where(dist >= 0, jnp.exp(lf * jnp.maximum(dist, 0.0)), 0.0)
            d_b = jnp.where(dist <= 0, jnp.exp(lb * jnp.maximum(-dist, 0.0)), 0.0)
            d_scr[h // 2, (h % 2) * cs:(h % 2 + 1) * cs, :] = d_f + d_b
        pos = lax.broadcasted_iota(jnp.int32, (cs, LANES), 0).astype(F32)
        for p in range(n_pairs):
            lf = lgp[p:p + 1, :]
            lb = lgp[n_pairs + p:n_pairs + p + 1, :]
            dec_scr[0, p] = jnp.exp(lf * (pos + 1.0))
            dec_scr[1, p] = jnp.exp(lf * (cs - 1.0 - pos))
            dec_scr[2, p] = jnp.exp(lb * (cs - pos))
            dec_scr[3, p] = jnp.exp(lb * pos)
            dec_scr[4, p] = jnp.broadcast_to(jnp.exp(lf * float(cs)), (cs, LANES))
            dec_scr[5, p] = jnp.broadcast_to(jnp.exp(lb * float(cs)), (cs, LANES))

    lo = _lo_mask(cs)
    blockdiag = ((lax.broadcasted_iota(jnp.int32, (LANES, LANES), 0) < half)
                 == (lax.broadcasted_iota(jnp.int32, (LANES, LANES), 1) < half))

    for b in range(bt):
        def intra(c, carry):
            rows = pl.ds(pl.multiple_of(c * cs, cs), cs)
            qc, kc, vc = q_ref[b, rows, :], k_ref[b, rows, :], v_ref[b, rows, :]
            for p in range(n_pairs):
                sl = slice(p * LANES, (p + 1) * LANES)
                k128, vb = kc[:, sl], vc[:, sl]
                inner = (_dot_nt(_split_halves(qc[:, sl]), k128.astype(BF16)) * d_scr[p]).astype(BF16)
                oo = _dot(inner, vb)
                o_scr[rows, sl] = jnp.where(lo, oo[:cs], oo[cs:])
                kk = jnp.concatenate([k128 * dec_scr[1, p], k128 * dec_scr[3, p]], axis=1).astype(BF16)
                kv = _dot_tn(kk, vb)
                kv_scr[0, c, p] = jnp.where(blockdiag, kv[:LANES], 0.0)
                kv_scr[1, c, p] = jnp.where(blockdiag, kv[LANES:], 0.0)
            return carry

        lax.fori_loop(0, n_chunks, intra, 0, unroll=min(n_chunks, RET_UNROLL))

        for p in range(n_pairs):
            for d in range(2):
                if has_state:
                    r0_ref = (r0f_ref, r0b_ref)[d]
                    zero = jnp.zeros((half, half), F32)
                    top = jnp.concatenate([r0_ref[b, 2 * p], zero], axis=1)
                    bot = jnp.concatenate([zero, r0_ref[b, 2 * p + 1]], axis=1)
                    r = jnp.concatenate([top, bot], axis=0)
                else:
                    r = jnp.zeros((LANES, LANES), F32)
                order = range(n_chunks) if d == 0 else range(n_chunks - 1, -1, -1)
                for c in order:
                    rs_scr[c, p, :, d * LANES:(d + 1) * LANES] = r.astype(BF16)
                    r = r * dec_scr[4 + d, p, :LANES, :] + kv_scr[d, c, p]
                if state_outs is not None:
                    state_outs[d][b, n_prev, 2 * p] = r[:half, :half]
                    state_outs[d][b, n_prev, 2 * p + 1] = r[half:, half:]

        def cross(c, carry):
            rows = pl.ds(pl.multiple_of(c * cs, cs), cs)
            qc = q_ref[b, rows, :]
            outs = []
            for p in range(n_pairs):
                sl = slice(p * LANES, (p + 1) * LANES)
                t = _dot(qc[:, sl], rs_scr[c, p])
                outs.append(o_scr[rows, sl] + t[:, :LANES] * dec_scr[0, p] + t[:, LANES:] * dec_scr[2, p])
            o = _group64_rmsnorm(jnp.concatenate(outs, axis=-1), norm_ref[...])
            y_ref[b, rows, :] = (gate_ref[b, rows, :] * o).astype(BF16)
            return carry

        lax.fori_loop(0, n_chunks, cross, 0, unroll=min(n_chunks, RET_UNROLL))


def _mixer_b(q, k, v, gate, layer, lg_pair, lg_head, norm, state, prev_states):
    b, l, wb = q.shape
    bt = max(1, RET_ROWS // l)
    chunk = min(l, RET_CHUNK)
    n_chunks = l // chunk
    has_state = state is not None
    tile = pl.BlockSpec((bt, l, wb), lambda i: (i, 0, 0))
    in_specs = [tile, tile, tile, tile]
    args = [q, k, v, gate]
    out_specs = [tile]
    out_shape = [jax.ShapeDtypeStruct((b, l, wb), BF16)]
    s_tail = (H_B, DK_B, DK_B)
    if has_state:
        sspec = pl.BlockSpec((bt, None) + s_tail, lambda i: (i, layer, 0, 0, 0))
        in_specs += [sspec, sspec]
        args += list(state)
    else:
        if layer:
            in_specs += [_stacked_spec(bt, layer, s_tail)] * 2
            args += list(prev_states)
        out_specs += [_stacked_spec(bt, layer + 1, s_tail)] * 2
        out_shape += [jax.ShapeDtypeStruct((b, layer + 1) + s_tail, F32)] * 2
    params = [lg_pair, lg_head, norm]
    in_specs += [_layer_spec(p, layer) for p in params]
    args += params
    n_pairs = H_B // 2
    return pl.pallas_call(
        functools.partial(_ret_kernel, chunk=chunk, n_chunks=n_chunks, has_state=has_state,
                          n_prev=0 if has_state else layer),
        grid=(b // bt,),
        in_specs=in_specs,
        out_specs=out_specs,
        out_shape=out_shape,
        scratch_shapes=[
            pltpu.VMEM((n_pairs, 2 * chunk, chunk), F32),
            pltpu.VMEM((6, n_pairs, chunk, LANES), F32),
            pltpu.VMEM((2, n_chunks, n_pairs, LANES, LANES), F32),
            pltpu.VMEM((n_chunks, n_pairs, LANES, 2 * LANES), BF16),
            pltpu.VMEM((l, wb), F32),
        ],
        compiler_params=_cparams(("arbitrary",)),
        name="retention",
    )(*args)


def _attn_kernel(*refs, has_cache, lam_init):
    if has_cache:
        (q_ref, k_ref, v_ref, ck_ref, cv_ref, dl_ref, dn_ref, y_ref, k_scr, v_scr) = refs
        n_past = ck_ref.shape[1]

        @pl.when(pl.program_id(1) == 0)
        def _gather_keys():
            k_scr[:n_past, :] = ck_ref[...].T.astype(BF16)
            for h in range(H_C):
                v_scr[:n_past, h * LANES:(h + 1) * LANES] = cv_ref[:, h, :].astype(BF16)
            k_scr[n_past:, :] = k_ref[0]
            v_scr[n_past:, :] = v_ref[0]

        keys = lambda b, sl: k_scr[:, sl]
        vals = lambda b, sl: v_scr[:, sl]
    else:
        (q_ref, k_ref, v_ref, dl_ref, dn_ref, y_ref) = refs
        keys = lambda b, sl: k_ref[b, :, sl]
        vals = lambda b, sl: v_ref[b, :, sl]
    bt, tq, _ = q_ref.shape
    dl = dl_ref[...]
    lam = (jnp.exp(jnp.sum(dl[0:1] * dl[1:2], axis=-1, keepdims=True))
           - jnp.exp(jnp.sum(dl[2:3] * dl[3:4], axis=-1, keepdims=True)) + lam_init)
    units = [(b, slice(h * LANES, (h + 1) * LANES)) for b in range(bt) for h in range(H_C)]
    scores = lambda b, sl: _dot_nt(keys(b, sl), _split_halves(q_ref[b, :, sl]))
    st_next = scores(*units[0])
    for i, (b, sl) in enumerate(units):
        st = st_next
        if i + 1 < len(units):
            st_next = scores(*units[i + 1])
        e = jnp.exp(st - jnp.max(st, axis=0, keepdims=True))
        l = jnp.sum(e, axis=0, keepdims=True)
        w = e * jnp.concatenate([1.0 / l[:, :tq], lam / l[:, tq:]], axis=1)
        o = _dot_tn((w[:, :tq] - w[:, tq:]).astype(BF16), vals(b, sl))
        y_ref[b, :, sl] = (_rmsnorm(o, dn_ref[...]) * (1.0 - lam_init)).astype(BF16)


def _mixer_c(q, k, v, layer, dl, dn, lam_init, cache):
    b, l, wc = q.shape
    has_cache = cache is not None
    tq = min(l, ATTN_Q_ROWS)
    bt = 1 if has_cache else max(1, ATTN_SEQ_ROWS // l)
    q_spec = pl.BlockSpec((bt, tq, wc), lambda i, j: (i, j, 0))
    kv_spec = pl.BlockSpec((bt, l, wc), lambda i, j: (i, 0, 0))
    in_specs = [q_spec, kv_spec, kv_spec]
    args = [q, k, v]
    n_keys = l
    if has_cache:
        ck_t, cv = cache
        n_past = cv.shape[2]
        n_keys += n_past
        in_specs += [pl.BlockSpec((None, None, wc, n_past), lambda i, j: (i, layer, 0, 0)),
                     pl.BlockSpec((None, None, n_past, H_C, LANES), lambda i, j: (i, layer, 0, 0, 0))]
        args += [ck_t, cv]
    in_specs += [_layer_spec(dl, layer), _layer_spec(dn, layer)]
    args += [dl, dn]
    scratch = [pltpu.VMEM((n_keys, wc), BF16), pltpu.VMEM((n_keys, wc), BF16)] if has_cache else []
    return pl.pallas_call(
        functools.partial(_attn_kernel, has_cache=has_cache, lam_init=lam_init),
        grid=(b // bt, l // tq),
        in_specs=in_specs,
        out_specs=q_spec,
        out_shape=jax.ShapeDtypeStruct((b, l, wc), BF16),
        scratch_shapes=scratch,
        compiler_params=_cparams(("parallel", "arbitrary")),
        name="diff_attention",
    )(*args)


def _mlp_kernel(ya_ref, yb_ref, yc_ref, x_ref, mod_ref, g_ref, wo_ref, up_ref, cw_ref, cb_ref, down_ref,
                o_ref, h_scr, g_scr, *, seq_len):
    rows = x_ref.shape[0]
    m = mod_ref[0]
    y = None
    k0 = 0
    for y_ref in (ya_ref, yb_ref, yc_ref):
        kw = y_ref.shape[1]
        part = _dot(y_ref[...], wo_ref[k0:k0 + kw, :])
        y = part if y is None else y + part
        k0 += kw
    x1 = x_ref[...] + m[2:3, :] * y
    o_ref[...] = x1
    h_scr[...] = (_rmsnorm(x1, g_ref[...]) * (1.0 + m[4:5, :]) + m[3:4, :]).astype(BF16)

    d_ff = down_ref.shape[0]
    fc = MXU_DIM
    pos = lax.broadcasted_iota(jnp.int32, (rows, fc), 0) % seq_len
    has_prev = pos > 0
    has_next = pos < seq_len - 1

    def conv(c0):
        u = _dot(h_scr[...], up_ref[:, c0:c0 + fc])
        w = cw_ref[:, c0:c0 + fc]
        prev = jnp.where(has_prev, pltpu.roll(u, 1, 0), 0.0)
        nxt = jnp.where(has_next, pltpu.roll(u, rows - 1, 0), 0.0)
        return prev * w[0:1, :] + u * w[1:2, :] + nxt * w[2:3, :] + cb_ref[:, c0:c0 + fc]

    for j in range(d_ff // fc):
        a = conv(j * fc)
        b = conv(d_ff + j * fc)
        g_scr[:, j * fc:(j + 1) * fc] = (_silu(a) * b).astype(BF16)

    o_ref[...] = o_ref[...] + m[5:6, :] * _dot(g_scr[...], down_ref[...])


def _mlp(ya, yb, yc, x, mods, mod_row, layer, seq_len, g, w_out, up, cw, cb, down):
    t, d = x.shape
    rows = MLP_ROWS
    tiles_per_seq = max(1, seq_len // rows)
    row_spec = lambda a: pl.BlockSpec((rows, a.shape[1]), lambda i: (i, 0))
    weights = [w_out, up, cw, cb, down]
    return pl.pallas_call(
        functools.partial(_mlp_kernel, seq_len=seq_len),
        grid=(t // rows,),
        in_specs=[row_spec(ya), row_spec(yb), row_spec(yc), row_spec(x),
                  _mod_spec(mods, layer, lambda i: mod_row(i // tiles_per_seq)), _layer_spec(g, layer)]
                 + [_layer_spec(w, layer, resident=True) for w in weights],
        out_specs=row_spec(x),
        out_shape=jax.ShapeDtypeStruct((t, d), F32),
        scratch_shapes=[pltpu.VMEM((rows, d), BF16), pltpu.VMEM((rows, down.shape[1]), BF16)],
        compiler_params=_cparams(("parallel",)),
        name="out_proj_mlp",
    )(ya, yb, yc, x, mods, g, *weights)


def _rope_tables(l):
    rows = l // GRID_W
    t_row = jnp.repeat(jnp.arange(rows, dtype=F32), GRID_W)
    t_col = jnp.tile(jnp.arange(GRID_W, dtype=F32), rows)
    inv = ROPE_BASE ** (-jnp.arange(ROPE_PAIRS, dtype=F32) / ROPE_PAIRS)
    ar, ac = t_row[:, None] * inv, t_col[:, None] * inv
    cos = jnp.concatenate([jnp.cos(ar), jnp.cos(ar), jnp.cos(ac), jnp.cos(ac)], axis=-1)
    sin = jnp.concatenate([-jnp.sin(ar), jnp.sin(ar), -jnp.sin(ac), jnp.sin(ac)], axis=-1)
    return jnp.tile(cos, (1, 2)), jnp.tile(sin, (1, 2))


def kernel(x_prompt, x_sample, c, cache_k, cache_v, state_ret_fwd, state_ret_bwd, c_ctx, norm1, w_mod, b_mod,
           w_in, sgu_norm, sgu_w, sgu_b, ret_logit_fwd, ret_logit_bwd, ret_norm, q_norm, k_norm, diff_lam,
           diff_norm, w_out, norm2, ffn_up, ffn_conv, ffn_conv_b, ffn_down):
    depth, d_model, _ = w_mod.shape
    batch, seq, _ = x_prompt.shape
    dec_batch, dec_seq, _ = x_sample.shape
    w_a = sgu_norm.shape[1]
    w_b = H_B * ret_norm.shape[2]
    w_c = 2 * H_C * HD_C
    past = cache_k.shape[2]

    pad = (-(dec_batch + 1)) % 8
    cond = jnp.concatenate([c, c_ctx[None, :], jnp.zeros((pad, d_model), F32)], axis=0)
    mods = _modulation(cond, w_mod, b_mod).reshape(depth, cond.shape[0], 6, d_model)

    w_in_b = w_in.astype(BF16)
    w_out_b = w_out.astype(BF16)
    sgu_w_b = sgu_w.astype(BF16)
    up_b = ffn_up.astype(BF16)
    down_b = ffn_down.astype(BF16)
    row3 = lambda a: a.reshape(depth, 1, -1)
    sgu_bias = jnp.repeat(jnp.swapaxes(sgu_b, 1, 2), w_a // G_A, axis=2)
    logits = jnp.stack([ret_logit_fwd, ret_logit_bwd], axis=1)
    lg_pair = jnp.repeat(logits, DK_B, axis=2).reshape(depth, 2 * (H_B // 2), LANES)
    lg_head = logits.reshape(depth, 2 * H_B, 1)
    qg = row3(jnp.tile(q_norm, (1, 2 * H_C)))
    kg = row3(jnp.tile(k_norm, (1, 2 * H_C)))
    norm1_r, norm2_r, sgu_g, ret_g, diff_g = row3(norm1), row3(norm2), row3(sgu_norm), row3(ret_norm), row3(diff_norm)
    conv_b = row3(ffn_conv_b)
    rope = _rope_tables(dec_seq)
    cache_kt = jnp.transpose(cache_k, (0, 1, 3, 4, 5, 2)).reshape(dec_batch, depth, w_c, past)

    def layer(x, l, nb, sl, mod_row, is_sample, new_cache, new_states):
        lam_init = 0.8 - 0.6 * math.exp(-0.3 * l)
        outs = _projection(x, mods, mod_row, l, sl, norm1_r, w_in_b, sgu_g, sgu_w_b, sgu_bias, qg, kg,
                           rope if is_sample else None, new_cache)
        ya, qr, kr, vr, gr, qa, ka, va = outs[:8]
        seq3 = lambda a: a.reshape(nb, sl, a.shape[1])
        state = (state_ret_fwd, state_ret_bwd) if is_sample else None
        yb, *states = _mixer_b(seq3(qr), seq3(kr), seq3(vr), seq3(gr), l, lg_pair, lg_head, ret_g, state,
                               new_states)
        cache = (cache_kt, cache_v) if is_sample else None
        yc = _mixer_c(seq3(qa), seq3(ka), seq3(va), l, diff_lam, diff_g, lam_init, cache)
        x = _mlp(ya, yb.reshape(nb * sl, w_b), yc.reshape(nb * sl, w_c), x, mods, mod_row, l, sl, norm2_r,
                 w_out_b, up_b, ffn_conv, conv_b, down_b)
        return x, outs[8:], states

    y_prompt = x_prompt.reshape(batch * seq, d_model)
    new_cache, new_states = None, None
    for l in range(depth):
        y_prompt, new_cache, new_states = layer(y_prompt, l, batch, seq, lambda b: dec_batch, False,
                                                new_cache, new_states)

    y_sample = x_sample.reshape(dec_batch * dec_seq, d_model)
    for l in range(depth):
        y_sample, _, _ = layer(y_sample, l, dec_batch, dec_seq, lambda b: b, True, None, None)

    new_kt, new_v = new_cache
    new_k = jnp.transpose(new_kt.reshape(batch, depth, H_C, 2, HD_C, seq), (0, 1, 5, 2, 3, 4))
    return (y_prompt.reshape(batch, seq, d_model), y_sample.reshape(dec_batch, dec_seq, d_model),
            new_k, new_v, new_states[0], new_states[1])
```

```python
import functools
import math

import jax
import jax.numpy as jnp
from jax import lax
from jax.experimental import pallas as pl
from jax.experimental.pallas import tpu as pltpu

F32 = jnp.float32
BF16 = jnp.bfloat16

GRID_W = 64
CHUNK = 128
EPS = 1e-6
ROPE_BASE = 10000.0
G_A = 4
H_B = 4
DK_B = 64
H_C = 4
HD_C = 64
ROPE_PAIRS = HD_C // 4
LANES = 128
MXU_DIM = 256
PROJ_ROWS = 512
MLP_ROWS = 1024
ATTN_Q_ROWS = 256
ATTN_SEQ_ROWS = 1024
RET_ROWS = 1024
RET_CHUNK = 256
RET_UNROLL = 4
VMEM_LIMIT = 60 * 1024 * 1024


def _cparams(sem):
    return pltpu.CompilerParams(dimension_semantics=sem, vmem_limit_bytes=VMEM_LIMIT)


def _const_spec(shape):
    n = len(shape)
    return pl.BlockSpec(shape, lambda *_: (0,) * n)


def _layer_spec(arr, layer, resident=False):
    n = arr.ndim - 1
    mode = dict(pipeline_mode=pl.Buffered(1)) if resident else {}
    return pl.BlockSpec((None,) + arr.shape[1:], lambda *_: (layer,) + (0,) * n, **mode)


def _silu(x):
    return x * jax.nn.sigmoid(x)


def _dot(a, b):
    return jnp.dot(a, b, preferred_element_type=F32)


def _dot_nt(a, b):
    return lax.dot_general(a, b, (((1,), (1,)), ((), ())), preferred_element_type=F32)


def _dot_tn(a, b):
    return lax.dot_general(a, b, (((0,), (0,)), ((), ())), preferred_element_type=F32)


def _lo_mask(rows):
    return lax.broadcasted_iota(jnp.int32, (rows, LANES), 1) < (LANES // 2)


def _split_halves(x):
    lo = _lo_mask(x.shape[0])
    zero = jnp.zeros_like(x)
    return jnp.concatenate([jnp.where(lo, x, zero), jnp.where(lo, zero, x)], axis=0)


def _group64_mean(x2):
    rows, n = x2.shape
    lo = _lo_mask(rows)
    outs = []
    for j in range(n // LANES):
        blk = x2[:, j * LANES:(j + 1) * LANES]
        s_lo = jnp.sum(jnp.where(lo, blk, 0.0), axis=-1, keepdims=True)
        s_hi = jnp.sum(jnp.where(lo, 0.0, blk), axis=-1, keepdims=True)
        outs.append(jnp.where(lo, s_lo, s_hi))
    out = outs[0] if len(outs) == 1 else jnp.concatenate(outs, axis=-1)
    return out * (1.0 / 64.0)


def _group64_rmsnorm(x, g):
    return x * lax.rsqrt(_group64_mean(x * x) + EPS) * g


def _rmsnorm(x, g):
    return x * lax.rsqrt(jnp.mean(x * x, axis=-1, keepdims=True) + EPS) * g


def _rope(x, cos, sin):
    rows, n = x.shape
    lane = lax.broadcasted_iota(jnp.int32, (rows, LANES), 1)
    first = (lane % (2 * ROPE_PAIRS)) < ROPE_PAIRS
    outs = []
    for j in range(n // LANES):
        blk = x[:, j * LANES:(j + 1) * LANES]
        partner = jnp.where(first, pltpu.roll(blk, LANES - ROPE_PAIRS, 1), pltpu.roll(blk, ROPE_PAIRS, 1))
        outs.append(blk * cos + partner * sin)
    return jnp.concatenate(outs, axis=-1)


def _mod_kernel(cond_ref, w_ref, b_ref, o_ref):
    s = _silu(cond_ref[...]).astype(BF16)
    o_ref[0] = _dot(s, w_ref[0].astype(BF16)) + b_ref[0]


def _modulation(cond, w_mod, b_mod):
    depth, d, n = w_mod.shape
    rows = cond.shape[0]
    tn = n // 4
    return pl.pallas_call(
        _mod_kernel,
        grid=(depth, n // tn),
        in_specs=[
            _const_spec((rows, d)),
            pl.BlockSpec((1, d, tn), lambda l, j: (l, 0, j)),
            pl.BlockSpec((1, 1, tn), lambda l, j: (l, 0, j)),
        ],
        out_specs=pl.BlockSpec((1, rows, tn), lambda l, j: (l, 0, j)),
        out_shape=jax.ShapeDtypeStruct((depth, rows, n), F32),
        compiler_params=_cparams(("parallel", "parallel")),
        name="adaln_mod",
    )(cond, w_mod, b_mod.reshape(depth, 1, n))


def _mod_spec(mods, layer, row_of_step):
    return pl.BlockSpec((None, 1) + mods.shape[2:], lambda i: (layer, row_of_step(i), 0, 0))


def _proj_kernel(*refs, has_rope, seq_len, n_prev):
    if has_rope:
        (x_ref, mod_ref, g_ref, w_ref, sg_ref, ws_ref, sb_ref, qg_ref, kg_ref, cos_ref, sin_ref,
         ya_ref, qr_ref, kr_ref, vr_ref, gr_ref, qa_ref, ka_ref, va_ref) = refs
    elif n_prev:
        (x_ref, mod_ref, g_ref, w_ref, sg_ref, ws_ref, sb_ref, qg_ref, kg_ref, kt_prev_ref, vf_prev_ref,
         ya_ref, qr_ref, kr_ref, vr_ref, gr_ref, qa_ref, ka_ref, va_ref, kt_ref, vf_ref) = refs
        kt_ref[:, :n_prev] = kt_prev_ref[...]
        vf_ref[:, :n_prev] = vf_prev_ref[...]
    else:
        (x_ref, mod_ref, g_ref, w_ref, sg_ref, ws_ref, sb_ref, qg_ref, kg_ref,
         ya_ref, qr_ref, kr_ref, vr_ref, gr_ref, qa_ref, ka_ref, va_ref, kt_ref, vf_ref) = refs
    rows = x_ref.shape[0]
    m = mod_ref[0]
    hb = (_rmsnorm(x_ref[...], g_ref[...]) * (1.0 + m[1:2, :]) + m[0:1, :]).astype(BF16)
    wa = ya_ref.shape[1]
    wb = qr_ref.shape[1]
    wc = qa_ref.shape[1]

    def proj(c0, width):
        return _dot(hb, w_ref[:, c0:c0 + width])

    def emit_q(z):
        qn = _group64_rmsnorm(z, qg_ref[...])
        if has_rope:
            qn = _rope(qn, cos_ref[...], sin_ref[...])
        qa_ref[...] = (qn * (HD_C ** -0.5)).astype(BF16)

    def emit_k(z):
        kn = _group64_rmsnorm(z, kg_ref[...])
        if has_rope:
            kn = _rope(kn, cos_ref[...], sin_ref[...])
        else:
            for s in range(rows // seq_len):
                kt_ref[s, n_prev] = kn[s * seq_len:(s + 1) * seq_len, :].T
        ka_ref[...] = kn.astype(BF16)

    def emit_v(z):
        if not has_rope:
            for s in range(rows // seq_len):
                zs = z[s * seq_len:(s + 1) * seq_len, :]
                heads = jnp.stack([zs[:, h * LANES:(h + 1) * LANES] for h in range(H_C)], axis=0)
                vf_ref[s, n_prev] = jnp.swapaxes(heads, 0, 1)
        va_ref[...] = z.astype(BF16)

    def sgu_prepare(z):
        za = jax.nn.gelu(z)
        return za[:, :wa], _rmsnorm(za[:, wa:], sg_ref[...]).astype(BF16)

    def sgu_mix(u, v):
        lane = lax.broadcasted_iota(jnp.int32, (CHUNK, wa), 1)
        dg = wa // G_A
        for c in range(rows // CHUNK):
            rs = slice(c * CHUNK, (c + 1) * CHUNK)
            s = _dot(ws_ref[G_A - 1], v[rs])
            for g in range(G_A - 2, -1, -1):
                s = jnp.where(lane < (g + 1) * dg, _dot(ws_ref[g], v[rs]), s)
            ya_ref[rs, :] = (u[rs] * (s + sb_ref[...])).astype(BF16)

    def emit_ret_qk(z):
        qr_ref[...] = z[:, :wb].astype(BF16)
        kr_ref[...] = z[:, wb:] * (DK_B ** -0.5)

    def emit_ret_vg(z):
        vr_ref[...] = z[:, :wb].astype(BF16)
        gr_ref[...] = _silu(z[:, wb:])

    cb = 2 * wa
    cc = cb + 4 * wb
    z_a = proj(0, 2 * wa)
    z_q = proj(cc, wc)
    u, v = sgu_prepare(z_a)
    z_k = proj(cc + wc, wc)
    sgu_mix(u, v)
    emit_q(z_q)
    z_v = proj(cc + 2 * wc, wc)
    emit_k(z_k)
    z_r = proj(cb, 2 * wb)
    emit_v(z_v)
    z_g = proj(cb + 2 * wb, 2 * wb)
    emit_ret_qk(z_r)
    emit_ret_vg(z_g)


def _stacked_spec(lead, layers, tail):
    return pl.BlockSpec((lead, layers) + tail, lambda i: (i, 0) + (0,) * len(tail))


def _projection(x, mods, mod_row, layer, seq_len, g, w_bf16, sgu_g, sgu_w, sgu_bias, qg, kg, rope, prev_cache):
    t, d = x.shape
    rows = 2 * PROJ_ROWS if rope is not None else PROJ_ROWS
    wa = sgu_g.shape[2]
    wb = wa
    wc = qg.shape[2]
    has_rope = rope is not None
    tiles_per_seq = max(1, seq_len // rows)
    row_spec = lambda w: pl.BlockSpec((rows, w), lambda i: (i, 0))
    params = [g, w_bf16, sgu_g, sgu_w, sgu_bias, qg, kg]
    in_specs = [row_spec(d), _mod_spec(mods, layer, lambda i: mod_row(i // tiles_per_seq))]
    in_specs += [_layer_spec(p, layer) for p in params]
    args = [x, mods] + params
    outs = [(wa, BF16), (wb, BF16), (wb, F32), (wb, BF16), (wb, F32), (wc, BF16), (wc, BF16), (wc, BF16)]
    out_specs = [row_spec(w) for w, _ in outs]
    out_shape = [jax.ShapeDtypeStruct((t, w), dt) for w, dt in outs]
    if has_rope:
        tab_spec = pl.BlockSpec((rows, LANES), lambda i: (i % tiles_per_seq, 0))
        in_specs += [tab_spec, tab_spec]
        args += list(rope)
    else:
        seqs = rows // seq_len
        k_tail, v_tail = (wc, seq_len), (seq_len, H_C, LANES)
        if layer:
            in_specs += [_stacked_spec(seqs, layer, k_tail), _stacked_spec(seqs, layer, v_tail)]
            args += list(prev_cache)
        out_specs += [_stacked_spec(seqs, layer + 1, k_tail), _stacked_spec(seqs, layer + 1, v_tail)]
        out_shape += [jax.ShapeDtypeStruct((t // seq_len, layer + 1) + k_tail, F32),
                      jax.ShapeDtypeStruct((t // seq_len, layer + 1) + v_tail, F32)]
    return pl.pallas_call(
        functools.partial(_proj_kernel, has_rope=has_rope, seq_len=seq_len, n_prev=0 if has_rope else layer),
        grid=(t // rows,),
        in_specs=in_specs,
        out_specs=out_specs,
        out_shape=out_shape,
        compiler_params=_cparams(("parallel",)),
        name="projection",
    )(*args)


def _log_sigmoid(x):
    y = -x
    return -(jnp.maximum(y, 0.0) + jnp.log1p(jnp.exp(-jnp.abs(y))))


def _ret_kernel(*refs, chunk, n_chunks, has_state, n_prev):
    if has_state:
        (q_ref, k_ref, v_ref, gate_ref, r0f_ref, r0b_ref, lgp_ref, lgh_ref, norm_ref,
         y_ref, d_scr, dec_scr, kv_scr, rs_scr, o_scr) = refs
        state_outs = None
    elif n_prev:
        (q_ref, k_ref, v_ref, gate_ref, rf_prev_ref, rb_prev_ref, lgp_ref, lgh_ref, norm_ref,
         y_ref, rf_ref, rb_ref, d_scr, dec_scr, kv_scr, rs_scr, o_scr) = refs
        rf_ref[:, :n_prev] = rf_prev_ref[...]
        rb_ref[:, :n_prev] = rb_prev_ref[...]
        state_outs = (rf_ref, rb_ref)
    else:
        (q_ref, k_ref, v_ref, gate_ref, lgp_ref, lgh_ref, norm_ref,
         y_ref, rf_ref, rb_ref, d_scr, dec_scr, kv_scr, rs_scr, o_scr) = refs
        state_outs = (rf_ref, rb_ref)
    bt = q_ref.shape[0]
    n_pairs = H_B // 2
    half = LANES // 2
    cs = chunk

    @pl.when(pl.program_id(0) == 0)
    def _build_decay_tables():
        lgp = _log_sigmoid(lgp_ref[...])
        lgh = _log_sigmoid(lgh_ref[...])
        dist = (lax.broadcasted_iota(jnp.int32, (cs, cs), 0) - lax.broadcasted_iota(jnp.int32, (cs, cs), 1)).astype(F32)
        for h in range(H_B):
            lf = lgh[h:h + 1, :]
            lb = lgh[H_B + h:H_B + h + 1, :]
            d_f = jnp.where(dist >= 0, jnp.exp(lf * jnp.maximum(dist, 0.0)), 0.0)
            d_b = jnp.where(dist <= 0, jnp.exp(lb * jnp.maximum(-dist, 0.0)), 0.0)
            d_scr[h // 2, (h % 2) * cs:(h % 2 + 1) * cs, :] = d_f + d_b
        pos = lax.broadcasted_iota(jnp.int32, (cs, LANES), 0).astype(F32)
        for p in range(n_pairs):
            lf = lgp[p:p + 1, :]
            lb = lgp[n_pairs + p:n_pairs + p + 1, :]
            dec_scr[0, p] = jnp.exp(lf * (pos + 1.0))
            dec_scr[1, p] = jnp.exp(lf * (cs - 1.0 - pos))
            dec_scr[2, p] = jnp.exp(lb * (cs - pos))
            dec_scr[3, p] = jnp.exp(lb * pos)
            dec_scr[4, p] = jnp.broadcast_to(jnp.exp(lf * float(cs)), (cs, LANES))
            dec_scr[5, p] = jnp.broadcast_to(jnp.exp(lb * float(cs)), (cs, LANES))

    lo = _lo_mask(cs)
    blockdiag = ((lax.broadcasted_iota(jnp.int32, (LANES, LANES), 0) < half)
                 == (lax.broadcasted_iota(jnp.int32, (LANES, LANES), 1) < half))

    for b in range(bt):
        def intra(c, carry):
            rows = pl.ds(pl.multiple_of(c * cs, cs), cs)
            qc, kc, vc = q_ref[b, rows, :], k_ref[b, rows, :], v_ref[b, rows, :]
            for p in range(n_pairs):
                sl = slice(p * LANES, (p + 1) * LANES)
                k128, vb = kc[:, sl], vc[:, sl]
                inner = (_dot_nt(_split_halves(qc[:, sl]), k128.astype(BF16)) * d_scr[p]).astype(BF16)
                oo = _dot(inner, vb)
                o_scr[rows, sl] = jnp.where(lo, oo[:cs], oo[cs:])
                kk = jnp.concatenate([k128 * dec_scr[1, p], k128 * dec_scr[3, p]], axis=1).astype(BF16)
                kv = _dot_tn(kk, vb)
                kv_scr[0, c, p] = jnp.where(blockdiag, kv[:LANES], 0.0)
                kv_scr[1, c, p] = jnp.where(blockdiag, kv[LANES:], 0.0)
            return carry

        lax.fori_loop(0, n_chunks, intra, 0, unroll=min(n_chunks, RET_UNROLL))

        for p in range(n_pairs):
            for d in range(2):
                if has_state:
                    r0_ref = (r0f_ref, r0b_ref)[d]
                    zero = jnp.zeros((half, half), F32)
                    top = jnp.concatenate([r0_ref[b, 2 * p], zero], axis=1)
                    bot = jnp.concatenate([zero, r0_ref[b, 2 * p + 1]], axis=1)
                    r = jnp.concatenate([top, bot], axis=0)
                else:
                    r = jnp.zeros((LANES, LANES), F32)
                order = range(n_chunks) if d == 0 else range(n_chunks - 1, -1, -1)
                for c in order:
                    rs_scr[c, p, :, d * LANES:(d + 1) * LANES] = r.astype(BF16)
                    r = r * dec_scr[4 + d, p, :LANES, :] + kv_scr[d, c, p]
                if state_outs is not None:
                    state_outs[d][b, n_prev, 2 * p] = r[:half, :half]
                    state_outs[d][b, n_prev, 2 * p + 1] = r[half:, half:]

        def cross(c, carry):
            rows = pl.ds(pl.multiple_of(c * cs, cs), cs)
            qc = q_ref[b, rows, :]
            outs = []
            for p in range(n_pairs):
                sl = slice(p * LANES, (p + 1) * LANES)
                t = _dot(qc[:, sl], rs_scr[c, p])
                outs.append(o_scr[rows, sl] + t[:, :LANES] * dec_scr[0, p] + t[:, LANES:] * dec_scr[2, p])
            o = _group64_rmsnorm(jnp.concatenate(outs, axis=-1), norm_ref[...])
            y_ref[b, rows, :] = (gate_ref[b, rows, :] * o).astype(BF16)
            return carry

        lax.fori_loop(0, n_chunks, cross, 0, unroll=min(n_chunks, RET_UNROLL))


def _mixer_b(q, k, v, gate, layer, lg_pair, lg_head, norm, state, prev_states):
    b, l, wb = q.shape
    bt = max(1, RET_ROWS // l)
    chunk = min(l, RET_CHUNK)
    n_chunks = l // chunk
    has_state = state is not None
    tile = pl.BlockSpec((bt, l, wb), lambda i: (i, 0, 0))
    in_specs = [tile, tile, tile, tile]
    args = [q, k, v, gate]
    out_specs = [tile]
    out_shape = [jax.ShapeDtypeStruct((b, l, wb), BF16)]
    s_tail = (H_B, DK_B, DK_B)
    if has_state:
        sspec = pl.BlockSpec((bt, None) + s_tail, lambda i: (i, layer, 0, 0, 0))
        in_specs += [sspec, sspec]
        args += list(state)
    else:
        if layer:
            in_specs += [_stacked_spec(bt, layer, s_tail)] * 2
            args += list(prev_states)
        out_specs += [_stacked_spec(bt, layer + 1, s_tail)] * 2
        out_shape += [jax.ShapeDtypeStruct((b, layer + 1) + s_tail, F32)] * 2
    params = [lg_pair, lg_head, norm]
    in_specs += [_layer_spec(p, layer) for p in params]
    args += params
    n_pairs = H_B // 2
    return pl.pallas_call(
        functools.partial(_ret_kernel, chunk=chunk, n_chunks=n_chunks, has_state=has_state,
                          n_prev=0 if has_state else layer),
        grid=(b // bt,),
        in_specs=in_specs,
        out_specs=out_specs,
        out_shape=out_shape,
        scratch_shapes=[
            pltpu.VMEM((n_pairs, 2 * chunk, chunk), F32),
            pltpu.VMEM((6, n_pairs, chunk, LANES), F32),
            pltpu.VMEM((2, n_chunks, n_pairs, LANES, LANES), F32),
            pltpu.VMEM((n_chunks, n_pairs, LANES, 2 * LANES), BF16),
            pltpu.VMEM((l, wb), F32),
        ],
        compiler_params=_cparams(("arbitrary",)),
        name="retention",
    )(*args)


def _attn_kernel(*refs, has_cache, lam_init):
    if has_cache:
        (q_ref, k_ref, v_ref, ck_ref, cv_ref, dl_ref, dn_ref, y_ref, k_scr, v_scr) = refs
        n_past = ck_ref.shape[1]

        @pl.when(pl.program_id(1) == 0)
        def _gather_keys():
            k_scr[...] = ck_ref[...].T.astype(BF16)
            cv = jnp.swapaxes(cv_ref[...], 0, 1)
            for h in range(H_C):
                v_scr[:, h * LANES:(h + 1) * LANES] = cv[h].astype(BF16)

        keys = lambda b, sl: jnp.concatenate([k_scr[:, sl], k_ref[b, :, sl]], axis=0)
        vals = lambda b, sl: jnp.concatenate([v_scr[:, sl], v_ref[b, :, sl]], axis=0)
    else:
        (q_ref, k_ref, v_ref, dl_ref, dn_ref, y_ref) = refs
        keys = lambda b, sl: k_ref[b, :, sl]
        vals = lambda b, sl: v_ref[b, :, sl]
    bt, tq, _ = q_ref.shape
    dl = dl_ref[...]
    lam = (jnp.exp(jnp.sum(dl[0:1] * dl[1:2], axis=-1, keepdims=True))
           - jnp.exp(jnp.sum(dl[2:3] * dl[3:4], axis=-1, keepdims=True)) + lam_init)
    units = [(b, slice(h * LANES, (h + 1) * LANES)) for b in range(bt) for h in range(H_C)]
    scores = lambda b, sl: _dot_nt(keys(b, sl), _split_halves(q_ref[b, :, sl]))
    st_next = scores(*units[0])
    for i, (b, sl) in enumerate(units):
        st = st_next
        if i + 1 < len(units):
            st_next = scores(*units[i + 1])
        e = jnp.exp(st - jnp.max(st, axis=0, keepdims=True))
        l = jnp.sum(e, axis=0, keepdims=True)
        w = e * jnp.concatenate([1.0 / l[:, :tq], lam / l[:, tq:]], axis=1)
        o = _dot_tn((w[:, :tq] - w[:, tq:]).astype(BF16), vals(b, sl))
        y_ref[b, :, sl] = (_rmsnorm(o, dn_ref[...]) * (1.0 - lam_init)).astype(BF16)


def _mixer_c(q, k, v, layer, dl, dn, lam_init, cache):
    b, l, wc = q.shape
    has_cache = cache is not None
    tq = min(l, ATTN_Q_ROWS)
    bt = 1 if has_cache else max(1, ATTN_SEQ_ROWS // l)
    q_spec = pl.BlockSpec((bt, tq, wc), lambda i, j: (i, j, 0))
    kv_spec = pl.BlockSpec((bt, l, wc), lambda i, j: (i, 0, 0))
    in_specs = [q_spec, kv_spec, kv_spec]
    args = [q, k, v]
    n_keys = l
    if has_cache:
        ck_t, cv = cache
        n_past = cv.shape[2]
        n_keys += n_past
        in_specs += [pl.BlockSpec((None, None, wc, n_past), lambda i, j: (i, layer, 0, 0)),
                     pl.BlockSpec((None, None, n_past, H_C, LANES), lambda i, j: (i, layer, 0, 0, 0))]
        args += [ck_t, cv]
    in_specs += [_layer_spec(dl, layer), _layer_spec(dn, layer)]
    args += [dl, dn]
    scratch = [pltpu.VMEM((n_past, wc), BF16), pltpu.VMEM((n_past, wc), BF16)] if has_cache else []
    return pl.pallas_call(
        functools.partial(_attn_kernel, has_cache=has_cache, lam_init=lam_init),
        grid=(b // bt, l // tq),
        in_specs=in_specs,
        out_specs=q_spec,
        out_shape=jax.ShapeDtypeStruct((b, l, wc), BF16),
        scratch_shapes=scratch,
        compiler_params=_cparams(("parallel", "arbitrary")),
        name="diff_attention",
    )(*args)


def _mlp_kernel(ya_ref, yb_ref, yc_ref, x_ref, mod_ref, g_ref, wo_ref, up_ref, cw_ref, cb_ref, down_ref,
                o_ref, h_scr, g_scr, *, seq_len):
    rows = x_ref.shape[0]
    m = mod_ref[0]
    y = None
    k0 = 0
    for y_ref in (ya_ref, yb_ref, yc_ref):
        kw = y_ref.shape[1]
        part = _dot(y_ref[...], wo_ref[k0:k0 + kw, :])
        y = part if y is None else y + part
        k0 += kw
    x1 = x_ref[...] + m[2:3, :] * y
    o_ref[...] = x1
    h_scr[...] = (_rmsnorm(x1, g_ref[...]) * (1.0 + m[4:5, :]) + m[3:4, :]).astype(BF16)

    d_ff = down_ref.shape[0]
    fc = MXU_DIM
    pos = lax.broadcasted_iota(jnp.int32, (rows, fc), 0) % seq_len
    has_prev = pos > 0
    has_next = pos < seq_len - 1

    def conv(c0):
        u = _dot(h_scr[...], up_ref[:, c0:c0 + fc])
        w = cw_ref[:, c0:c0 + fc]
        prev = jnp.where(has_prev, pltpu.roll(u, 1, 0), 0.0)
        nxt = jnp.where(has_next, pltpu.roll(u, rows - 1, 0), 0.0)
        return prev * w[0:1, :] + u * w[1:2, :] + nxt * w[2:3, :] + cb_ref[:, c0:c0 + fc]

    for j in range(d_ff // fc):
        a = conv(j * fc)
        b = conv(d_ff + j * fc)
        g_scr[:, j * fc:(j + 1) * fc] = (_silu(a) * b).astype(BF16)

    o_ref[...] = o_ref[...] + m[5:6, :] * _dot(g_scr[...], down_ref[...])


def _mlp(ya, yb, yc, x, mods, mod_row, layer, seq_len, g, w_out, up, cw, cb, down):
    t, d = x.shape
    rows = MLP_ROWS
    tiles_per_seq = max(1, seq_len // rows)
    row_spec = lambda a: pl.BlockSpec((rows, a.shape[1]), lambda i: (i, 0))
    weights = [w_out, up, cw, cb, down]
    return pl.pallas_call(
        functools.partial(_mlp_kernel, seq_len=seq_len),
        grid=(t // rows,),
        in_specs=[row_spec(ya), row_spec(yb), row_spec(yc), row_spec(x),
                  _mod_spec(mods, layer, lambda i: mod_row(i // tiles_per_seq)), _layer_spec(g, layer)]
                 + [_layer_spec(w, layer, resident=True) for w in weights],
        out_specs=row_spec(x),
        out_shape=jax.ShapeDtypeStruct((t, d), F32),
        scratch_shapes=[pltpu.VMEM((rows, d), BF16), pltpu.VMEM((rows, down.shape[1]), BF16)],
        compiler_params=_cparams(("parallel",)),
        name="out_proj_mlp",
    )(ya, yb, yc, x, mods, g, *weights)


def _rope_tables(l):
    rows = l // GRID_W
    t_row = jnp.repeat(jnp.arange(rows, dtype=F32), GRID_W)
    t_col = jnp.tile(jnp.arange(GRID_W, dtype=F32), rows)
    inv = ROPE_BASE ** (-jnp.arange(ROPE_PAIRS, dtype=F32) / ROPE_PAIRS)
    ar, ac = t_row[:, None] * inv, t_col[:, None] * inv
    cos = jnp.concatenate([jnp.cos(ar), jnp.cos(ar), jnp.cos(ac), jnp.cos(ac)], axis=-1)
    sin = jnp.concatenate([-jnp.sin(ar), jnp.sin(ar), -jnp.sin(ac), jnp.sin(ac)], axis=-1)
    return jnp.tile(cos, (1, 2)), jnp.tile(sin, (1, 2))


def kernel(x_prompt, x_sample, c, cache_k, cache_v, state_ret_fwd, state_ret_bwd, c_ctx, norm1, w_mod, b_mod,
           w_in, sgu_norm, sgu_w, sgu_b, ret_logit_fwd, ret_logit_bwd, ret_norm, q_norm, k_norm, diff_lam,
           diff_norm, w_out, norm2, ffn_up, ffn_conv, ffn_conv_b, ffn_down):
    depth, d_model, _ = w_mod.shape
    batch, seq, _ = x_prompt.shape
    dec_batch, dec_seq, _ = x_sample.shape
    w_a = sgu_norm.shape[1]
    w_b = H_B * ret_norm.shape[2]
    w_c = 2 * H_C * HD_C
    past = cache_k.shape[2]

    pad = (-(dec_batch + 1)) % 8
    cond = jnp.concatenate([c, c_ctx[None, :], jnp.zeros((pad, d_model), F32)], axis=0)
    mods = _modulation(cond, w_mod, b_mod).reshape(depth, cond.shape[0], 6, d_model)

    w_in_b = w_in.astype(BF16)
    w_out_b = w_out.astype(BF16)
    sgu_w_b = sgu_w.astype(BF16)
    up_b = ffn_up.astype(BF16)
    down_b = ffn_down.astype(BF16)
    row3 = lambda a: a.reshape(depth, 1, -1)
    sgu_bias = jnp.repeat(jnp.swapaxes(sgu_b, 1, 2), w_a // G_A, axis=2)
    logits = jnp.stack([ret_logit_fwd, ret_logit_bwd], axis=1)
    lg_pair = jnp.repeat(logits, DK_B, axis=2).reshape(depth, 2 * (H_B // 2), LANES)
    lg_head = logits.reshape(depth, 2 * H_B, 1)
    qg = row3(jnp.tile(q_norm, (1, 2 * H_C)))
    kg = row3(jnp.tile(k_norm, (1, 2 * H_C)))
    norm1_r, norm2_r, sgu_g, ret_g, diff_g = row3(norm1), row3(norm2), row3(sgu_norm), row3(ret_norm), row3(diff_norm)
    conv_b = row3(ffn_conv_b)
    rope = _rope_tables(dec_seq)
    cache_kt = jnp.transpose(cache_k, (0, 1, 3, 4, 5, 2)).reshape(dec_batch, depth, w_c, past)

    def layer(x, l, nb, sl, mod_row, is_sample, new_cache, new_states):
        lam_init = 0.8 - 0.6 * math.exp(-0.3 * l)
        outs = _projection(x, mods, mod_row, l, sl, norm1_r, w_in_b, sgu_g, sgu_w_b, sgu_bias, qg, kg,
                           rope if is_sample else None, new_cache)
        ya, qr, kr, vr, gr, qa, ka, va = outs[:8]
        seq3 = lambda a: a.reshape(nb, sl, a.shape[1])
        state = (state_ret_fwd, state_ret_bwd) if is_sample else None
        yb, *states = _mixer_b(seq3(qr), seq3(kr), seq3(vr), seq3(gr), l, lg_pair, lg_head, ret_g, state,
                               new_states)
        cache = (cache_kt, cache_v) if is_sample else None
        yc = _mixer_c(seq3(qa), seq3(ka), seq3(va), l, diff_lam, diff_g, lam_init, cache)
        x = _mlp(ya, yb.reshape(nb * sl, w_b), yc.reshape(nb * sl, w_c), x, mods, mod_row, l, sl, norm2_r,
                 w_out_b, up_b, ffn_conv, conv_b, down_b)
        return x, outs[8:], states

    y_prompt = x_prompt.reshape(batch * seq, d_model)
    new_cache, new_states = None, None
    for l in range(depth):
        y_prompt, new_cache, new_states = layer(y_prompt, l, batch, seq, lambda b: dec_batch, False,
                                                new_cache, new_states)

    y_sample = x_sample.reshape(dec_batch * dec_seq, d_model)
    for l in range(depth):
        y_sample, _, _ = layer(y_sample, l, dec_batch, dec_seq, lambda b: b, True, None, None)

    new_kt, new_v = new_cache
    new_k = jnp.transpose(new_kt.reshape(batch, depth, H_C, 2, HD_C, seq), (0, 1, 5, 2, 3, 4))
    return (y_prompt.reshape(batch, seq, d_model), y_sample.reshape(dec_batch, dec_seq, d_model),
            new_k, new_v, new_states[0], new_states[1])
```

```python
import functools
import math

import jax
import jax.numpy as jnp
from jax import lax
from jax.experimental import pallas as pl
from jax.experimental.pallas import tpu as pltpu

F32 = jnp.float32
BF16 = jnp.bfloat16

GRID_W = 64
CHUNK = 128
EPS = 1e-6
ROPE_BASE = 10000.0
G_A = 4
H_B = 4
DK_B = 64
H_C = 4
HD_C = 64
ROPE_PAIRS = HD_C // 4
LANES = 128
MXU_DIM = 256
MOD_COL_TILES = 4
PROJ_ROWS = 512
MLP_ROWS = 1024
ATTN_Q_ROWS = 256
ATTN_SEQ_ROWS = 1024
RET_ROWS = 2048
RET_CHUNK = 256
RET_UNROLL = 4
VMEM_LIMIT = 60 * 1024 * 1024


def _cparams(sem):
    return pltpu.CompilerParams(dimension_semantics=sem, vmem_limit_bytes=VMEM_LIMIT)


def _const_spec(shape):
    n = len(shape)
    return pl.BlockSpec(shape, lambda *_: (0,) * n)


def _layer_spec(arr, layer, resident=False):
    n = arr.ndim - 1
    mode = dict(pipeline_mode=pl.Buffered(1)) if resident else {}
    return pl.BlockSpec((None,) + arr.shape[1:], lambda *_: (layer,) + (0,) * n, **mode)


def _silu(x):
    return x * jax.nn.sigmoid(x)


def _dot(a, b):
    return jnp.dot(a, b, preferred_element_type=F32)


def _dot_nt(a, b):
    return lax.dot_general(a, b, (((1,), (1,)), ((), ())), preferred_element_type=F32)


def _dot_tn(a, b):
    return lax.dot_general(a, b, (((0,), (0,)), ((), ())), preferred_element_type=F32)


def _lo_mask(rows):
    return lax.broadcasted_iota(jnp.int32, (rows, LANES), 1) < (LANES // 2)


def _split_halves(x):
    lo = _lo_mask(x.shape[0])
    zero = jnp.zeros_like(x)
    return jnp.concatenate([jnp.where(lo, x, zero), jnp.where(lo, zero, x)], axis=0)


def _group64_mean(x2):
    rows, n = x2.shape
    lo = _lo_mask(rows)
    outs = []
    for j in range(n // LANES):
        blk = x2[:, j * LANES:(j + 1) * LANES]
        s_lo = jnp.sum(jnp.where(lo, blk, 0.0), axis=-1, keepdims=True)
        s_hi = jnp.sum(jnp.where(lo, 0.0, blk), axis=-1, keepdims=True)
        outs.append(jnp.where(lo, s_lo, s_hi))
    out = outs[0] if len(outs) == 1 else jnp.concatenate(outs, axis=-1)
    return out * (1.0 / (LANES // 2))


def _group64_rmsnorm(x, g):
    return x * lax.rsqrt(_group64_mean(x * x) + EPS) * g


def _rmsnorm(x, g):
    return x * lax.rsqrt(jnp.mean(x * x, axis=-1, keepdims=True) + EPS) * g


def _rope(x, cos, sin):
    rows, n = x.shape
    lane = lax.broadcasted_iota(jnp.int32, (rows, LANES), 1)
    first = (lane % (2 * ROPE_PAIRS)) < ROPE_PAIRS
    outs = []
    for j in range(n // LANES):
        blk = x[:, j * LANES:(j + 1) * LANES]
        partner = jnp.where(first, pltpu.roll(blk, LANES - ROPE_PAIRS, 1), pltpu.roll(blk, ROPE_PAIRS, 1))
        outs.append(blk * cos + partner * sin)
    return jnp.concatenate(outs, axis=-1)


def _mod_kernel(cond_ref, w_ref, b_ref, o_ref):
    s = _silu(cond_ref[...]).astype(BF16)
    o_ref[0] = _dot(s, w_ref[0].astype(BF16)) + b_ref[0]


def _modulation(cond, w_mod, b_mod):
    depth, d, n = w_mod.shape
    rows = cond.shape[0]
    tn = n // MOD_COL_TILES
    return pl.pallas_call(
        _mod_kernel,
        grid=(depth, n // tn),
        in_specs=[
            _const_spec((rows, d)),
            pl.BlockSpec((1, d, tn), lambda l, j: (l, 0, j)),
            pl.BlockSpec((1, 1, tn), lambda l, j: (l, 0, j)),
        ],
        out_specs=pl.BlockSpec((1, rows, tn), lambda l, j: (l, 0, j)),
        out_shape=jax.ShapeDtypeStruct((depth, rows, n), F32),
        compiler_params=_cparams(("parallel", "parallel")),
        name="adaln_mod",
    )(cond, w_mod, b_mod.reshape(depth, 1, n))


def _mod_spec(mods, layer, row_of_step):
    return pl.BlockSpec((None, 1) + mods.shape[2:], lambda i: (layer, row_of_step(i), 0, 0))


def _proj_kernel(*refs, has_rope, seq_len, n_prev):
    if has_rope:
        (x_ref, mod_ref, g_ref, w_ref, sg_ref, ws_ref, sb_ref, qg_ref, kg_ref, cos_ref, sin_ref,
         ya_ref, qr_ref, kr_ref, vr_ref, gr_ref, qa_ref, ka_ref, va_ref) = refs
    elif n_prev:
        (x_ref, mod_ref, g_ref, w_ref, sg_ref, ws_ref, sb_ref, qg_ref, kg_ref, kt_prev_ref, vf_prev_ref,
         ya_ref, qr_ref, kr_ref, vr_ref, gr_ref, qa_ref, ka_ref, va_ref, kt_ref, vf_ref) = refs
        kt_ref[:, :n_prev] = kt_prev_ref[...]
        vf_ref[:, :n_prev] = vf_prev_ref[...]
    else:
        (x_ref, mod_ref, g_ref, w_ref, sg_ref, ws_ref, sb_ref, qg_ref, kg_ref,
         ya_ref, qr_ref, kr_ref, vr_ref, gr_ref, qa_ref, ka_ref, va_ref, kt_ref, vf_ref) = refs
    rows = x_ref.shape[0]
    m = mod_ref[0]
    hb = (_rmsnorm(x_ref[...], g_ref[...]) * (1.0 + m[1:2, :]) + m[0:1, :]).astype(BF16)
    wa = ya_ref.shape[1]
    wb = qr_ref.shape[1]
    wc = qa_ref.shape[1]

    def proj(c0, width):
        return _dot(hb, w_ref[:, c0:c0 + width])

    def emit_q(z):
        qn = _group64_rmsnorm(z, qg_ref[...])
        if has_rope:
            qn = _rope(qn, cos_ref[...], sin_ref[...])
        qa_ref[...] = (qn * (HD_C ** -0.5)).astype(BF16)

    def emit_k(z):
        kn = _group64_rmsnorm(z, kg_ref[...])
        if has_rope:
            kn = _rope(kn, cos_ref[...], sin_ref[...])
        else:
            for s in range(rows // seq_len):
                kt_ref[s, n_prev] = kn[s * seq_len:(s + 1) * seq_len, :].T
        ka_ref[...] = kn.astype(BF16)

    def emit_v(z):
        if not has_rope:
            for s in range(rows // seq_len):
                zs = z[s * seq_len:(s + 1) * seq_len, :]
                heads = jnp.stack([zs[:, h * LANES:(h + 1) * LANES] for h in range(H_C)], axis=0)
                vf_ref[s, n_prev] = jnp.swapaxes(heads, 0, 1)
        va_ref[...] = z.astype(BF16)

    def sgu_prepare(z):
        za = jax.nn.gelu(z)
        return za[:, :wa], _rmsnorm(za[:, wa:], sg_ref[...]).astype(BF16)

    def sgu_mix(u, v):
        lane = lax.broadcasted_iota(jnp.int32, (CHUNK, wa), 1)
        dg = wa // G_A
        for c in range(rows // CHUNK):
            rs = slice(c * CHUNK, (c + 1) * CHUNK)
            s = _dot(ws_ref[G_A - 1], v[rs])
            for g in range(G_A - 2, -1, -1):
                s = jnp.where(lane < (g + 1) * dg, _dot(ws_ref[g], v[rs]), s)
            ya_ref[rs, :] = (u[rs] * (s + sb_ref[...])).astype(BF16)

    def emit_ret_qk(z):
        qr_ref[...] = z[:, :wb].astype(BF16)
        kr_ref[...] = z[:, wb:] * (DK_B ** -0.5)

    def emit_ret_vg(z):
        vr_ref[...] = z[:, :wb].astype(BF16)
        gr_ref[...] = _silu(z[:, wb:])

    cb = 2 * wa
    cc = cb + 4 * wb
    z_a = proj(0, 2 * wa)
    z_q = proj(cc, wc)
    u, v = sgu_prepare(z_a)
    z_k = proj(cc + wc, wc)
    sgu_mix(u, v)
    emit_q(z_q)
    z_v = proj(cc + 2 * wc, wc)
    emit_k(z_k)
    z_r = proj(cb, 2 * wb)
    emit_v(z_v)
    z_g = proj(cb + 2 * wb, 2 * wb)
    emit_ret_qk(z_r)
    emit_ret_vg(z_g)


def _stacked_spec(lead, layers, tail):
    return pl.BlockSpec((lead, layers) + tail, lambda i: (i, 0) + (0,) * len(tail))


def _projection(x, mods, mod_row, layer, seq_len, g, w_bf16, sgu_g, sgu_w, sgu_bias, qg, kg, rope, prev_cache):
    t, d = x.shape
    rows = 2 * PROJ_ROWS if rope is not None else PROJ_ROWS
    wa = sgu_g.shape[2]
    wb = wa
    wc = qg.shape[2]
    has_rope = rope is not None
    tiles_per_seq = max(1, seq_len // rows)
    row_spec = lambda w: pl.BlockSpec((rows, w), lambda i: (i, 0))
    params = [g, w_bf16, sgu_g, sgu_w, sgu_bias, qg, kg]
    in_specs = [row_spec(d), _mod_spec(mods, layer, lambda i: mod_row(i // tiles_per_seq))]
    in_specs += [_layer_spec(p, layer) for p in params]
    args = [x, mods] + params
    outs = [(wa, BF16), (wb, BF16), (wb, F32), (wb, BF16), (wb, F32), (wc, BF16), (wc, BF16), (wc, BF16)]
    out_specs = [row_spec(w) for w, _ in outs]
    out_shape = [jax.ShapeDtypeStruct((t, w), dt) for w, dt in outs]
    if has_rope:
        tab_spec = pl.BlockSpec((rows, LANES), lambda i: (i % tiles_per_seq, 0))
        in_specs += [tab_spec, tab_spec]
        args += list(rope)
    else:
        seqs = rows // seq_len
        k_tail, v_tail = (wc, seq_len), (seq_len, H_C, LANES)
        if layer:
            in_specs += [_stacked_spec(seqs, layer, k_tail), _stacked_spec(seqs, layer, v_tail)]
            args += list(prev_cache)
        out_specs += [_stacked_spec(seqs, layer + 1, k_tail), _stacked_spec(seqs, layer + 1, v_tail)]
        out_shape += [jax.ShapeDtypeStruct((t // seq_len, layer + 1) + k_tail, F32),
                      jax.ShapeDtypeStruct((t // seq_len, layer + 1) + v_tail, F32)]
    return pl.pallas_call(
        functools.partial(_proj_kernel, has_rope=has_rope, seq_len=seq_len, n_prev=0 if has_rope else layer),
        grid=(t // rows,),
        in_specs=in_specs,
        out_specs=out_specs,
        out_shape=out_shape,
        compiler_params=_cparams(("parallel",)),
        name="projection",
    )(*args)


def _log_sigmoid(x):
    y = -x
    return -(jnp.maximum(y, 0.0) + jnp.log1p(jnp.exp(-jnp.abs(y))))


def _ret_kernel(*refs, chunk, n_chunks, has_state, n_prev):
    if has_state:
        (q_ref, k_ref, v_ref, gate_ref, r0f_ref, r0b_ref, lgp_ref, lgh_ref, norm_ref,
         y_ref, d_scr, dec_scr, kv_scr, rs_scr, o_scr) = refs
        state_outs = None
    elif n_prev:
        (q_ref, k_ref, v_ref, gate_ref, rf_prev_ref, rb_prev_ref, lgp_ref, lgh_ref, norm_ref,
         y_ref, rf_ref, rb_ref, d_scr, dec_scr, kv_scr, rs_scr, o_scr) = refs
        rf_ref[:, :n_prev] = rf_prev_ref[...]
        rb_ref[:, :n_prev] = rb_prev_ref[...]
        state_outs = (rf_ref, rb_ref)
    else:
        (q_ref, k_ref, v_ref, gate_ref, lgp_ref, lgh_ref, norm_ref,
         y_ref, rf_ref, rb_ref, d_scr, dec_scr, kv_scr, rs_scr, o_scr) = refs
        state_outs = (rf_ref, rb_ref)
    bt = q_ref.shape[0]
    n_pairs = H_B // 2
    half = LANES // 2
    cs = chunk

    @pl.when(pl.program_id(0) == 0)
    def _build_decay_tables():
        lgp = _log_sigmoid(lgp_ref[...])
        lgh = _log_sigmoid(lgh_ref[...])
        dist = (lax.broadcasted_iota(jnp.int32, (cs, cs), 0) - lax.broadcasted_iota(jnp.int32, (cs, cs), 1)).astype(F32)
        for h in range(H_B):
            lf = lgh[h:h + 1, :]
            lb = lgh[H_B + h:H_B + h + 1, :]
            d_f = jnp.where(dist >= 0, jnp.exp(lf * jnp.maximum(dist, 0.0)), 0.0)
            d_b = jnp.where(dist <= 0, jnp.exp(lb * jnp.maximum(-dist, 0.0)), 0.0)
            d_scr[h // 2, (h % 2) * cs:(h % 2 + 1) * cs, :] = d_f + d_b
        pos = lax.broadcasted_iota(jnp.int32, (cs, LANES), 0).astype(F32)
        for p in range(n_pairs):
            lf = lgp[p:p + 1, :]
            lb = lgp[n_pairs + p:n_pairs + p + 1, :]
            dec_scr[0, p] = jnp.exp(lf * (pos + 1.0))
            dec_scr[1, p] = jnp.exp(lf * (cs - 1.0 - pos))
            dec_scr[2, p] = jnp.exp(lb * (cs - pos))
            dec_scr[3, p] = jnp.exp(lb * pos)
            dec_scr[4, p] = jnp.broadcast_to(jnp.exp(lf * float(cs)), (cs, LANES))
            dec_scr[5, p] = jnp.broadcast_to(jnp.exp(lb * float(cs)), (cs, LANES))

    lo = _lo_mask(cs)
    blockdiag = ((lax.broadcasted_iota(jnp.int32, (LANES, LANES), 0) < half)
                 == (lax.broadcasted_iota(jnp.int32, (LANES, LANES), 1) < half))

    for b in range(bt):
        def intra(c, carry):
            rows = pl.ds(pl.multiple_of(c * cs, cs), cs)
            qc, kc, vc = q_ref[b, rows, :], k_ref[b, rows, :], v_ref[b, rows, :]
            for p in range(n_pairs):
                sl = slice(p * LANES, (p + 1) * LANES)
                k128, vb = kc[:, sl], vc[:, sl]
                inner = (_dot_nt(_split_halves(qc[:, sl]), k128.astype(BF16)) * d_scr[p]).astype(BF16)
                oo = _dot(inner, vb)
                o_scr[rows, sl] = jnp.where(lo, oo[:cs], oo[cs:])
                kk = jnp.concatenate([k128 * dec_scr[1, p], k128 * dec_scr[3, p]], axis=1).astype(BF16)
                kv = _dot_tn(kk, vb)
                kv_scr[0, c, p] = jnp.where(blockdiag, kv[:LANES], 0.0)
                kv_scr[1, c, p] = jnp.where(blockdiag, kv[LANES:], 0.0)
            return carry

        lax.fori_loop(0, n_chunks, intra, 0, unroll=min(n_chunks, RET_UNROLL))

        for p in range(n_pairs):
            for d in range(2):
                if has_state:
                    r0_ref = (r0f_ref, r0b_ref)[d]
                    zero = jnp.zeros((half, half), F32)
                    top = jnp.concatenate([r0_ref[b, 2 * p], zero], axis=1)
                    bot = jnp.concatenate([zero, r0_ref[b, 2 * p + 1]], axis=1)
                    r = jnp.concatenate([top, bot], axis=0)
                else:
                    r = jnp.zeros((LANES, LANES), F32)
                order = range(n_chunks) if d == 0 else range(n_chunks - 1, -1, -1)
                for c in order:
                    rs_scr[c, p, :, d * LANES:(d + 1) * LANES] = r.astype(BF16)
                    r = r * dec_scr[4 + d, p, :LANES, :] + kv_scr[d, c, p]
                if state_outs is not None:
                    state_outs[d][b, n_prev, 2 * p] = r[:half, :half]
                    state_outs[d][b, n_prev, 2 * p + 1] = r[half:, half:]

        def cross(c, carry):
            rows = pl.ds(pl.multiple_of(c * cs, cs), cs)
            qc = q_ref[b, rows, :]
            outs = []
            for p in range(n_pairs):
                sl = slice(p * LANES, (p + 1) * LANES)
                t = _dot(qc[:, sl], rs_scr[c, p])
                outs.append(o_scr[rows, sl] + t[:, :LANES] * dec_scr[0, p] + t[:, LANES:] * dec_scr[2, p])
            o = _group64_rmsnorm(jnp.concatenate(outs, axis=-1), norm_ref[...])
            y_ref[b, rows, :] = (gate_ref[b, rows, :] * o).astype(BF16)
            return carry

        lax.fori_loop(0, n_chunks, cross, 0, unroll=min(n_chunks, RET_UNROLL))


def _mixer_b(q, k, v, gate, layer, lg_pair, lg_head, norm, state, prev_states):
    b, l, wb = q.shape
    bt = max(1, RET_ROWS // l)
    chunk = min(l, RET_CHUNK)
    n_chunks = l // chunk
    has_state = state is not None
    tile = pl.BlockSpec((bt, l, wb), lambda i: (i, 0, 0))
    in_specs = [tile, tile, tile, tile]
    args = [q, k, v, gate]
    out_specs = [tile]
    out_shape = [jax.ShapeDtypeStruct((b, l, wb), BF16)]
    s_tail = (H_B, DK_B, DK_B)
    if has_state:
        sspec = pl.BlockSpec((bt, None) + s_tail, lambda i: (i, layer, 0, 0, 0))
        in_specs += [sspec, sspec]
        args += list(state)
    else:
        if layer:
            in_specs += [_stacked_spec(bt, layer, s_tail)] * 2
            args += list(prev_states)
        out_specs += [_stacked_spec(bt, layer + 1, s_tail)] * 2
        out_shape += [jax.ShapeDtypeStruct((b, layer + 1) + s_tail, F32)] * 2
    params = [lg_pair, lg_head, norm]
    in_specs += [_layer_spec(p, layer) for p in params]
    args += params
    n_pairs = H_B // 2
    return pl.pallas_call(
        functools.partial(_ret_kernel, chunk=chunk, n_chunks=n_chunks, has_state=has_state,
                          n_prev=0 if has_state else layer),
        grid=(b // bt,),
        in_specs=in_specs,
        out_specs=out_specs,
        out_shape=out_shape,
        scratch_shapes=[
            pltpu.VMEM((n_pairs, 2 * chunk, chunk), F32),
            pltpu.VMEM((6, n_pairs, chunk, LANES), F32),
            pltpu.VMEM((2, n_chunks, n_pairs, LANES, LANES), F32),
            pltpu.VMEM((n_chunks, n_pairs, LANES, 2 * LANES), BF16),
            pltpu.VMEM((l, wb), F32),
        ],
        compiler_params=_cparams(("arbitrary",)),
        name="retention",
    )(*args)


def _attn_kernel(*refs, has_cache, lam_init):
    if has_cache:
        (q_ref, k_ref, v_ref, ck_ref, cv_ref, dl_ref, dn_ref, y_ref, k_scr, v_scr) = refs
        n_past = ck_ref.shape[1]

        @pl.when(pl.program_id(1) == 0)
        def _gather_keys():
            k_scr[...] = ck_ref[...].T.astype(BF16)
            cv = jnp.swapaxes(cv_ref[...], 0, 1)
            for h in range(H_C):
                v_scr[:, h * LANES:(h + 1) * LANES] = cv[h].astype(BF16)

        keys = lambda b, sl: jnp.concatenate([k_scr[:, sl], k_ref[b, :, sl]], axis=0)
        vals = lambda b, sl: jnp.concatenate([v_scr[:, sl], v_ref[b, :, sl]], axis=0)
    else:
        (q_ref, k_ref, v_ref, dl_ref, dn_ref, y_ref) = refs
        keys = lambda b, sl: k_ref[b, :, sl]
        vals = lambda b, sl: v_ref[b, :, sl]
    bt, tq, _ = q_ref.shape
    dl = dl_ref[...]
    lam = (jnp.exp(jnp.sum(dl[0:1] * dl[1:2], axis=-1, keepdims=True))
           - jnp.exp(jnp.sum(dl[2:3] * dl[3:4], axis=-1, keepdims=True)) + lam_init)
    units = [(b, slice(h * LANES, (h + 1) * LANES)) for b in range(bt) for h in range(H_C)]
    scores = lambda b, sl: _dot_nt(keys(b, sl), _split_halves(q_ref[b, :, sl]))
    st_next = scores(*units[0])
    for i, (b, sl) in enumerate(units):
        st = st_next
        if i + 1 < len(units):
            st_next = scores(*units[i + 1])
        e = jnp.exp(st - jnp.max(st, axis=0, keepdims=True))
        l = jnp.sum(e, axis=0, keepdims=True)
        w = e * jnp.concatenate([1.0 / l[:, :tq], lam / l[:, tq:]], axis=1)
        o = _dot_tn((w[:, :tq] - w[:, tq:]).astype(BF16), vals(b, sl))
        y_ref[b, :, sl] = (_rmsnorm(o, dn_ref[...]) * (1.0 - lam_init)).astype(BF16)


def _mixer_c(q, k, v, layer, dl, dn, lam_init, cache):
    b, l, wc = q.shape
    has_cache = cache is not None
    tq = min(l, ATTN_Q_ROWS)
    bt = 1 if has_cache else max(1, ATTN_SEQ_ROWS // l)
    q_spec = pl.BlockSpec((bt, tq, wc), lambda i, j: (i, j, 0))
    kv_spec = pl.BlockSpec((bt, l, wc), lambda i, j: (i, 0, 0))
    in_specs = [q_spec, kv_spec, kv_spec]
    args = [q, k, v]
    n_keys = l
    if has_cache:
        ck_t, cv = cache
        n_past = cv.shape[2]
        n_keys += n_past
        in_specs += [pl.BlockSpec((None, None, wc, n_past), lambda i, j: (i, layer, 0, 0)),
                     pl.BlockSpec((None, None, n_past, H_C, LANES), lambda i, j: (i, layer, 0, 0, 0))]
        args += [ck_t, cv]
    in_specs += [_layer_spec(dl, layer), _layer_spec(dn, layer)]
    args += [dl, dn]
    scratch = [pltpu.VMEM((n_past, wc), BF16), pltpu.VMEM((n_past, wc), BF16)] if has_cache else []
    return pl.pallas_call(
        functools.partial(_attn_kernel, has_cache=has_cache, lam_init=lam_init),
        grid=(b // bt, l // tq),
        in_specs=in_specs,
        out_specs=q_spec,
        out_shape=jax.ShapeDtypeStruct((b, l, wc), BF16),
        scratch_shapes=scratch,
        compiler_params=_cparams(("parallel", "arbitrary")),
        name="diff_attention",
    )(*args)


def _mlp_kernel(ya_ref, yb_ref, yc_ref, x_ref, mod_ref, g_ref, wo_ref, up_ref, cw_ref, cb_ref, down_ref,
                o_ref, h_scr, g_scr, *, seq_len):
    rows = x_ref.shape[0]
    m = mod_ref[0]
    y = None
    k0 = 0
    for y_ref in (ya_ref, yb_ref, yc_ref):
        kw = y_ref.shape[1]
        part = _dot(y_ref[...], wo_ref[k0:k0 + kw, :])
        y = part if y is None else y + part
        k0 += kw
    x1 = x_ref[...] + m[2:3, :] * y
    o_ref[...] = x1
    h_scr[...] = (_rmsnorm(x1, g_ref[...]) * (1.0 + m[4:5, :]) + m[3:4, :]).astype(BF16)

    d_ff = down_ref.shape[0]
    fc = MXU_DIM
    pos = lax.broadcasted_iota(jnp.int32, (rows, fc), 0) % seq_len
    has_prev = pos > 0
    has_next = pos < seq_len - 1

    def conv(c0):
        u = _dot(h_scr[...], up_ref[:, c0:c0 + fc])
        w = cw_ref[:, c0:c0 + fc]
        prev = jnp.where(has_prev, pltpu.roll(u, 1, 0), 0.0)
        nxt = jnp.where(has_next, pltpu.roll(u, rows - 1, 0), 0.0)
        return prev * w[0:1, :] + u * w[1:2, :] + nxt * w[2:3, :] + cb_ref[:, c0:c0 + fc]

    for j in range(d_ff // fc):
        a = conv(j * fc)
        b = conv(d_ff + j * fc)
        g_scr[:, j * fc:(j + 1) * fc] = (_silu(a) * b).astype(BF16)

    o_ref[...] = o_ref[...] + m[5:6, :] * _dot(g_scr[...], down_ref[...])


def _mlp(ya, yb, yc, x, mods, mod_row, layer, seq_len, g, w_out, up, cw, cb, down):
    t, d = x.shape
    rows = MLP_ROWS
    tiles_per_seq = max(1, seq_len // rows)
    row_spec = lambda a: pl.BlockSpec((rows, a.shape[1]), lambda i: (i, 0))
    weights = [w_out, up, cw, cb, down]
    return pl.pallas_call(
        functools.partial(_mlp_kernel, seq_len=seq_len),
        grid=(t // rows,),
        in_specs=[row_spec(ya), row_spec(yb), row_spec(yc), row_spec(x),
                  _mod_spec(mods, layer, lambda i: mod_row(i // tiles_per_seq)), _layer_spec(g, layer)]
                 + [_layer_spec(w, layer, resident=True) for w in weights],
        out_specs=row_spec(x),
        out_shape=jax.ShapeDtypeStruct((t, d), F32),
        scratch_shapes=[pltpu.VMEM((rows, d), BF16), pltpu.VMEM((rows, down.shape[1]), BF16)],
        compiler_params=_cparams(("parallel",)),
        name="out_proj_mlp",
    )(ya, yb, yc, x, mods, g, *weights)


def _rope_tables(l):
    rows = l // GRID_W
    t_row = jnp.repeat(jnp.arange(rows, dtype=F32), GRID_W)
    t_col = jnp.tile(jnp.arange(GRID_W, dtype=F32), rows)
    inv = ROPE_BASE ** (-jnp.arange(ROPE_PAIRS, dtype=F32) / ROPE_PAIRS)
    ar, ac = t_row[:, None] * inv, t_col[:, None] * inv
    cos = jnp.concatenate([jnp.cos(ar), jnp.cos(ar), jnp.cos(ac), jnp.cos(ac)], axis=-1)
    sin = jnp.concatenate([-jnp.sin(ar), jnp.sin(ar), -jnp.sin(ac), jnp.sin(ac)], axis=-1)
    return jnp.tile(cos, (1, 2)), jnp.tile(sin, (1, 2))


def kernel(x_prompt, x_sample, c, cache_k, cache_v, state_ret_fwd, state_ret_bwd, c_ctx, norm1, w_mod, b_mod,
           w_in, sgu_norm, sgu_w, sgu_b, ret_logit_fwd, ret_logit_bwd, ret_norm, q_norm, k_norm, diff_lam,
           diff_norm, w_out, norm2, ffn_up, ffn_conv, ffn_conv_b, ffn_down):
    depth, d_model, _ = w_mod.shape
    batch, seq, _ = x_prompt.shape
    dec_batch, dec_seq, _ = x_sample.shape
    w_a = sgu_norm.shape[1]
    w_b = H_B * ret_norm.shape[2]
    w_c = 2 * H_C * HD_C
    past = cache_k.shape[2]

    pad = (-(dec_batch + 1)) % 8
    cond = jnp.concatenate([c, c_ctx[None, :], jnp.zeros((pad, d_model), F32)], axis=0)
    mods = _modulation(cond, w_mod, b_mod).reshape(depth, cond.shape[0], 6, d_model)

    w_in_b = w_in.astype(BF16)
    w_out_b = w_out.astype(BF16)
    sgu_w_b = sgu_w.astype(BF16)
    up_b = ffn_up.astype(BF16)
    down_b = ffn_down.astype(BF16)
    row3 = lambda a: a.reshape(depth, 1, -1)
    sgu_bias = jnp.repeat(jnp.swapaxes(sgu_b, 1, 2), w_a // G_A, axis=2)
    logits = jnp.stack([ret_logit_fwd, ret_logit_bwd], axis=1)
    lg_pair = jnp.repeat(logits, DK_B, axis=2).reshape(depth, 2 * (H_B // 2), LANES)
    lg_head = logits.reshape(depth, 2 * H_B, 1)
    qg = row3(jnp.tile(q_norm, (1, 2 * H_C)))
    kg = row3(jnp.tile(k_norm, (1, 2 * H_C)))
    norm1_r, norm2_r, sgu_g, ret_g, diff_g = row3(norm1), row3(norm2), row3(sgu_norm), row3(ret_norm), row3(diff_norm)
    conv_b = row3(ffn_conv_b)
    rope = _rope_tables(dec_seq)
    cache_kt = jnp.transpose(cache_k, (0, 1, 3, 4, 5, 2)).reshape(dec_batch, depth, w_c, past)

    def layer(x, l, nb, sl, mod_row, is_sample, new_cache, new_states):
        lam_init = 0.8 - 0.6 * math.exp(-0.3 * l)
        outs = _projection(x, mods, mod_row, l, sl, norm1_r, w_in_b, sgu_g, sgu_w_b, sgu_bias, qg, kg,
                           rope if is_sample else None, new_cache)
        ya, qr, kr, vr, gr, qa, ka, va = outs[:8]
        seq3 = lambda a: a.reshape(nb, sl, a.shape[1])
        state = (state_ret_fwd, state_ret_bwd) if is_sample else None
        yb, *states = _mixer_b(seq3(qr), seq3(kr), seq3(vr), seq3(gr), l, lg_pair, lg_head, ret_g, state,
                               new_states)
        cache = (cache_kt, cache_v) if is_sample else None
        yc = _mixer_c(seq3(qa), seq3(ka), seq3(va), l, diff_lam, diff_g, lam_init, cache)
        x = _mlp(ya, yb.reshape(nb * sl, w_b), yc.reshape(nb * sl, w_c), x, mods, mod_row, l, sl, norm2_r,
                 w_out_b, up_b, ffn_conv, conv_b, down_b)
        return x, outs[8:], states

    y_prompt = x_prompt.reshape(batch * seq, d_model)
    new_cache, new_states = None, None
    for l in range(depth):
        y_prompt, new_cache, new_states = layer(y_prompt, l, batch, seq, lambda b: dec_batch, False,
                                                new_cache, new_states)

    y_sample = x_sample.reshape(dec_batch * dec_seq, d_model)
    for l in range(depth):
        y_sample, _, _ = layer(y_sample, l, dec_batch, dec_seq, lambda b: b, True, None, None)

    new_kt, new_v = new_cache
    new_k = jnp.transpose(new_kt.reshape(batch, depth, H_C, 2, HD_C, seq), (0, 1, 5, 2, 3, 4))
    return (y_prompt.reshape(batch, seq, d_model), y_sample.reshape(dec_batch, dec_seq, d_model),
            new_k, new_v, new_states[0], new_states[1])
```

```python
import functools
import math

import jax
import jax.numpy as jnp
from jax import lax
from jax.experimental import pallas as pl
from jax.experimental.pallas import tpu as pltpu

F32 = jnp.float32
BF16 = jnp.bfloat16

GRID_W = 64
CHUNK = 128
EPS = 1e-6
ROPE_BASE = 10000.0
G_A = 4
H_B = 4
DK_B = 64
H_C = 4
HD_C = 64
ROPE_PAIRS = HD_C // 4
LANES = 128
MXU_DIM = 256
MOD_COL_TILES = 4
PROJ_ROWS = 512
MLP_ROWS = 1024
ATTN_Q_ROWS = 256
ATTN_KEY_BLOCK = 128
ATTN_SEQ_ROWS = 1024
RET_ROWS = 2048
RET_CHUNK = 256
RET_UNROLL = 4
VMEM_LIMIT = 60 * 1024 * 1024


def _cparams(sem):
    return pltpu.CompilerParams(dimension_semantics=sem, vmem_limit_bytes=VMEM_LIMIT)


def _const_spec(shape):
    n = len(shape)
    return pl.BlockSpec(shape, lambda *_: (0,) * n)


def _layer_spec(arr, layer, resident=False):
    n = arr.ndim - 1
    mode = dict(pipeline_mode=pl.Buffered(1)) if resident else {}
    return pl.BlockSpec((None,) + arr.shape[1:], lambda *_: (layer,) + (0,) * n, **mode)


def _silu(x):
    return x * jax.nn.sigmoid(x)


def _dot(a, b):
    return jnp.dot(a, b, preferred_element_type=F32)


def _dot_nt(a, b):
    return lax.dot_general(a, b, (((1,), (1,)), ((), ())), preferred_element_type=F32)


def _dot_tn(a, b):
    return lax.dot_general(a, b, (((0,), (0,)), ((), ())), preferred_element_type=F32)


def _lo_mask(rows):
    return lax.broadcasted_iota(jnp.int32, (rows, LANES), 1) < (LANES // 2)


def _split_halves(x):
    lo = _lo_mask(x.shape[0])
    zero = jnp.zeros_like(x)
    return jnp.concatenate([jnp.where(lo, x, zero), jnp.where(lo, zero, x)], axis=0)


def _group64_mean(x2):
    rows, n = x2.shape
    lo = _lo_mask(rows)
    outs = []
    for j in range(n // LANES):
        blk = x2[:, j * LANES:(j + 1) * LANES]
        s_lo = jnp.sum(jnp.where(lo, blk, 0.0), axis=-1, keepdims=True)
        s_hi = jnp.sum(jnp.where(lo, 0.0, blk), axis=-1, keepdims=True)
        outs.append(jnp.where(lo, s_lo, s_hi))
    out = outs[0] if len(outs) == 1 else jnp.concatenate(outs, axis=-1)
    return out * (1.0 / (LANES // 2))


def _group64_rmsnorm(x, g):
    return x * lax.rsqrt(_group64_mean(x * x) + EPS) * g


def _rmsnorm(x, g):
    return x * lax.rsqrt(jnp.mean(x * x, axis=-1, keepdims=True) + EPS) * g


def _rope(x, cos, sin):
    rows, n = x.shape
    lane = lax.broadcasted_iota(jnp.int32, (rows, LANES), 1)
    first = (lane % (2 * ROPE_PAIRS)) < ROPE_PAIRS
    outs = []
    for j in range(n // LANES):
        blk = x[:, j * LANES:(j + 1) * LANES]
        partner = jnp.where(first, pltpu.roll(blk, LANES - ROPE_PAIRS, 1), pltpu.roll(blk, ROPE_PAIRS, 1))
        outs.append(blk * cos + partner * sin)
    return jnp.concatenate(outs, axis=-1)


def _mod_kernel(cond_ref, w_ref, b_ref, o_ref):
    s = _silu(cond_ref[...]).astype(BF16)
    o_ref[0] = _dot(s, w_ref[0].astype(BF16)) + b_ref[0]


def _modulation(cond, w_mod, b_mod):
    depth, d, n = w_mod.shape
    rows = cond.shape[0]
    tn = n // MOD_COL_TILES
    return pl.pallas_call(
        _mod_kernel,
        grid=(depth, n // tn),
        in_specs=[
            _const_spec((rows, d)),
            pl.BlockSpec((1, d, tn), lambda l, j: (l, 0, j)),
            pl.BlockSpec((1, 1, tn), lambda l, j: (l, 0, j)),
        ],
        out_specs=pl.BlockSpec((1, rows, tn), lambda l, j: (l, 0, j)),
        out_shape=jax.ShapeDtypeStruct((depth, rows, n), F32),
        compiler_params=_cparams(("parallel", "parallel")),
        name="adaln_mod",
    )(cond, w_mod, b_mod.reshape(depth, 1, n))


def _mod_spec(mods, layer, row_of_step):
    return pl.BlockSpec((None, 1) + mods.shape[2:], lambda i: (layer, row_of_step(i), 0, 0))


def _proj_kernel(*refs, has_rope, seq_len, n_prev):
    if has_rope:
        (x_ref, mod_ref, g_ref, w_ref, sg_ref, ws_ref, sb_ref, qg_ref, kg_ref, cos_ref, sin_ref,
         ya_ref, qr_ref, kr_ref, vr_ref, gr_ref, qa_ref, ka_ref, va_ref) = refs
    elif n_prev:
        (x_ref, mod_ref, g_ref, w_ref, sg_ref, ws_ref, sb_ref, qg_ref, kg_ref, kt_prev_ref, vf_prev_ref,
         ya_ref, qr_ref, kr_ref, vr_ref, gr_ref, qa_ref, ka_ref, va_ref, kt_ref, vf_ref) = refs
        kt_ref[:, :n_prev] = kt_prev_ref[...]
        vf_ref[:, :n_prev] = vf_prev_ref[...]
    else:
        (x_ref, mod_ref, g_ref, w_ref, sg_ref, ws_ref, sb_ref, qg_ref, kg_ref,
         ya_ref, qr_ref, kr_ref, vr_ref, gr_ref, qa_ref, ka_ref, va_ref, kt_ref, vf_ref) = refs
    rows = x_ref.shape[0]
    m = mod_ref[0]
    hb = (_rmsnorm(x_ref[...], g_ref[...]) * (1.0 + m[1:2, :]) + m[0:1, :]).astype(BF16)
    wa = ya_ref.shape[1]
    wb = qr_ref.shape[1]
    wc = qa_ref.shape[1]

    def proj(c0, width):
        return _dot(hb, w_ref[:, c0:c0 + width])

    def emit_q(z):
        qn = _group64_rmsnorm(z, qg_ref[...])
        if has_rope:
            qn = _rope(qn, cos_ref[...], sin_ref[...])
        qa_ref[...] = (qn * (HD_C ** -0.5 * math.log2(math.e))).astype(BF16)

    def emit_k(z):
        kn = _group64_rmsnorm(z, kg_ref[...])
        if has_rope:
            kn = _rope(kn, cos_ref[...], sin_ref[...])
        else:
            for s in range(rows // seq_len):
                kt_ref[s, n_prev] = kn[s * seq_len:(s + 1) * seq_len, :].T
        ka_ref[...] = kn.astype(BF16)

    def emit_v(z):
        if not has_rope:
            for s in range(rows // seq_len):
                zs = z[s * seq_len:(s + 1) * seq_len, :]
                heads = jnp.stack([zs[:, h * LANES:(h + 1) * LANES] for h in range(H_C)], axis=0)
                vf_ref[s, n_prev] = jnp.swapaxes(heads, 0, 1)
        va_ref[...] = z.astype(BF16)

    def sgu_prepare(z):
        za = jax.nn.gelu(z)
        return za[:, :wa], _rmsnorm(za[:, wa:], sg_ref[...]).astype(BF16)

    def sgu_mix(u, v):
        lane = lax.broadcasted_iota(jnp.int32, (CHUNK, wa), 1)
        dg = wa // G_A
        for c in range(rows // CHUNK):
            rs = slice(c * CHUNK, (c + 1) * CHUNK)
            s = _dot(ws_ref[G_A - 1], v[rs])
            for g in range(G_A - 2, -1, -1):
                s = jnp.where(lane < (g + 1) * dg, _dot(ws_ref[g], v[rs]), s)
            ya_ref[rs, :] = (u[rs] * (s + sb_ref[...])).astype(BF16)

    def emit_ret_qk(z):
        qr_ref[...] = z[:, :wb].astype(BF16)
        kr_ref[...] = z[:, wb:] * (DK_B ** -0.5)

    def emit_ret_vg(z):
        vr_ref[...] = z[:, :wb].astype(BF16)
        gr_ref[...] = _silu(z[:, wb:])

    cb = 2 * wa
    cc = cb + 4 * wb
    z_a = proj(0, 2 * wa)
    z_q = proj(cc, wc)
    u, v = sgu_prepare(z_a)
    z_k = proj(cc + wc, wc)
    sgu_mix(u, v)
    emit_q(z_q)
    z_v = proj(cc + 2 * wc, wc)
    emit_k(z_k)
    z_r = proj(cb, 2 * wb)
    emit_v(z_v)
    z_g = proj(cb + 2 * wb, 2 * wb)
    emit_ret_qk(z_r)
    emit_ret_vg(z_g)


def _stacked_spec(lead, layers, tail):
    return pl.BlockSpec((lead, layers) + tail, lambda i: (i, 0) + (0,) * len(tail))


def _projection(x, mods, mod_row, layer, seq_len, g, w_bf16, sgu_g, sgu_w, sgu_bias, qg, kg, rope, prev_cache):
    t, d = x.shape
    rows = 2 * PROJ_ROWS if rope is not None else PROJ_ROWS
    wa = sgu_g.shape[2]
    wb = wa
    wc = qg.shape[2]
    has_rope = rope is not None
    tiles_per_seq = max(1, seq_len // rows)
    row_spec = lambda w: pl.BlockSpec((rows, w), lambda i: (i, 0))
    params = [g, w_bf16, sgu_g, sgu_w, sgu_bias, qg, kg]
    in_specs = [row_spec(d), _mod_spec(mods, layer, lambda i: mod_row(i // tiles_per_seq))]
    in_specs += [_layer_spec(p, layer) for p in params]
    args = [x, mods] + params
    outs = [(wa, BF16), (wb, BF16), (wb, F32), (wb, BF16), (wb, F32), (wc, BF16), (wc, BF16), (wc, BF16)]
    out_specs = [row_spec(w) for w, _ in outs]
    out_shape = [jax.ShapeDtypeStruct((t, w), dt) for w, dt in outs]
    if has_rope:
        tab_spec = pl.BlockSpec((rows, LANES), lambda i: (i % tiles_per_seq, 0))
        in_specs += [tab_spec, tab_spec]
        args += list(rope)
    else:
        seqs = rows // seq_len
        k_tail, v_tail = (wc, seq_len), (seq_len, H_C, LANES)
        if layer:
            in_specs += [_stacked_spec(seqs, layer, k_tail), _stacked_spec(seqs, layer, v_tail)]
            args += list(prev_cache)
        out_specs += [_stacked_spec(seqs, layer + 1, k_tail), _stacked_spec(seqs, layer + 1, v_tail)]
        out_shape += [jax.ShapeDtypeStruct((t // seq_len, layer + 1) + k_tail, F32),
                      jax.ShapeDtypeStruct((t // seq_len, layer + 1) + v_tail, F32)]
    return pl.pallas_call(
        functools.partial(_proj_kernel, has_rope=has_rope, seq_len=seq_len, n_prev=0 if has_rope else layer),
        grid=(t // rows,),
        in_specs=in_specs,
        out_specs=out_specs,
        out_shape=out_shape,
        compiler_params=_cparams(("parallel",)),
        name="projection",
    )(*args)


def _log_sigmoid(x):
    y = -x
    return -(jnp.maximum(y, 0.0) + jnp.log1p(jnp.exp(-jnp.abs(y))))


def _ret_kernel(*refs, chunk, n_chunks, has_state, n_prev):
    if has_state:
        (q_ref, k_ref, v_ref, gate_ref, r0f_ref, r0b_ref, lgp_ref, lgh_ref, norm_ref,
         y_ref, d_scr, dec_scr, kv_scr, rs_scr, o_scr) = refs
        state_outs = None
    elif n_prev:
        (q_ref, k_ref, v_ref, gate_ref, rf_prev_ref, rb_prev_ref, lgp_ref, lgh_ref, norm_ref,
         y_ref, rf_ref, rb_ref, d_scr, dec_scr, kv_scr, rs_scr, o_scr) = refs
        rf_ref[:, :n_prev] = rf_prev_ref[...]
        rb_ref[:, :n_prev] = rb_prev_ref[...]
        state_outs = (rf_ref, rb_ref)
    else:
        (q_ref, k_ref, v_ref, gate_ref, lgp_ref, lgh_ref, norm_ref,
         y_ref, rf_ref, rb_ref, d_scr, dec_scr, kv_scr, rs_scr, o_scr) = refs
        state_outs = (rf_ref, rb_ref)
    bt = q_ref.shape[0]
    n_pairs = H_B // 2
    half = LANES // 2
    cs = chunk

    @pl.when(pl.program_id(0) == 0)
    def _build_decay_tables():
        lgp = _log_sigmoid(lgp_ref[...])
        lgh = _log_sigmoid(lgh_ref[...])
        dist = (lax.broadcasted_iota(jnp.int32, (cs, cs), 0) - lax.broadcasted_iota(jnp.int32, (cs, cs), 1)).astype(F32)
        for h in range(H_B):
            lf = lgh[h:h + 1, :]
            lb = lgh[H_B + h:H_B + h + 1, :]
            d_f = jnp.where(dist >= 0, jnp.exp(lf * jnp.maximum(dist, 0.0)), 0.0)
            d_b = jnp.where(dist <= 0, jnp.exp(lb * jnp.maximum(-dist, 0.0)), 0.0)
            d_scr[h // 2, (h % 2) * cs:(h % 2 + 1) * cs, :] = d_f + d_b
        pos = lax.broadcasted_iota(jnp.int32, (cs, LANES), 0).astype(F32)
        for p in range(n_pairs):
            lf = lgp[p:p + 1, :]
            lb = lgp[n_pairs + p:n_pairs + p + 1, :]
            dec_scr[0, p] = jnp.exp(lf * (pos + 1.0))
            dec_scr[1, p] = jnp.exp(lf * (cs - 1.0 - pos))
            dec_scr[2, p] = jnp.exp(lb * (cs - pos))
            dec_scr[3, p] = jnp.exp(lb * pos)
            dec_scr[4, p] = jnp.broadcast_to(jnp.exp(lf * float(cs)), (cs, LANES))
            dec_scr[5, p] = jnp.broadcast_to(jnp.exp(lb * float(cs)), (cs, LANES))

    lo = _lo_mask(cs)
    blockdiag = ((lax.broadcasted_iota(jnp.int32, (LANES, LANES), 0) < half)
                 == (lax.broadcasted_iota(jnp.int32, (LANES, LANES), 1) < half))

    for b in range(bt):
        def intra(c, carry):
            rows = pl.ds(pl.multiple_of(c * cs, cs), cs)
            qc, kc, vc = q_ref[b, rows, :], k_ref[b, rows, :], v_ref[b, rows, :]
            for p in range(n_pairs):
                sl = slice(p * LANES, (p + 1) * LANES)
                k128, vb = kc[:, sl], vc[:, sl]
                inner = (_dot_nt(_split_halves(qc[:, sl]), k128.astype(BF16)) * d_scr[p]).astype(BF16)
                oo = _dot(inner, vb)
                o_scr[rows, sl] = jnp.where(lo, oo[:cs], oo[cs:])
                kk = jnp.concatenate([k128 * dec_scr[1, p], k128 * dec_scr[3, p]], axis=1).astype(BF16)
                kv = _dot_tn(kk, vb)
                kv_scr[0, c, p] = jnp.where(blockdiag, kv[:LANES], 0.0)
                kv_scr[1, c, p] = jnp.where(blockdiag, kv[LANES:], 0.0)
            return carry

        lax.fori_loop(0, n_chunks, intra, 0, unroll=min(n_chunks, RET_UNROLL))

        for p in range(n_pairs):
            for d in range(2):
                if has_state:
                    r0_ref = (r0f_ref, r0b_ref)[d]
                    zero = jnp.zeros((half, half), F32)
                    top = jnp.concatenate([r0_ref[b, 2 * p], zero], axis=1)
                    bot = jnp.concatenate([zero, r0_ref[b, 2 * p + 1]], axis=1)
                    r = jnp.concatenate([top, bot], axis=0)
                else:
                    r = jnp.zeros((LANES, LANES), F32)
                order = range(n_chunks) if d == 0 else range(n_chunks - 1, -1, -1)
                for c in order:
                    rs_scr[c, p, :, d * LANES:(d + 1) * LANES] = r.astype(BF16)
                    r = r * dec_scr[4 + d, p, :LANES, :] + kv_scr[d, c, p]
                if state_outs is not None:
                    state_outs[d][b, n_prev, 2 * p] = r[:half, :half]
                    state_outs[d][b, n_prev, 2 * p + 1] = r[half:, half:]

        def cross(c, carry):
            rows = pl.ds(pl.multiple_of(c * cs, cs), cs)
            qc = q_ref[b, rows, :]
            outs = []
            for p in range(n_pairs):
                sl = slice(p * LANES, (p + 1) * LANES)
                t = _dot(qc[:, sl], rs_scr[c, p])
                outs.append(o_scr[rows, sl] + t[:, :LANES] * dec_scr[0, p] + t[:, LANES:] * dec_scr[2, p])
            o = _group64_rmsnorm(jnp.concatenate(outs, axis=-1), norm_ref[...])
            y_ref[b, rows, :] = (gate_ref[b, rows, :] * o).astype(BF16)
            return carry

        lax.fori_loop(0, n_chunks, cross, 0, unroll=min(n_chunks, RET_UNROLL))


def _mixer_b(q, k, v, gate, layer, lg_pair, lg_head, norm, state, prev_states):
    b, l, wb = q.shape
    bt = max(1, RET_ROWS // l)
    chunk = min(l, RET_CHUNK)
    n_chunks = l // chunk
    has_state = state is not None
    tile = pl.BlockSpec((bt, l, wb), lambda i: (i, 0, 0))
    in_specs = [tile, tile, tile, tile]
    args = [q, k, v, gate]
    out_specs = [tile]
    out_shape = [jax.ShapeDtypeStruct((b, l, wb), BF16)]
    s_tail = (H_B, DK_B, DK_B)
    if has_state:
        sspec = pl.BlockSpec((bt, None) + s_tail, lambda i: (i, layer, 0, 0, 0))
        in_specs += [sspec, sspec]
        args += list(state)
    else:
        if layer:
            in_specs += [_stacked_spec(bt, layer, s_tail)] * 2
            args += list(prev_states)
        out_specs += [_stacked_spec(bt, layer + 1, s_tail)] * 2
        out_shape += [jax.ShapeDtypeStruct((b, layer + 1) + s_tail, F32)] * 2
    params = [lg_pair, lg_head, norm]
    in_specs += [_layer_spec(p, layer) for p in params]
    args += params
    n_pairs = H_B // 2
    return pl.pallas_call(
        functools.partial(_ret_kernel, chunk=chunk, n_chunks=n_chunks, has_state=has_state,
                          n_prev=0 if has_state else layer),
        grid=(b // bt,),
        in_specs=in_specs,
        out_specs=out_specs,
        out_shape=out_shape,
        scratch_shapes=[
            pltpu.VMEM((n_pairs, 2 * chunk, chunk), F32),
            pltpu.VMEM((6, n_pairs, chunk, LANES), F32),
            pltpu.VMEM((2, n_chunks, n_pairs, LANES, LANES), F32),
            pltpu.VMEM((n_chunks, n_pairs, LANES, 2 * LANES), BF16),
            pltpu.VMEM((l, wb), F32),
        ],
        compiler_params=_cparams(("arbitrary",)),
        name="retention",
    )(*args)


def _attn_kernel(*refs, has_cache, lam_init):
    if has_cache:
        (q_ref, k_ref, v_ref, ck_ref, cv_ref, dl_ref, dn_ref, y_ref, k_scr, v_scr) = refs
        n_past = ck_ref.shape[1]

        @pl.when(pl.program_id(1) == 0)
        def _gather_keys():
            k_scr[...] = ck_ref[...].T.astype(BF16)
            cv = jnp.swapaxes(cv_ref[...], 0, 1)
            for h in range(H_C):
                v_scr[:, h * LANES:(h + 1) * LANES] = cv[h].astype(BF16)

        keys = lambda b, sl: jnp.concatenate([k_scr[:, sl], k_ref[b, :, sl]], axis=0)
        vals = lambda b, sl: jnp.concatenate([v_scr[:, sl], v_ref[b, :, sl]], axis=0)
    else:
        (q_ref, k_ref, v_ref, dl_ref, dn_ref, y_ref) = refs
        keys = lambda b, sl: k_ref[b, :, sl]
        vals = lambda b, sl: v_ref[b, :, sl]
    bt, tq, _ = q_ref.shape
    dl = dl_ref[...]
    lam = (jnp.exp(jnp.sum(dl[0:1] * dl[1:2], axis=-1, keepdims=True))
           - jnp.exp(jnp.sum(dl[2:3] * dl[3:4], axis=-1, keepdims=True)) + lam_init)
    units = [(b, slice(h * LANES, (h + 1) * LANES)) for b in range(bt) for h in range(H_C)]
    scores = lambda b, sl: _dot_nt(keys(b, sl), _split_halves(q_ref[b, :, sl]))
    st_next = scores(*units[0])
    for i, (b, sl) in enumerate(units):
        st = st_next
        if i + 1 < len(units):
            st_next = scores(*units[i + 1])
        kb = ATTN_KEY_BLOCK
        blocks = [st[j:j + kb] for j in range(0, st.shape[0], kb)]
        m_j = [jnp.max(blk, axis=0, keepdims=True) for blk in blocks]
        e_j = [jnp.exp2(blk - mj) for blk, mj in zip(blocks, m_j)]
        l_j = [jnp.sum(e, axis=0, keepdims=True) for e in e_j]
        m = functools.reduce(jnp.maximum, m_j)
        s_j = [jnp.exp2(mj - m) for mj in m_j]
        l = functools.reduce(lambda a, c: a + c, [lj * sj for lj, sj in zip(l_j, s_j)])
        coef = jnp.concatenate([1.0 / l[:, :tq], lam / l[:, tq:]], axis=1)
        parts = []
        for e, sj in zip(e_j, s_j):
            w = e * (coef * sj)
            parts.append((w[:, :tq] - w[:, tq:]).astype(BF16))
        o = _dot_tn(jnp.concatenate(parts, axis=0), vals(b, sl))
        y_ref[b, :, sl] = (_rmsnorm(o, dn_ref[...]) * (1.0 - lam_init)).astype(BF16)


def _mixer_c(q, k, v, layer, dl, dn, lam_init, cache):
    b, l, wc = q.shape
    has_cache = cache is not None
    tq = min(l, ATTN_Q_ROWS)
    bt = 1 if has_cache else max(1, ATTN_SEQ_ROWS // l)
    q_spec = pl.BlockSpec((bt, tq, wc), lambda i, j: (i, j, 0))
    kv_spec = pl.BlockSpec((bt, l, wc), lambda i, j: (i, 0, 0))
    in_specs = [q_spec, kv_spec, kv_spec]
    args = [q, k, v]
    n_keys = l
    if has_cache:
        ck_t, cv = cache
        n_past = cv.shape[2]
        n_keys += n_past
        in_specs += [pl.BlockSpec((None, None, wc, n_past), lambda i, j: (i, layer, 0, 0)),
                     pl.BlockSpec((None, None, n_past, H_C, LANES), lambda i, j: (i, layer, 0, 0, 0))]
        args += [ck_t, cv]
    in_specs += [_layer_spec(dl, layer), _layer_spec(dn, layer)]
    args += [dl, dn]
    scratch = [pltpu.VMEM((n_past, wc), BF16), pltpu.VMEM((n_past, wc), BF16)] if has_cache else []
    return pl.pallas_call(
        functools.partial(_attn_kernel, has_cache=has_cache, lam_init=lam_init),
        grid=(b // bt, l // tq),
        in_specs=in_specs,
        out_specs=q_spec,
        out_shape=jax.ShapeDtypeStruct((b, l, wc), BF16),
        scratch_shapes=scratch,
        compiler_params=_cparams(("parallel", "arbitrary")),
        name="diff_attention",
    )(*args)


def _mlp_kernel(ya_ref, yb_ref, yc_ref, x_ref, mod_ref, g_ref, wo_ref, up_ref, cw_ref, cb_ref, down_ref,
                o_ref, h_scr, g_scr, *, seq_len):
    rows = x_ref.shape[0]
    m = mod_ref[0]
    y = None
    k0 = 0
    for y_ref in (ya_ref, yb_ref, yc_ref):
        kw = y_ref.shape[1]
        part = _dot(y_ref[...], wo_ref[k0:k0 + kw, :])
        y = part if y is None else y + part
        k0 += kw
    x1 = x_ref[...] + m[2:3, :] * y
    o_ref[...] = x1
    h_scr[...] = (_rmsnorm(x1, g_ref[...]) * (1.0 + m[4:5, :]) + m[3:4, :]).astype(BF16)

    d_ff = down_ref.shape[0]
    fc = MXU_DIM
    pos = lax.broadcasted_iota(jnp.int32, (rows, fc), 0) % seq_len
    has_prev = pos > 0
    has_next = pos < seq_len - 1

    def conv(c0):
        u = _dot(h_scr[...], up_ref[:, c0:c0 + fc])
        w = cw_ref[:, c0:c0 + fc]
        prev = jnp.where(has_prev, pltpu.roll(u, 1, 0), 0.0)
        nxt = jnp.where(has_next, pltpu.roll(u, rows - 1, 0), 0.0)
        return prev * w[0:1, :] + u * w[1:2, :] + nxt * w[2:3, :] + cb_ref[:, c0:c0 + fc]

    for j in range(d_ff // fc):
        a = conv(j * fc)
        b = conv(d_ff + j * fc)
        g_scr[:, j * fc:(j + 1) * fc] = (_silu(a) * b).astype(BF16)

    o_ref[...] = o_ref[...] + m[5:6, :] * _dot(g_scr[...], down_ref[...])


def _mlp(ya, yb, yc, x, mods, mod_row, layer, seq_len, g, w_out, up, cw, cb, down):
    t, d = x.shape
    rows = MLP_ROWS
    tiles_per_seq = max(1, seq_len // rows)
    row_spec = lambda a: pl.BlockSpec((rows, a.shape[1]), lambda i: (i, 0))
    weights = [w_out, up, cw, cb, down]
    return pl.pallas_call(
        functools.partial(_mlp_kernel, seq_len=seq_len),
        grid=(t // rows,),
        in_specs=[row_spec(ya), row_spec(yb), row_spec(yc), row_spec(x),
                  _mod_spec(mods, layer, lambda i: mod_row(i // tiles_per_seq)), _layer_spec(g, layer)]
                 + [_layer_spec(w, layer, resident=True) for w in weights],
        out_specs=row_spec(x),
        out_shape=jax.ShapeDtypeStruct((t, d), F32),
        scratch_shapes=[pltpu.VMEM((rows, d), BF16), pltpu.VMEM((rows, down.shape[1]), BF16)],
        compiler_params=_cparams(("parallel",)),
        name="out_proj_mlp",
    )(ya, yb, yc, x, mods, g, *weights)


def _rope_tables(l):
    rows = l // GRID_W
    t_row = jnp.repeat(jnp.arange(rows, dtype=F32), GRID_W)
    t_col = jnp.tile(jnp.arange(GRID_W, dtype=F32), rows)
    inv = ROPE_BASE ** (-jnp.arange(ROPE_PAIRS, dtype=F32) / ROPE_PAIRS)
    ar, ac = t_row[:, None] * inv, t_col[:, None] * inv
    cos = jnp.concatenate([jnp.cos(ar), jnp.cos(ar), jnp.cos(ac), jnp.cos(ac)], axis=-1)
    sin = jnp.concatenate([-jnp.sin(ar), jnp.sin(ar), -jnp.sin(ac), jnp.sin(ac)], axis=-1)
    return jnp.tile(cos, (1, 2)), jnp.tile(sin, (1, 2))


def kernel(x_prompt, x_sample, c, cache_k, cache_v, state_ret_fwd, state_ret_bwd, c_ctx, norm1, w_mod, b_mod,
           w_in, sgu_norm, sgu_w, sgu_b, ret_logit_fwd, ret_logit_bwd, ret_norm, q_norm, k_norm, diff_lam,
           diff_norm, w_out, norm2, ffn_up, ffn_conv, ffn_conv_b, ffn_down):
    depth, d_model, _ = w_mod.shape
    batch, seq, _ = x_prompt.shape
    dec_batch, dec_seq, _ = x_sample.shape
    w_a = sgu_norm.shape[1]
    w_b = H_B * ret_norm.shape[2]
    w_c = 2 * H_C * HD_C
    past = cache_k.shape[2]

    pad = (-(dec_batch + 1)) % 8
    cond = jnp.concatenate([c, c_ctx[None, :], jnp.zeros((pad, d_model), F32)], axis=0)
    mods = _modulation(cond, w_mod, b_mod).reshape(depth, cond.shape[0], 6, d_model)

    w_in_b = w_in.astype(BF16)
    w_out_b = w_out.astype(BF16)
    sgu_w_b = sgu_w.astype(BF16)
    up_b = ffn_up.astype(BF16)
    down_b = ffn_down.astype(BF16)
    row3 = lambda a: a.reshape(depth, 1, -1)
    sgu_bias = jnp.repeat(jnp.swapaxes(sgu_b, 1, 2), w_a // G_A, axis=2)
    logits = jnp.stack([ret_logit_fwd, ret_logit_bwd], axis=1)
    lg_pair = jnp.repeat(logits, DK_B, axis=2).reshape(depth, 2 * (H_B // 2), LANES)
    lg_head = logits.reshape(depth, 2 * H_B, 1)
    qg = row3(jnp.tile(q_norm, (1, 2 * H_C)))
    kg = row3(jnp.tile(k_norm, (1, 2 * H_C)))
    norm1_r, norm2_r, sgu_g, ret_g, diff_g = row3(norm1), row3(norm2), row3(sgu_norm), row3(ret_norm), row3(diff_norm)
    conv_b = row3(ffn_conv_b)
    rope = _rope_tables(dec_seq)
    cache_kt = jnp.transpose(cache_k, (0, 1, 3, 4, 5, 2)).reshape(dec_batch, depth, w_c, past)

    def layer(x, l, nb, sl, mod_row, is_sample, new_cache, new_states):
        lam_init = 0.8 - 0.6 * math.exp(-0.3 * l)
        outs = _projection(x, mods, mod_row, l, sl, norm1_r, w_in_b, sgu_g, sgu_w_b, sgu_bias, qg, kg,
                           rope if is_sample else None, new_cache)
        ya, qr, kr, vr, gr, qa, ka, va = outs[:8]
        seq3 = lambda a: a.reshape(nb, sl, a.shape[1])
        state = (state_ret_fwd, state_ret_bwd) if is_sample else None
        yb, *states = _mixer_b(seq3(qr), seq3(kr), seq3(vr), seq3(gr), l, lg_pair, lg_head, ret_g, state,
                               new_states)
        cache = (cache_kt, cache_v) if is_sample else None
        yc = _mixer_c(seq3(qa), seq3(ka), seq3(va), l, diff_lam, diff_g, lam_init, cache)
        x = _mlp(ya, yb.reshape(nb * sl, w_b), yc.reshape(nb * sl, w_c), x, mods, mod_row, l, sl, norm2_r,
                 w_out_b, up_b, ffn_conv, conv_b, down_b)
        return x, outs[8:], states

    y_prompt = x_prompt.reshape(batch * seq, d_model)
    new_cache, new_states = None, None
    for l in range(depth):
        y_prompt, new_cache, new_states = layer(y_prompt, l, batch, seq, lambda b: dec_batch, False,
                                                new_cache, new_states)

    y_sample = x_sample.reshape(dec_batch * dec_seq, d_model)
    for l in range(depth):
        y_sample, _, _ = layer(y_sample, l, dec_batch, dec_seq, lambda b: b, True, None, None)

    new_kt, new_v = new_cache
    new_k = jnp.transpose(new_kt.reshape(batch, depth, H_C, 2, HD_C, seq), (0, 1, 5, 2, 3, 4))
    return (y_prompt.reshape(batch, seq, d_model), y_sample.reshape(dec_batch, dec_seq, d_model),
            new_k, new_v, new_states[0], new_states[1])
```

```python
import functools
import math

import jax
import jax.numpy as jnp
from jax import lax
from jax.experimental import pallas as pl
from jax.experimental.pallas import tpu as pltpu

F32 = jnp.float32
BF16 = jnp.bfloat16

GRID_W = 64
CHUNK = 128
EPS = 1e-6
ROPE_BASE = 10000.0
G_A = 4
H_B = 4
DK_B = 64
H_C = 4
HD_C = 64
ROPE_PAIRS = HD_C // 4
LANES = 128
MXU_DIM = 256
MOD_COL_TILES = 4
PROJ_ROWS = 512
MLP_ROWS = 1024
ATTN_Q_ROWS = 512
ATTN_KEY_BLOCK = 128
ATTN_SEQ_ROWS = 1024
RET_ROWS = 2048
RET_CHUNK = 256
RET_UNROLL = 4
VMEM_LIMIT = 60 * 1024 * 1024


def _cparams(sem):
    return pltpu.CompilerParams(dimension_semantics=sem, vmem_limit_bytes=VMEM_LIMIT)


def _const_spec(shape):
    n = len(shape)
    return pl.BlockSpec(shape, lambda *_: (0,) * n)


def _layer_spec(arr, layer, resident=False):
    n = arr.ndim - 1
    mode = dict(pipeline_mode=pl.Buffered(1)) if resident else {}
    return pl.BlockSpec((None,) + arr.shape[1:], lambda *_: (layer,) + (0,) * n, **mode)


def _silu(x):
    return x * jax.nn.sigmoid(x)


def _dot(a, b):
    return jnp.dot(a, b, preferred_element_type=F32)


def _dot_nt(a, b):
    return lax.dot_general(a, b, (((1,), (1,)), ((), ())), preferred_element_type=F32)


def _dot_tn(a, b):
    return lax.dot_general(a, b, (((0,), (0,)), ((), ())), preferred_element_type=F32)


def _lo_mask(rows):
    return lax.broadcasted_iota(jnp.int32, (rows, LANES), 1) < (LANES // 2)


def _split_halves(x):
    lo = _lo_mask(x.shape[0])
    zero = jnp.zeros_like(x)
    return jnp.concatenate([jnp.where(lo, x, zero), jnp.where(lo, zero, x)], axis=0)


def _group64_mean(x2):
    rows, n = x2.shape
    lo = _lo_mask(rows)
    outs = []
    for j in range(n // LANES):
        blk = x2[:, j * LANES:(j + 1) * LANES]
        s_lo = jnp.sum(jnp.where(lo, blk, 0.0), axis=-1, keepdims=True)
        s_hi = jnp.sum(jnp.where(lo, 0.0, blk), axis=-1, keepdims=True)
        outs.append(jnp.where(lo, s_lo, s_hi))
    out = outs[0] if len(outs) == 1 else jnp.concatenate(outs, axis=-1)
    return out * (1.0 / (LANES // 2))


def _group64_rmsnorm(x, g):
    return x * lax.rsqrt(_group64_mean(x * x) + EPS) * g


def _rmsnorm(x, g):
    return x * lax.rsqrt(jnp.mean(x * x, axis=-1, keepdims=True) + EPS) * g


def _rope(x, cos, sin):
    rows, n = x.shape
    lane = lax.broadcasted_iota(jnp.int32, (rows, LANES), 1)
    first = (lane % (2 * ROPE_PAIRS)) < ROPE_PAIRS
    outs = []
    for j in range(n // LANES):
        blk = x[:, j * LANES:(j + 1) * LANES]
        partner = jnp.where(first, pltpu.roll(blk, LANES - ROPE_PAIRS, 1), pltpu.roll(blk, ROPE_PAIRS, 1))
        outs.append(blk * cos + partner * sin)
    return jnp.concatenate(outs, axis=-1)


def _mod_kernel(cond_ref, w_ref, b_ref, o_ref):
    s = _silu(cond_ref[...]).astype(BF16)
    o_ref[0] = _dot(s, w_ref[0].astype(BF16)) + b_ref[0]


def _modulation(cond, w_mod, b_mod):
    depth, d, n = w_mod.shape
    rows = cond.shape[0]
    tn = n // MOD_COL_TILES
    return pl.pallas_call(
        _mod_kernel,
        grid=(depth, n // tn),
        in_specs=[
            _const_spec((rows, d)),
            pl.BlockSpec((1, d, tn), lambda l, j: (l, 0, j)),
            pl.BlockSpec((1, 1, tn), lambda l, j: (l, 0, j)),
        ],
        out_specs=pl.BlockSpec((1, rows, tn), lambda l, j: (l, 0, j)),
        out_shape=jax.ShapeDtypeStruct((depth, rows, n), F32),
        compiler_params=_cparams(("parallel", "parallel")),
        name="adaln_mod",
    )(cond, w_mod, b_mod.reshape(depth, 1, n))


def _mod_spec(mods, layer, row_of_step):
    return pl.BlockSpec((None, 1) + mods.shape[2:], lambda i: (layer, row_of_step(i), 0, 0))


def _proj_kernel(*refs, has_rope, seq_len, n_prev):
    if has_rope:
        (x_ref, mod_ref, g_ref, w_ref, sg_ref, ws_ref, sb_ref, qg_ref, kg_ref, cos_ref, sin_ref,
         ya_ref, qr_ref, kr_ref, vr_ref, gr_ref, qa_ref, ka_ref, va_ref) = refs
    elif n_prev:
        (x_ref, mod_ref, g_ref, w_ref, sg_ref, ws_ref, sb_ref, qg_ref, kg_ref, kt_prev_ref, vf_prev_ref,
         ya_ref, qr_ref, kr_ref, vr_ref, gr_ref, qa_ref, ka_ref, va_ref, kt_ref, vf_ref) = refs
        kt_ref[:, :n_prev] = kt_prev_ref[...]
        vf_ref[:, :n_prev] = vf_prev_ref[...]
    else:
        (x_ref, mod_ref, g_ref, w_ref, sg_ref, ws_ref, sb_ref, qg_ref, kg_ref,
         ya_ref, qr_ref, kr_ref, vr_ref, gr_ref, qa_ref, ka_ref, va_ref, kt_ref, vf_ref) = refs
    rows = x_ref.shape[0]
    m = mod_ref[0]
    hb = (_rmsnorm(x_ref[...], g_ref[...]) * (1.0 + m[1:2, :]) + m[0:1, :]).astype(BF16)
    wa = ya_ref.shape[1]
    wb = qr_ref.shape[1]
    wc = qa_ref.shape[1]

    def proj(c0, width):
        return _dot(hb, w_ref[:, c0:c0 + width])

    def emit_q(z):
        qn = _group64_rmsnorm(z, qg_ref[...])
        if has_rope:
            qn = _rope(qn, cos_ref[...], sin_ref[...])
        qa_ref[...] = (qn * (HD_C ** -0.5 * math.log2(math.e))).astype(BF16)

    def emit_k(z):
        kn = _group64_rmsnorm(z, kg_ref[...])
        if has_rope:
            kn = _rope(kn, cos_ref[...], sin_ref[...])
        else:
            for s in range(rows // seq_len):
                kt_ref[s, n_prev] = kn[s * seq_len:(s + 1) * seq_len, :].T
        ka_ref[...] = kn.astype(BF16)

    def emit_v(z):
        if not has_rope:
            for s in range(rows // seq_len):
                zs = z[s * seq_len:(s + 1) * seq_len, :]
                heads = jnp.stack([zs[:, h * LANES:(h + 1) * LANES] for h in range(H_C)], axis=0)
                vf_ref[s, n_prev] = jnp.swapaxes(heads, 0, 1)
        va_ref[...] = z.astype(BF16)

    def sgu_prepare(z):
        za = jax.nn.gelu(z)
        return za[:, :wa], _rmsnorm(za[:, wa:], sg_ref[...]).astype(BF16)

    def sgu_mix(u, v):
        lane = lax.broadcasted_iota(jnp.int32, (CHUNK, wa), 1)
        dg = wa // G_A
        for c in range(rows // CHUNK):
            rs = slice(c * CHUNK, (c + 1) * CHUNK)
            s = _dot(ws_ref[G_A - 1], v[rs])
            for g in range(G_A - 2, -1, -1):
                s = jnp.where(lane < (g + 1) * dg, _dot(ws_ref[g], v[rs]), s)
            ya_ref[rs, :] = (u[rs] * (s + sb_ref[...])).astype(BF16)

    def emit_ret_qk(z):
        qr_ref[...] = z[:, :wb].astype(BF16)
        kr_ref[...] = z[:, wb:] * (DK_B ** -0.5)

    def emit_ret_vg(z):
        vr_ref[...] = z[:, :wb].astype(BF16)
        gr_ref[...] = _silu(z[:, wb:])

    cb = 2 * wa
    cc = cb + 4 * wb
    z_a = proj(0, 2 * wa)
    z_q = proj(cc, wc)
    u, v = sgu_prepare(z_a)
    z_k = proj(cc + wc, wc)
    sgu_mix(u, v)
    emit_q(z_q)
    z_v = proj(cc + 2 * wc, wc)
    emit_k(z_k)
    z_r = proj(cb, 2 * wb)
    emit_v(z_v)
    z_g = proj(cb + 2 * wb, 2 * wb)
    emit_ret_qk(z_r)
    emit_ret_vg(z_g)


def _stacked_spec(lead, layers, tail):
    return pl.BlockSpec((lead, layers) + tail, lambda i: (i, 0) + (0,) * len(tail))


def _projection(x, mods, mod_row, layer, seq_len, g, w_bf16, sgu_g, sgu_w, sgu_bias, qg, kg, rope, prev_cache):
    t, d = x.shape
    rows = 2 * PROJ_ROWS if rope is not None else PROJ_ROWS
    wa = sgu_g.shape[2]
    wb = wa
    wc = qg.shape[2]
    has_rope = rope is not None
    tiles_per_seq = max(1, seq_len // rows)
    row_spec = lambda w: pl.BlockSpec((rows, w), lambda i: (i, 0))
    params = [g, w_bf16, sgu_g, sgu_w, sgu_bias, qg, kg]
    in_specs = [row_spec(d), _mod_spec(mods, layer, lambda i: mod_row(i // tiles_per_seq))]
    in_specs += [_layer_spec(p, layer) for p in params]
    args = [x, mods] + params
    outs = [(wa, BF16), (wb, BF16), (wb, F32), (wb, BF16), (wb, F32), (wc, BF16), (wc, BF16), (wc, BF16)]
    out_specs = [row_spec(w) for w, _ in outs]
    out_shape = [jax.ShapeDtypeStruct((t, w), dt) for w, dt in outs]
    if has_rope:
        tab_spec = pl.BlockSpec((rows, LANES), lambda i: (i % tiles_per_seq, 0))
        in_specs += [tab_spec, tab_spec]
        args += list(rope)
    else:
        seqs = rows // seq_len
        k_tail, v_tail = (wc, seq_len), (seq_len, H_C, LANES)
        if layer:
            in_specs += [_stacked_spec(seqs, layer, k_tail), _stacked_spec(seqs, layer, v_tail)]
            args += list(prev_cache)
        out_specs += [_stacked_spec(seqs, layer + 1, k_tail), _stacked_spec(seqs, layer + 1, v_tail)]
        out_shape += [jax.ShapeDtypeStruct((t // seq_len, layer + 1) + k_tail, F32),
                      jax.ShapeDtypeStruct((t // seq_len, layer + 1) + v_tail, F32)]
    return pl.pallas_call(
        functools.partial(_proj_kernel, has_rope=has_rope, seq_len=seq_len, n_prev=0 if has_rope else layer),
        grid=(t // rows,),
        in_specs=in_specs,
        out_specs=out_specs,
        out_shape=out_shape,
        compiler_params=_cparams(("parallel",)),
        name="projection",
    )(*args)


def _log_sigmoid(x):
    y = -x
    return -(jnp.maximum(y, 0.0) + jnp.log1p(jnp.exp(-jnp.abs(y))))


def _ret_kernel(*refs, chunk, n_chunks, has_state, n_prev):
    if has_state:
        (q_ref, k_ref, v_ref, gate_ref, r0f_ref, r0b_ref, lgp_ref, lgh_ref, norm_ref,
         y_ref, d_scr, dec_scr, kv_scr, rs_scr, o_scr) = refs
        state_outs = None
    elif n_prev:
        (q_ref, k_ref, v_ref, gate_ref, rf_prev_ref, rb_prev_ref, lgp_ref, lgh_ref, norm_ref,
         y_ref, rf_ref, rb_ref, d_scr, dec_scr, kv_scr, rs_scr, o_scr) = refs
        rf_ref[:, :n_prev] = rf_prev_ref[...]
        rb_ref[:, :n_prev] = rb_prev_ref[...]
        state_outs = (rf_ref, rb_ref)
    else:
        (q_ref, k_ref, v_ref, gate_ref, lgp_ref, lgh_ref, norm_ref,
         y_ref, rf_ref, rb_ref, d_scr, dec_scr, kv_scr, rs_scr, o_scr) = refs
        state_outs = (rf_ref, rb_ref)
    bt = q_ref.shape[0]
    n_pairs = H_B // 2
    half = LANES // 2
    cs = chunk

    @pl.when(pl.program_id(0) == 0)
    def _build_decay_tables():
        lgp = _log_sigmoid(lgp_ref[...])
        lgh = _log_sigmoid(lgh_ref[...])
        dist = (lax.broadcasted_iota(jnp.int32, (cs, cs), 0) - lax.broadcasted_iota(jnp.int32, (cs, cs), 1)).astype(F32)
        for h in range(H_B):
            lf = lgh[h:h + 1, :]
            lb = lgh[H_B + h:H_B + h + 1, :]
            d_f = jnp.where(dist >= 0, jnp.exp(lf * jnp.maximum(dist, 0.0)), 0.0)
            d_b = jnp.where(dist <= 0, jnp.exp(lb * jnp.maximum(-dist, 0.0)), 0.0)
            d_scr[h // 2, (h % 2) * cs:(h % 2 + 1) * cs, :] = d_f + d_b
        pos = lax.broadcasted_iota(jnp.int32, (cs, LANES), 0).astype(F32)
        for p in range(n_pairs):
            lf = lgp[p:p + 1, :]
            lb = lgp[n_pairs + p:n_pairs + p + 1, :]
            dec_scr[0, p] = jnp.exp(lf * (pos + 1.0))
            dec_scr[1, p] = jnp.exp(lf * (cs - 1.0 - pos))
            dec_scr[2, p] = jnp.exp(lb * (cs - pos))
            dec_scr[3, p] = jnp.exp(lb * pos)
            dec_scr[4, p] = jnp.broadcast_to(jnp.exp(lf * float(cs)), (cs, LANES))
            dec_scr[5, p] = jnp.broadcast_to(jnp.exp(lb * float(cs)), (cs, LANES))

    lo = _lo_mask(cs)
    blockdiag = ((lax.broadcasted_iota(jnp.int32, (LANES, LANES), 0) < half)
                 == (lax.broadcasted_iota(jnp.int32, (LANES, LANES), 1) < half))

    for b in range(bt):
        def intra(c, carry):
            rows = pl.ds(pl.multiple_of(c * cs, cs), cs)
            qc, kc, vc = q_ref[b, rows, :], k_ref[b, rows, :], v_ref[b, rows, :]
            for p in range(n_pairs):
                sl = slice(p * LANES, (p + 1) * LANES)
                k128, vb = kc[:, sl], vc[:, sl]
                inner = (_dot_nt(_split_halves(qc[:, sl]), k128.astype(BF16)) * d_scr[p]).astype(BF16)
                oo = _dot(inner, vb)
                o_scr[rows, sl] = jnp.where(lo, oo[:cs], oo[cs:])
                kk = jnp.concatenate([k128 * dec_scr[1, p], k128 * dec_scr[3, p]], axis=1).astype(BF16)
                kv = _dot_tn(kk, vb)
                kv_scr[0, c, p] = jnp.where(blockdiag, kv[:LANES], 0.0)
                kv_scr[1, c, p] = jnp.where(blockdiag, kv[LANES:], 0.0)
            return carry

        lax.fori_loop(0, n_chunks, intra, 0, unroll=min(n_chunks, RET_UNROLL))

        for p in range(n_pairs):
            for d in range(2):
                if has_state:
                    r0_ref = (r0f_ref, r0b_ref)[d]
                    zero = jnp.zeros((half, half), F32)
                    top = jnp.concatenate([r0_ref[b, 2 * p], zero], axis=1)
                    bot = jnp.concatenate([zero, r0_ref[b, 2 * p + 1]], axis=1)
                    r = jnp.concatenate([top, bot], axis=0)
                else:
                    r = jnp.zeros((LANES, LANES), F32)
                order = range(n_chunks) if d == 0 else range(n_chunks - 1, -1, -1)
                for c in order:
                    rs_scr[c, p, :, d * LANES:(d + 1) * LANES] = r.astype(BF16)
                    r = r * dec_scr[4 + d, p, :LANES, :] + kv_scr[d, c, p]
                if state_outs is not None:
                    state_outs[d][b, n_prev, 2 * p] = r[:half, :half]
                    state_outs[d][b, n_prev, 2 * p + 1] = r[half:, half:]

        def cross(c, carry):
            rows = pl.ds(pl.multiple_of(c * cs, cs), cs)
            qc = q_ref[b, rows, :]
            outs = []
            for p in range(n_pairs):
                sl = slice(p * LANES, (p + 1) * LANES)
                t = _dot(qc[:, sl], rs_scr[c, p])
                outs.append(o_scr[rows, sl] + t[:, :LANES] * dec_scr[0, p] + t[:, LANES:] * dec_scr[2, p])
            o = _group64_rmsnorm(jnp.concatenate(outs, axis=-1), norm_ref[...])
            y_ref[b, rows, :] = (gate_ref[b, rows, :] * o).astype(BF16)
            return carry

        lax.fori_loop(0, n_chunks, cross, 0, unroll=min(n_chunks, RET_UNROLL))


def _mixer_b(q, k, v, gate, layer, lg_pair, lg_head, norm, state, prev_states):
    b, l, wb = q.shape
    bt = max(1, RET_ROWS // l)
    chunk = min(l, RET_CHUNK)
    n_chunks = l // chunk
    has_state = state is not None
    tile = pl.BlockSpec((bt, l, wb), lambda i: (i, 0, 0))
    in_specs = [tile, tile, tile, tile]
    args = [q, k, v, gate]
    out_specs = [tile]
    out_shape = [jax.ShapeDtypeStruct((b, l, wb), BF16)]
    s_tail = (H_B, DK_B, DK_B)
    if has_state:
        sspec = pl.BlockSpec((bt, None) + s_tail, lambda i: (i, layer, 0, 0, 0))
        in_specs += [sspec, sspec]
        args += list(state)
    else:
        if layer:
            in_specs += [_stacked_spec(bt, layer, s_tail)] * 2
            args += list(prev_states)
        out_specs += [_stacked_spec(bt, layer + 1, s_tail)] * 2
        out_shape += [jax.ShapeDtypeStruct((b, layer + 1) + s_tail, F32)] * 2
    params = [lg_pair, lg_head, norm]
    in_specs += [_layer_spec(p, layer) for p in params]
    args += params
    n_pairs = H_B // 2
    return pl.pallas_call(
        functools.partial(_ret_kernel, chunk=chunk, n_chunks=n_chunks, has_state=has_state,
                          n_prev=0 if has_state else layer),
        grid=(b // bt,),
        in_specs=in_specs,
        out_specs=out_specs,
        out_shape=out_shape,
        scratch_shapes=[
            pltpu.VMEM((n_pairs, 2 * chunk, chunk), F32),
            pltpu.VMEM((6, n_pairs, chunk, LANES), F32),
            pltpu.VMEM((2, n_chunks, n_pairs, LANES, LANES), F32),
            pltpu.VMEM((n_chunks, n_pairs, LANES, 2 * LANES), BF16),
            pltpu.VMEM((l, wb), F32),
        ],
        compiler_params=_cparams(("arbitrary",)),
        name="retention",
    )(*args)


def _attn_kernel(*refs, has_cache, lam_init):
    if has_cache:
        (q_ref, k_ref, v_ref, ck_ref, cv_ref, dl_ref, dn_ref, y_ref, k_scr, v_scr) = refs
        n_past = ck_ref.shape[1]

        @pl.when(pl.program_id(1) == 0)
        def _gather_keys():
            k_scr[...] = ck_ref[...].T.astype(BF16)
            cv = jnp.swapaxes(cv_ref[...], 0, 1)
            for h in range(H_C):
                v_scr[:, h * LANES:(h + 1) * LANES] = cv[h].astype(BF16)

        keys = lambda b, sl: jnp.concatenate([k_scr[:, sl], k_ref[b, :, sl]], axis=0)
        vals = lambda b, sl: jnp.concatenate([v_scr[:, sl], v_ref[b, :, sl]], axis=0)
    else:
        (q_ref, k_ref, v_ref, dl_ref, dn_ref, y_ref) = refs
        keys = lambda b, sl: k_ref[b, :, sl]
        vals = lambda b, sl: v_ref[b, :, sl]
    bt, tq, _ = q_ref.shape
    dl = dl_ref[...]
    lam = (jnp.exp(jnp.sum(dl[0:1] * dl[1:2], axis=-1, keepdims=True))
           - jnp.exp(jnp.sum(dl[2:3] * dl[3:4], axis=-1, keepdims=True)) + lam_init)
    units = [(b, slice(h * LANES, (h + 1) * LANES)) for b in range(bt) for h in range(H_C)]
    scores = lambda b, sl: _dot_nt(keys(b, sl), _split_halves(q_ref[b, :, sl]))
    st_next = scores(*units[0])
    for i, (b, sl) in enumerate(units):
        st = st_next
        if i + 1 < len(units):
            st_next = scores(*units[i + 1])
        kb = ATTN_KEY_BLOCK
        blocks = [st[j:j + kb] for j in range(0, st.shape[0], kb)]
        m_j = [jnp.max(blk, axis=0, keepdims=True) for blk in blocks]
        e_j = [jnp.exp2(blk - mj) for blk, mj in zip(blocks, m_j)]
        l_j = [jnp.sum(e, axis=0, keepdims=True) for e in e_j]
        m = functools.reduce(jnp.maximum, m_j)
        s_j = [jnp.exp2(mj - m) for mj in m_j]
        l = functools.reduce(lambda a, c: a + c, [lj * sj for lj, sj in zip(l_j, s_j)])
        coef = jnp.concatenate([1.0 / l[:, :tq], lam / l[:, tq:]], axis=1)
        parts = []
        for e, sj in zip(e_j, s_j):
            w = e * (coef * sj)
            parts.append((w[:, :tq] - w[:, tq:]).astype(BF16))
        o = _dot_tn(jnp.concatenate(parts, axis=0), vals(b, sl))
        y_ref[b, :, sl] = (_rmsnorm(o, dn_ref[...]) * (1.0 - lam_init)).astype(BF16)


def _mixer_c(q, k, v, layer, dl, dn, lam_init, cache):
    b, l, wc = q.shape
    has_cache = cache is not None
    tq = min(l, ATTN_Q_ROWS)
    bt = 1 if has_cache else max(1, ATTN_SEQ_ROWS // l)
    q_spec = pl.BlockSpec((bt, tq, wc), lambda i, j: (i, j, 0))
    kv_spec = pl.BlockSpec((bt, l, wc), lambda i, j: (i, 0, 0))
    in_specs = [q_spec, kv_spec, kv_spec]
    args = [q, k, v]
    n_keys = l
    if has_cache:
        ck_t, cv = cache
        n_past = cv.shape[2]
        n_keys += n_past
        in_specs += [pl.BlockSpec((None, None, wc, n_past), lambda i, j: (i, layer, 0, 0)),
                     pl.BlockSpec((None, None, n_past, H_C, LANES), lambda i, j: (i, layer, 0, 0, 0))]
        args += [ck_t, cv]
    in_specs += [_layer_spec(dl, layer), _layer_spec(dn, layer)]
    args += [dl, dn]
    scratch = [pltpu.VMEM((n_past, wc), BF16), pltpu.VMEM((n_past, wc), BF16)] if has_cache else []
    return pl.pallas_call(
        functools.partial(_attn_kernel, has_cache=has_cache, lam_init=lam_init),
        grid=(b // bt, l // tq),
        in_specs=in_specs,
        out_specs=q_spec,
        out_shape=jax.ShapeDtypeStruct((b, l, wc), BF16),
        scratch_shapes=scratch,
        compiler_params=_cparams(("parallel", "arbitrary")),
        name="diff_attention",
    )(*args)


def _mlp_kernel(ya_ref, yb_ref, yc_ref, x_ref, mod_ref, g_ref, wo_ref, up_ref, cw_ref, cb_ref, down_ref,
                o_ref, h_scr, g_scr, *, seq_len):
    rows = x_ref.shape[0]
    m = mod_ref[0]
    y = None
    k0 = 0
    for y_ref in (ya_ref, yb_ref, yc_ref):
        kw = y_ref.shape[1]
        part = _dot(y_ref[...], wo_ref[k0:k0 + kw, :])
        y = part if y is None else y + part
        k0 += kw
    x1 = x_ref[...] + m[2:3, :] * y
    o_ref[...] = x1
    h_scr[...] = (_rmsnorm(x1, g_ref[...]) * (1.0 + m[4:5, :]) + m[3:4, :]).astype(BF16)

    d_ff = down_ref.shape[0]
    fc = MXU_DIM
    pos = lax.broadcasted_iota(jnp.int32, (rows, fc), 0) % seq_len
    has_prev = pos > 0
    has_next = pos < seq_len - 1

    def conv(c0):
        u = _dot(h_scr[...], up_ref[:, c0:c0 + fc])
        w = cw_ref[:, c0:c0 + fc]
        prev = jnp.where(has_prev, pltpu.roll(u, 1, 0), 0.0)
        nxt = jnp.where(has_next, pltpu.roll(u, rows - 1, 0), 0.0)
        return prev * w[0:1, :] + u * w[1:2, :] + nxt * w[2:3, :] + cb_ref[:, c0:c0 + fc]

    for j in range(d_ff // fc):
        a = conv(j * fc)
        b = conv(d_ff + j * fc)
        g_scr[:, j * fc:(j + 1) * fc] = (_silu(a) * b).astype(BF16)

    o_ref[...] = o_ref[...] + m[5:6, :] * _dot(g_scr[...], down_ref[...])


def _mlp(ya, yb, yc, x, mods, mod_row, layer, seq_len, g, w_out, up, cw, cb, down):
    t, d = x.shape
    rows = MLP_ROWS
    tiles_per_seq = max(1, seq_len // rows)
    row_spec = lambda a: pl.BlockSpec((rows, a.shape[1]), lambda i: (i, 0))
    weights = [w_out, up, cw, cb, down]
    return pl.pallas_call(
        functools.partial(_mlp_kernel, seq_len=seq_len),
        grid=(t // rows,),
        in_specs=[row_spec(ya), row_spec(yb), row_spec(yc), row_spec(x),
                  _mod_spec(mods, layer, lambda i: mod_row(i // tiles_per_seq)), _layer_spec(g, layer)]
                 + [_layer_spec(w, layer, resident=True) for w in weights],
        out_specs=row_spec(x),
        out_shape=jax.ShapeDtypeStruct((t, d), F32),
        scratch_shapes=[pltpu.VMEM((rows, d), BF16), pltpu.VMEM((rows, down.shape[1]), BF16)],
        compiler_params=_cparams(("parallel",)),
        name="out_proj_mlp",
    )(ya, yb, yc, x, mods, g, *weights)


def _rope_tables(l):
    rows = l // GRID_W
    t_row = jnp.repeat(jnp.arange(rows, dtype=F32), GRID_W)
    t_col = jnp.tile(jnp.arange(GRID_W, dtype=F32), rows)
    inv = ROPE_BASE ** (-jnp.arange(ROPE_PAIRS, dtype=F32) / ROPE_PAIRS)
    ar, ac = t_row[:, None] * inv, t_col[:, None] * inv
    cos = jnp.concatenate([jnp.cos(ar), jnp.cos(ar), jnp.cos(ac), jnp.cos(ac)], axis=-1)
    sin = jnp.concatenate([-jnp.sin(ar), jnp.sin(ar), -jnp.sin(ac), jnp.sin(ac)], axis=-1)
    return jnp.tile(cos, (1, 2)), jnp.tile(sin, (1, 2))


def kernel(x_prompt, x_sample, c, cache_k, cache_v, state_ret_fwd, state_ret_bwd, c_ctx, norm1, w_mod, b_mod,
           w_in, sgu_norm, sgu_w, sgu_b, ret_logit_fwd, ret_logit_bwd, ret_norm, q_norm, k_norm, diff_lam,
           diff_norm, w_out, norm2, ffn_up, ffn_conv, ffn_conv_b, ffn_down):
    depth, d_model, _ = w_mod.shape
    batch, seq, _ = x_prompt.shape
    dec_batch, dec_seq, _ = x_sample.shape
    w_a = sgu_norm.shape[1]
    w_b = H_B * ret_norm.shape[2]
    w_c = 2 * H_C * HD_C
    past = cache_k.shape[2]

    pad = (-(dec_batch + 1)) % 8
    cond = jnp.concatenate([c, c_ctx[None, :], jnp.zeros((pad, d_model), F32)], axis=0)
    mods = _modulation(cond, w_mod, b_mod).reshape(depth, cond.shape[0], 6, d_model)

    w_in_b = w_in.astype(BF16)
    w_out_b = w_out.astype(BF16)
    sgu_w_b = sgu_w.astype(BF16)
    up_b = ffn_up.astype(BF16)
    down_b = ffn_down.astype(BF16)
    row3 = lambda a: a.reshape(depth, 1, -1)
    sgu_bias = jnp.repeat(jnp.swapaxes(sgu_b, 1, 2), w_a // G_A, axis=2)
    logits = jnp.stack([ret_logit_fwd, ret_logit_bwd], axis=1)
    lg_pair = jnp.repeat(logits, DK_B, axis=2).reshape(depth, 2 * (H_B // 2), LANES)
    lg_head = logits.reshape(depth, 2 * H_B, 1)
    qg = row3(jnp.tile(q_norm, (1, 2 * H_C)))
    kg = row3(jnp.tile(k_norm, (1, 2 * H_C)))
    norm1_r, norm2_r, sgu_g, ret_g, diff_g = row3(norm1), row3(norm2), row3(sgu_norm), row3(ret_norm), row3(diff_norm)
    conv_b = row3(ffn_conv_b)
    rope = _rope_tables(dec_seq)
    cache_kt = jnp.transpose(cache_k, (0, 1, 3, 4, 5, 2)).reshape(dec_batch, depth, w_c, past)

    def layer(x, l, nb, sl, mod_row, is_sample, new_cache, new_states):
        lam_init = 0.8 - 0.6 * math.exp(-0.3 * l)
        outs = _projection(x, mods, mod_row, l, sl, norm1_r, w_in_b, sgu_g, sgu_w_b, sgu_bias, qg, kg,
                           rope if is_sample else None, new_cache)
        ya, qr, kr, vr, gr, qa, ka, va = outs[:8]
        seq3 = lambda a: a.reshape(nb, sl, a.shape[1])
        state = (state_ret_fwd, state_ret_bwd) if is_sample else None
        yb, *states = _mixer_b(seq3(qr), seq3(kr), seq3(vr), seq3(gr), l, lg_pair, lg_head, ret_g, state,
                               new_states)
        cache = (cache_kt, cache_v) if is_sample else None
        yc = _mixer_c(seq3(qa), seq3(ka), seq3(va), l, diff_lam, diff_g, lam_init, cache)
        x = _mlp(ya, yb.reshape(nb * sl, w_b), yc.reshape(nb * sl, w_c), x, mods, mod_row, l, sl, norm2_r,
                 w_out_b, up_b, ffn_conv, conv_b, down_b)
        return x, outs[8:], states

    y_prompt = x_prompt.reshape(batch * seq, d_model)
    new_cache, new_states = None, None
    for l in range(depth):
        y_prompt, new_cache, new_states = layer(y_prompt, l, batch, seq, lambda b: dec_batch, False,
                                                new_cache, new_states)

    y_sample = x_sample.reshape(dec_batch * dec_seq, d_model)
    for l in range(depth):
        y_sample, _, _ = layer(y_sample, l, dec_batch, dec_seq, lambda b: b, True, None, None)

    new_kt, new_v = new_cache
    new_k = jnp.transpose(new_kt.reshape(batch, depth, H_C, 2, HD_C, seq), (0, 1, 5, 2, 3, 4))
    return (y_prompt.reshape(batch, seq, d_model), y_sample.reshape(dec_batch, dec_seq, d_model),
            new_k, new_v, new_states[0], new_states[1])
```

```python
import functools
import math

import jax
import jax.numpy as jnp
from jax import lax
from jax.experimental import pallas as pl
from jax.experimental.pallas import tpu as pltpu

F32 = jnp.float32
BF16 = jnp.bfloat16

GRID_W = 64
CHUNK = 128
EPS = 1e-6
ROPE_BASE = 10000.0
G_A = 4
H_B = 4
DK_B = 64
H_C = 4
HD_C = 64
ROPE_PAIRS = HD_C // 4
LANES = 128
SUBLANES = 8
MXU_DIM = 256
MOD_COL_TILES = 4
PROJ_ROWS = 512
MLP_ROWS = 1024
ATTN_Q_ROWS = 512
ATTN_KEY_BLOCK = 128
ATTN_SEQ_ROWS = 1024
RET_ROWS = 2048
RET_CHUNK = 256
RET_UNROLL = 4
VMEM_LIMIT = 60 * 1024 * 1024


def _cparams(sem):
    return pltpu.CompilerParams(dimension_semantics=sem, vmem_limit_bytes=VMEM_LIMIT)


def _const_spec(shape):
    n = len(shape)
    return pl.BlockSpec(shape, lambda *_: (0,) * n)


def _layer_spec(arr, layer, resident=False):
    n = arr.ndim - 1
    mode = dict(pipeline_mode=pl.Buffered(1)) if resident else {}
    return pl.BlockSpec((None,) + arr.shape[1:], lambda *_: (layer,) + (0,) * n, **mode)


def _silu(x):
    return x * jax.nn.sigmoid(x)


def _dot(a, b):
    return jnp.dot(a, b, preferred_element_type=F32)


def _dot_nt(a, b):
    return lax.dot_general(a, b, (((1,), (1,)), ((), ())), preferred_element_type=F32)


def _dot_tn(a, b):
    return lax.dot_general(a, b, (((0,), (0,)), ((), ())), preferred_element_type=F32)


def _lo_mask(rows):
    return lax.broadcasted_iota(jnp.int32, (rows, LANES), 1) < (LANES // 2)


def _split_halves(x):
    lo = _lo_mask(x.shape[0])
    zero = jnp.zeros_like(x)
    return jnp.concatenate([jnp.where(lo, x, zero), jnp.where(lo, zero, x)], axis=0)


def _group64_mean(x2):
    rows, n = x2.shape
    lo = _lo_mask(rows)
    outs = []
    for j in range(n // LANES):
        blk = x2[:, j * LANES:(j + 1) * LANES]
        s_lo = jnp.sum(jnp.where(lo, blk, 0.0), axis=-1, keepdims=True)
        s_hi = jnp.sum(jnp.where(lo, 0.0, blk), axis=-1, keepdims=True)
        outs.append(jnp.where(lo, s_lo, s_hi))
    out = outs[0] if len(outs) == 1 else jnp.concatenate(outs, axis=-1)
    return out * (1.0 / (LANES // 2))


def _group64_rmsnorm(x, g):
    return x * lax.rsqrt(_group64_mean(x * x) + EPS) * g


def _rmsnorm(x, g):
    return x * lax.rsqrt(jnp.mean(x * x, axis=-1, keepdims=True) + EPS) * g


def _rope(x, cos, sin):
    rows, n = x.shape
    lane = lax.broadcasted_iota(jnp.int32, (rows, LANES), 1)
    first = (lane % (2 * ROPE_PAIRS)) < ROPE_PAIRS
    outs = []
    for j in range(n // LANES):
        blk = x[:, j * LANES:(j + 1) * LANES]
        partner = jnp.where(first, pltpu.roll(blk, LANES - ROPE_PAIRS, 1), pltpu.roll(blk, ROPE_PAIRS, 1))
        outs.append(blk * cos + partner * sin)
    return jnp.concatenate(outs, axis=-1)


def _mod_kernel(cond_ref, w_ref, b_ref, o_ref):
    s = _silu(cond_ref[...]).astype(BF16)
    o_ref[0] = _dot(s, w_ref[0].astype(BF16)) + b_ref[0]


def _modulation(cond, w_mod, b_mod):
    depth, d, n = w_mod.shape
    rows = cond.shape[0]
    tn = n // MOD_COL_TILES
    return pl.pallas_call(
        _mod_kernel,
        grid=(depth, n // tn),
        in_specs=[
            _const_spec((rows, d)),
            pl.BlockSpec((1, d, tn), lambda l, j: (l, 0, j)),
            pl.BlockSpec((1, 1, tn), lambda l, j: (l, 0, j)),
        ],
        out_specs=pl.BlockSpec((1, rows, tn), lambda l, j: (l, 0, j)),
        out_shape=jax.ShapeDtypeStruct((depth, rows, n), F32),
        compiler_params=_cparams(("parallel", "parallel")),
        name="adaln_mod",
    )(cond, w_mod, b_mod.reshape(depth, 1, n))


def _mod_spec(mods, layer, row_of_step):
    return pl.BlockSpec((None, 1) + mods.shape[2:], lambda i: (layer, row_of_step(i), 0, 0))


def _proj_kernel(*refs, has_rope, seq_len, n_prev):
    if has_rope:
        (x_ref, mod_ref, g_ref, w_ref, sg_ref, ws_ref, sb_ref, qg_ref, kg_ref, cos_ref, sin_ref,
         ya_ref, qr_ref, kr_ref, vr_ref, gr_ref, qa_ref, ka_ref, va_ref) = refs
    elif n_prev:
        (x_ref, mod_ref, g_ref, w_ref, sg_ref, ws_ref, sb_ref, qg_ref, kg_ref, kt_prev_ref, vf_prev_ref,
         ya_ref, qr_ref, kr_ref, vr_ref, gr_ref, qa_ref, ka_ref, va_ref, kt_ref, vf_ref) = refs
        kt_ref[:, :n_prev] = kt_prev_ref[...]
        vf_ref[:, :n_prev] = vf_prev_ref[...]
    else:
        (x_ref, mod_ref, g_ref, w_ref, sg_ref, ws_ref, sb_ref, qg_ref, kg_ref,
         ya_ref, qr_ref, kr_ref, vr_ref, gr_ref, qa_ref, ka_ref, va_ref, kt_ref, vf_ref) = refs
    rows = x_ref.shape[0]
    m = mod_ref[0]
    hb = (_rmsnorm(x_ref[...], g_ref[...]) * (1.0 + m[1:2, :]) + m[0:1, :]).astype(BF16)
    wa = ya_ref.shape[1]
    wb = qr_ref.shape[1]
    wc = qa_ref.shape[1]

    def proj(c0, width):
        return _dot(hb, w_ref[:, c0:c0 + width])

    def emit_q(z):
        qn = _group64_rmsnorm(z, qg_ref[...])
        if has_rope:
            qn = _rope(qn, cos_ref[...], sin_ref[...])
        qa_ref[...] = (qn * (HD_C ** -0.5 * math.log2(math.e))).astype(BF16)

    def emit_k(z):
        kn = _group64_rmsnorm(z, kg_ref[...])
        if has_rope:
            kn = _rope(kn, cos_ref[...], sin_ref[...])
        else:
            for s in range(rows // seq_len):
                kt_ref[s, n_prev] = kn[s * seq_len:(s + 1) * seq_len, :].T
        ka_ref[...] = kn.astype(BF16)

    def emit_v(z):
        if not has_rope:
            for s in range(rows // seq_len):
                zs = z[s * seq_len:(s + 1) * seq_len, :]
                heads = jnp.stack([zs[:, h * LANES:(h + 1) * LANES] for h in range(H_C)], axis=0)
                vf_ref[s, n_prev] = jnp.swapaxes(heads, 0, 1)
        va_ref[...] = z.astype(BF16)

    def sgu_prepare(z):
        za = jax.nn.gelu(z)
        return za[:, :wa], _rmsnorm(za[:, wa:], sg_ref[...]).astype(BF16)

    def sgu_mix(u, v):
        lane = lax.broadcasted_iota(jnp.int32, (CHUNK, wa), 1)
        dg = wa // G_A
        for c in range(rows // CHUNK):
            rs = slice(c * CHUNK, (c + 1) * CHUNK)
            s = _dot(ws_ref[G_A - 1], v[rs])
            for g in range(G_A - 2, -1, -1):
                s = jnp.where(lane < (g + 1) * dg, _dot(ws_ref[g], v[rs]), s)
            ya_ref[rs, :] = (u[rs] * (s + sb_ref[...])).astype(BF16)

    def emit_ret_qk(z):
        qr_ref[...] = z[:, :wb].astype(BF16)
        kr_ref[...] = z[:, wb:] * (DK_B ** -0.5)

    def emit_ret_vg(z):
        vr_ref[...] = z[:, :wb].astype(BF16)
        gr_ref[...] = _silu(z[:, wb:])

    cb = 2 * wa
    cc = cb + 4 * wb
    z_a = proj(0, 2 * wa)
    z_q = proj(cc, wc)
    u, v = sgu_prepare(z_a)
    z_k = proj(cc + wc, wc)
    sgu_mix(u, v)
    emit_q(z_q)
    z_v = proj(cc + 2 * wc, wc)
    emit_k(z_k)
    z_r = proj(cb, 2 * wb)
    emit_v(z_v)
    z_g = proj(cb + 2 * wb, 2 * wb)
    emit_ret_qk(z_r)
    emit_ret_vg(z_g)


def _stacked_spec(lead, layers, tail):
    return pl.BlockSpec((lead, layers) + tail, lambda i: (i, 0) + (0,) * len(tail))


def _projection(x, mods, mod_row, layer, seq_len, g, w_bf16, sgu_g, sgu_w, sgu_bias, qg, kg, rope, prev_cache):
    t, d = x.shape
    rows = 2 * PROJ_ROWS if rope is not None else PROJ_ROWS
    wa = sgu_g.shape[2]
    wb = wa
    wc = qg.shape[2]
    has_rope = rope is not None
    tiles_per_seq = max(1, seq_len // rows)
    row_spec = lambda w: pl.BlockSpec((rows, w), lambda i: (i, 0))
    params = [g, w_bf16, sgu_g, sgu_w, sgu_bias, qg, kg]
    in_specs = [row_spec(d), _mod_spec(mods, layer, lambda i: mod_row(i // tiles_per_seq))]
    in_specs += [_layer_spec(p, layer) for p in params]
    args = [x, mods] + params
    outs = [(wa, BF16), (wb, BF16), (wb, F32), (wb, BF16), (wb, F32), (wc, BF16), (wc, BF16), (wc, BF16)]
    out_specs = [row_spec(w) for w, _ in outs]
    out_shape = [jax.ShapeDtypeStruct((t, w), dt) for w, dt in outs]
    if has_rope:
        tab_spec = pl.BlockSpec((rows, LANES), lambda i: (i % tiles_per_seq, 0))
        in_specs += [tab_spec, tab_spec]
        args += list(rope)
    else:
        seqs = rows // seq_len
        k_tail, v_tail = (wc, seq_len), (seq_len, H_C, LANES)
        if layer:
            in_specs += [_stacked_spec(seqs, layer, k_tail), _stacked_spec(seqs, layer, v_tail)]
            args += list(prev_cache)
        out_specs += [_stacked_spec(seqs, layer + 1, k_tail), _stacked_spec(seqs, layer + 1, v_tail)]
        out_shape += [jax.ShapeDtypeStruct((t // seq_len, layer + 1) + k_tail, F32),
                      jax.ShapeDtypeStruct((t // seq_len, layer + 1) + v_tail, F32)]
    return pl.pallas_call(
        functools.partial(_proj_kernel, has_rope=has_rope, seq_len=seq_len, n_prev=0 if has_rope else layer),
        grid=(t // rows,),
        in_specs=in_specs,
        out_specs=out_specs,
        out_shape=out_shape,
        compiler_params=_cparams(("parallel",)),
        name="projection",
    )(*args)


def _log_sigmoid(x):
    y = -x
    return -(jnp.maximum(y, 0.0) + jnp.log1p(jnp.exp(-jnp.abs(y))))


def _ret_kernel(*refs, chunk, n_chunks, has_state, n_prev):
    if has_state:
        (q_ref, k_ref, v_ref, gate_ref, r0f_ref, r0b_ref, lgp_ref, lgh_ref, norm_ref,
         y_ref, d_scr, dec_scr, kv_scr, rs_scr, o_scr) = refs
        state_outs = None
    elif n_prev:
        (q_ref, k_ref, v_ref, gate_ref, rf_prev_ref, rb_prev_ref, lgp_ref, lgh_ref, norm_ref,
         y_ref, rf_ref, rb_ref, d_scr, dec_scr, kv_scr, rs_scr, o_scr) = refs
        rf_ref[:, :n_prev] = rf_prev_ref[...]
        rb_ref[:, :n_prev] = rb_prev_ref[...]
        state_outs = (rf_ref, rb_ref)
    else:
        (q_ref, k_ref, v_ref, gate_ref, lgp_ref, lgh_ref, norm_ref,
         y_ref, rf_ref, rb_ref, d_scr, dec_scr, kv_scr, rs_scr, o_scr) = refs
        state_outs = (rf_ref, rb_ref)
    bt = q_ref.shape[0]
    n_pairs = H_B // 2
    half = LANES // 2
    cs = chunk

    @pl.when(pl.program_id(0) == 0)
    def _build_decay_tables():
        lgp = _log_sigmoid(lgp_ref[...])
        lgh = _log_sigmoid(lgh_ref[...])
        dist = (lax.broadcasted_iota(jnp.int32, (cs, cs), 0) - lax.broadcasted_iota(jnp.int32, (cs, cs), 1)).astype(F32)
        for h in range(H_B):
            lf = lgh[h:h + 1, :]
            lb = lgh[H_B + h:H_B + h + 1, :]
            d_f = jnp.where(dist >= 0, jnp.exp(lf * jnp.maximum(dist, 0.0)), 0.0)
            d_b = jnp.where(dist <= 0, jnp.exp(lb * jnp.maximum(-dist, 0.0)), 0.0)
            d_scr[h // 2, (h % 2) * cs:(h % 2 + 1) * cs, :] = d_f + d_b
        pos = lax.broadcasted_iota(jnp.int32, (cs, LANES), 0).astype(F32)
        for p in range(n_pairs):
            lf = lgp[p:p + 1, :]
            lb = lgp[n_pairs + p:n_pairs + p + 1, :]
            dec_scr[0, p] = jnp.exp(lf * (pos + 1.0))
            dec_scr[1, p] = jnp.exp(lf * (cs - 1.0 - pos))
            dec_scr[2, p] = jnp.exp(lb * (cs - pos))
            dec_scr[3, p] = jnp.exp(lb * pos)
            dec_scr[4, p] = jnp.broadcast_to(jnp.exp(lf * float(cs)), (cs, LANES))
            dec_scr[5, p] = jnp.broadcast_to(jnp.exp(lb * float(cs)), (cs, LANES))

    lo = _lo_mask(cs)
    blockdiag = ((lax.broadcasted_iota(jnp.int32, (LANES, LANES), 0) < half)
                 == (lax.broadcasted_iota(jnp.int32, (LANES, LANES), 1) < half))

    for b in range(bt):
        def intra(c, carry):
            rows = pl.ds(pl.multiple_of(c * cs, cs), cs)
            qc, kc, vc = q_ref[b, rows, :], k_ref[b, rows, :], v_ref[b, rows, :]
            for p in range(n_pairs):
                sl = slice(p * LANES, (p + 1) * LANES)
                k128, vb = kc[:, sl], vc[:, sl]
                inner = (_dot_nt(_split_halves(qc[:, sl]), k128.astype(BF16)) * d_scr[p]).astype(BF16)
                oo = _dot(inner, vb)
                o_scr[rows, sl] = jnp.where(lo, oo[:cs], oo[cs:])
                kk = jnp.concatenate([k128 * dec_scr[1, p], k128 * dec_scr[3, p]], axis=1).astype(BF16)
                kv = _dot_tn(kk, vb)
                kv_scr[0, c, p] = jnp.where(blockdiag, kv[:LANES], 0.0)
                kv_scr[1, c, p] = jnp.where(blockdiag, kv[LANES:], 0.0)
            return carry

        lax.fori_loop(0, n_chunks, intra, 0, unroll=min(n_chunks, RET_UNROLL))

        for p in range(n_pairs):
            for d in range(2):
                if has_state:
                    r0_ref = (r0f_ref, r0b_ref)[d]
                    zero = jnp.zeros((half, half), F32)
                    top = jnp.concatenate([r0_ref[b, 2 * p], zero], axis=1)
                    bot = jnp.concatenate([zero, r0_ref[b, 2 * p + 1]], axis=1)
                    r = jnp.concatenate([top, bot], axis=0)
                else:
                    r = jnp.zeros((LANES, LANES), F32)
                order = range(n_chunks) if d == 0 else range(n_chunks - 1, -1, -1)
                for c in order:
                    rs_scr[c, p, :, d * LANES:(d + 1) * LANES] = r.astype(BF16)
                    r = r * dec_scr[4 + d, p, :LANES, :] + kv_scr[d, c, p]
                if state_outs is not None:
                    state_outs[d][b, n_prev, 2 * p] = r[:half, :half]
                    state_outs[d][b, n_prev, 2 * p + 1] = r[half:, half:]

        def cross(c, carry):
            rows = pl.ds(pl.multiple_of(c * cs, cs), cs)
            qc = q_ref[b, rows, :]
            outs = []
            for p in range(n_pairs):
                sl = slice(p * LANES, (p + 1) * LANES)
                t = _dot(qc[:, sl], rs_scr[c, p])
                outs.append(o_scr[rows, sl] + t[:, :LANES] * dec_scr[0, p] + t[:, LANES:] * dec_scr[2, p])
            o = _group64_rmsnorm(jnp.concatenate(outs, axis=-1), norm_ref[...])
            y_ref[b, rows, :] = (gate_ref[b, rows, :] * o).astype(BF16)
            return carry

        lax.fori_loop(0, n_chunks, cross, 0, unroll=min(n_chunks, RET_UNROLL))


def _mixer_b(q, k, v, gate, layer, lg_pair, lg_head, norm, state, prev_states):
    b, l, wb = q.shape
    bt = max(1, RET_ROWS // l)
    chunk = min(l, RET_CHUNK)
    n_chunks = l // chunk
    has_state = state is not None
    tile = pl.BlockSpec((bt, l, wb), lambda i: (i, 0, 0))
    in_specs = [tile, tile, tile, tile]
    args = [q, k, v, gate]
    out_specs = [tile]
    out_shape = [jax.ShapeDtypeStruct((b, l, wb), BF16)]
    s_tail = (H_B, DK_B, DK_B)
    if has_state:
        sspec = pl.BlockSpec((bt, None) + s_tail, lambda i: (i, layer, 0, 0, 0))
        in_specs += [sspec, sspec]
        args += list(state)
    else:
        if layer:
            in_specs += [_stacked_spec(bt, layer, s_tail)] * 2
            args += list(prev_states)
        out_specs += [_stacked_spec(bt, layer + 1, s_tail)] * 2
        out_shape += [jax.ShapeDtypeStruct((b, layer + 1) + s_tail, F32)] * 2
    params = [lg_pair, lg_head, norm]
    in_specs += [_layer_spec(p, layer) for p in params]
    args += params
    n_pairs = H_B // 2
    return pl.pallas_call(
        functools.partial(_ret_kernel, chunk=chunk, n_chunks=n_chunks, has_state=has_state,
                          n_prev=0 if has_state else layer),
        grid=(b // bt,),
        in_specs=in_specs,
        out_specs=out_specs,
        out_shape=out_shape,
        scratch_shapes=[
            pltpu.VMEM((n_pairs, 2 * chunk, chunk), F32),
            pltpu.VMEM((6, n_pairs, chunk, LANES), F32),
            pltpu.VMEM((2, n_chunks, n_pairs, LANES, LANES), F32),
            pltpu.VMEM((n_chunks, n_pairs, LANES, 2 * LANES), BF16),
            pltpu.VMEM((l, wb), F32),
        ],
        compiler_params=_cparams(("arbitrary",)),
        name="retention",
    )(*args)


def _diff_softmax_t(st, lam, tq):
    n_keys = st.shape[0]
    st4 = st.reshape(n_keys // ATTN_KEY_BLOCK, ATTN_KEY_BLOCK // SUBLANES, SUBLANES, 2 * tq)
    m_loc = jnp.max(st4, axis=1)
    e4 = jnp.exp2(st4 - m_loc[:, None])
    l_loc = jnp.sum(e4, axis=1)
    m = jnp.max(jnp.max(m_loc, axis=0), axis=0, keepdims=True)
    s_loc = jnp.exp2(m_loc - m)
    l = jnp.sum(jnp.sum(l_loc * s_loc, axis=0), axis=0, keepdims=True)
    coef = jnp.concatenate([1.0 / l[:, :tq], lam / l[:, tq:]], axis=1)
    w = e4 * (coef * s_loc)[:, None]
    return (w[..., :tq] - w[..., tq:]).reshape(n_keys, tq).astype(BF16)


def _attn_kernel(*refs, has_cache, lam_init):
    if has_cache:
        (q_ref, k_ref, v_ref, ck_ref, cv_ref, dl_ref, dn_ref, y_ref, k_scr, v_scr) = refs
        n_past = ck_ref.shape[1]

        @pl.when(pl.program_id(1) == 0)
        def _gather_keys():
            k_scr[...] = ck_ref[...].T.astype(BF16)
            cv = jnp.swapaxes(cv_ref[...], 0, 1)
            for h in range(H_C):
                v_scr[:, h * LANES:(h + 1) * LANES] = cv[h].astype(BF16)

        keys = lambda b, sl: jnp.concatenate([k_scr[:, sl], k_ref[b, :, sl]], axis=0)
        vals = lambda b, sl: jnp.concatenate([v_scr[:, sl], v_ref[b, :, sl]], axis=0)
    else:
        (q_ref, k_ref, v_ref, dl_ref, dn_ref, y_ref) = refs
        keys = lambda b, sl: k_ref[b, :, sl]
        vals = lambda b, sl: v_ref[b, :, sl]
    bt, tq, _ = q_ref.shape
    dl = dl_ref[...]
    lam = (jnp.exp(jnp.sum(dl[0:1] * dl[1:2], axis=-1, keepdims=True))
           - jnp.exp(jnp.sum(dl[2:3] * dl[3:4], axis=-1, keepdims=True)) + lam_init)
    units = [(b, slice(h * LANES, (h + 1) * LANES)) for b in range(bt) for h in range(H_C)]
    scores = lambda b, sl: _dot_nt(keys(b, sl), _split_halves(q_ref[b, :, sl]))
    st_next = scores(*units[0])
    for i, (b, sl) in enumerate(units):
        st = st_next
        if i + 1 < len(units):
            st_next = scores(*units[i + 1])
        o = _dot_tn(_diff_softmax_t(st, lam, tq), vals(b, sl))
        y_ref[b, :, sl] = (_rmsnorm(o, dn_ref[...]) * (1.0 - lam_init)).astype(BF16)


def _mixer_c(q, k, v, layer, dl, dn, lam_init, cache):
    b, l, wc = q.shape
    has_cache = cache is not None
    tq = min(l, ATTN_Q_ROWS)
    bt = 1 if has_cache else max(1, ATTN_SEQ_ROWS // l)
    q_spec = pl.BlockSpec((bt, tq, wc), lambda i, j: (i, j, 0))
    kv_spec = pl.BlockSpec((bt, l, wc), lambda i, j: (i, 0, 0))
    in_specs = [q_spec, kv_spec, kv_spec]
    args = [q, k, v]
    n_keys = l
    if has_cache:
        ck_t, cv = cache
        n_past = cv.shape[2]
        n_keys += n_past
        in_specs += [pl.BlockSpec((None, None, wc, n_past), lambda i, j: (i, layer, 0, 0)),
                     pl.BlockSpec((None, None, n_past, H_C, LANES), lambda i, j: (i, layer, 0, 0, 0))]
        args += [ck_t, cv]
    in_specs += [_layer_spec(dl, layer), _layer_spec(dn, layer)]
    args += [dl, dn]
    scratch = [pltpu.VMEM((n_past, wc), BF16), pltpu.VMEM((n_past, wc), BF16)] if has_cache else []
    return pl.pallas_call(
        functools.partial(_attn_kernel, has_cache=has_cache, lam_init=lam_init),
        grid=(b // bt, l // tq),
        in_specs=in_specs,
        out_specs=q_spec,
        out_shape=jax.ShapeDtypeStruct((b, l, wc), BF16),
        scratch_shapes=scratch,
        compiler_params=_cparams(("parallel", "arbitrary")),
        name="diff_attention",
    )(*args)


def _mlp_kernel(ya_ref, yb_ref, yc_ref, x_ref, mod_ref, g_ref, wo_ref, up_ref, cw_ref, cb_ref, down_ref,
                o_ref, h_scr, g_scr, *, seq_len):
    rows = x_ref.shape[0]
    m = mod_ref[0]
    y = None
    k0 = 0
    for y_ref in (ya_ref, yb_ref, yc_ref):
        kw = y_ref.shape[1]
        part = _dot(y_ref[...], wo_ref[k0:k0 + kw, :])
        y = part if y is None else y + part
        k0 += kw
    x1 = x_ref[...] + m[2:3, :] * y
    o_ref[...] = x1
    h_scr[...] = (_rmsnorm(x1, g_ref[...]) * (1.0 + m[4:5, :]) + m[3:4, :]).astype(BF16)

    d_ff = down_ref.shape[0]
    fc = MXU_DIM
    pos = lax.broadcasted_iota(jnp.int32, (rows, fc), 0) % seq_len
    has_prev = pos > 0
    has_next = pos < seq_len - 1

    def conv(c0):
        u = _dot(h_scr[...], up_ref[:, c0:c0 + fc])
        w = cw_ref[:, c0:c0 + fc]
        prev = jnp.where(has_prev, pltpu.roll(u, 1, 0), 0.0)
        nxt = jnp.where(has_next, pltpu.roll(u, rows - 1, 0), 0.0)
        return prev * w[0:1, :] + u * w[1:2, :] + nxt * w[2:3, :] + cb_ref[:, c0:c0 + fc]

    for j in range(d_ff // fc):
        a = conv(j * fc)
        b = conv(d_ff + j * fc)
        g_scr[:, j * fc:(j + 1) * fc] = (_silu(a) * b).astype(BF16)

    o_ref[...] = o_ref[...] + m[5:6, :] * _dot(g_scr[...], down_ref[...])


def _mlp(ya, yb, yc, x, mods, mod_row, layer, seq_len, g, w_out, up, cw, cb, down):
    t, d = x.shape
    rows = MLP_ROWS
    tiles_per_seq = max(1, seq_len // rows)
    row_spec = lambda a: pl.BlockSpec((rows, a.shape[1]), lambda i: (i, 0))
    weights = [w_out, up, cw, cb, down]
    return pl.pallas_call(
        functools.partial(_mlp_kernel, seq_len=seq_len),
        grid=(t // rows,),
        in_specs=[row_spec(ya), row_spec(yb), row_spec(yc), row_spec(x),
                  _mod_spec(mods, layer, lambda i: mod_row(i // tiles_per_seq)), _layer_spec(g, layer)]
                 + [_layer_spec(w, layer, resident=True) for w in weights],
        out_specs=row_spec(x),
        out_shape=jax.ShapeDtypeStruct((t, d), F32),
        scratch_shapes=[pltpu.VMEM((rows, d), BF16), pltpu.VMEM((rows, down.shape[1]), BF16)],
        compiler_params=_cparams(("parallel",)),
        name="out_proj_mlp",
    )(ya, yb, yc, x, mods, g, *weights)


def _rope_tables(l):
    rows = l // GRID_W
    t_row = jnp.repeat(jnp.arange(rows, dtype=F32), GRID_W)
    t_col = jnp.tile(jnp.arange(GRID_W, dtype=F32), rows)
    inv = ROPE_BASE ** (-jnp.arange(ROPE_PAIRS, dtype=F32) / ROPE_PAIRS)
    ar, ac = t_row[:, None] * inv, t_col[:, None] * inv
    cos = jnp.concatenate([jnp.cos(ar), jnp.cos(ar), jnp.cos(ac), jnp.cos(ac)], axis=-1)
    sin = jnp.concatenate([-jnp.sin(ar), jnp.sin(ar), -jnp.sin(ac), jnp.sin(ac)], axis=-1)
    return jnp.tile(cos, (1, 2)), jnp.tile(sin, (1, 2))


def kernel(x_prompt, x_sample, c, cache_k, cache_v, state_ret_fwd, state_ret_bwd, c_ctx, norm1, w_mod, b_mod,
           w_in, sgu_norm, sgu_w, sgu_b, ret_logit_fwd, ret_logit_bwd, ret_norm, q_norm, k_norm, diff_lam,
           diff_norm, w_out, norm2, ffn_up, ffn_conv, ffn_conv_b, ffn_down):
    depth, d_model, _ = w_mod.shape
    batch, seq, _ = x_prompt.shape
    dec_batch, dec_seq, _ = x_sample.shape
    w_a = sgu_norm.shape[1]
    w_b = H_B * ret_norm.shape[2]
    w_c = 2 * H_C * HD_C
    past = cache_k.shape[2]

    pad = (-(dec_batch + 1)) % 8
    cond = jnp.concatenate([c, c_ctx[None, :], jnp.zeros((pad, d_model), F32)], axis=0)
    mods = _modulation(cond, w_mod, b_mod).reshape(depth, cond.shape[0], 6, d_model)

    w_in_b = w_in.astype(BF16)
    w_out_b = w_out.astype(BF16)
    sgu_w_b = sgu_w.astype(BF16)
    up_b = ffn_up.astype(BF16)
    down_b = ffn_down.astype(BF16)
    row3 = lambda a: a.reshape(depth, 1, -1)
    sgu_bias = jnp.repeat(jnp.swapaxes(sgu_b, 1, 2), w_a // G_A, axis=2)
    logits = jnp.stack([ret_logit_fwd, ret_logit_bwd], axis=1)
    lg_pair = jnp.repeat(logits, DK_B, axis=2).reshape(depth, 2 * (H_B // 2), LANES)
    lg_head = logits.reshape(depth, 2 * H_B, 1)
    qg = row3(jnp.tile(q_norm, (1, 2 * H_C)))
    kg = row3(jnp.tile(k_norm, (1, 2 * H_C)))
    norm1_r, norm2_r, sgu_g, ret_g, diff_g = row3(norm1), row3(norm2), row3(sgu_norm), row3(ret_norm), row3(diff_norm)
    conv_b = row3(ffn_conv_b)
    rope = _rope_tables(dec_seq)
    cache_kt = jnp.transpose(cache_k, (0, 1, 3, 4, 5, 2)).reshape(dec_batch, depth, w_c, past)

    def layer(x, l, nb, sl, mod_row, is_sample, new_cache, new_states):
        lam_init = 0.8 - 0.6 * math.exp(-0.3 * l)
        outs = _projection(x, mods, mod_row, l, sl, norm1_r, w_in_b, sgu_g, sgu_w_b, sgu_bias, qg, kg,
                           rope if is_sample else None, new_cache)
        ya, qr, kr, vr, gr, qa, ka, va = outs[:8]
        seq3 = lambda a: a.reshape(nb, sl, a.shape[1])
        state = (state_ret_fwd, state_ret_bwd) if is_sample else None
        yb, *states = _mixer_b(seq3(qr), seq3(kr), seq3(vr), seq3(gr), l, lg_pair, lg_head, ret_g, state,
                               new_states)
        cache = (cache_kt, cache_v) if is_sample else None
        yc = _mixer_c(seq3(qa), seq3(ka), seq3(va), l, diff_lam, diff_g, lam_init, cache)
        x = _mlp(ya, yb.reshape(nb * sl, w_b), yc.reshape(nb * sl, w_c), x, mods, mod_row, l, sl, norm2_r,
                 w_out_b, up_b, ffn_conv, conv_b, down_b)
        return x, outs[8:], states

    y_prompt = x_prompt.reshape(batch * seq, d_model)
    new_cache, new_states = None, None
    for l in range(depth):
        y_prompt, new_cache, new_states = layer(y_prompt, l, batch, seq, lambda b: dec_batch, False,
                                                new_cache, new_states)

    y_sample = x_sample.reshape(dec_batch * dec_seq, d_model)
    for l in range(depth):
        y_sample, _, _ = layer(y_sample, l, dec_batch, dec_seq, lambda b: b, True, None, None)

    new_kt, new_v = new_cache
    new_k = jnp.transpose(new_kt.reshape(batch, depth, H_C, 2, HD_C, seq), (0, 1, 5, 2, 3, 4))
    return (y_prompt.reshape(batch, seq, d_model), y_sample.reshape(dec_batch, dec_seq, d_model),
            new_k, new_v, new_states[0], new_states[1])
```

```python
import functools
import math

import jax
import jax.numpy as jnp
from jax import lax
from jax.experimental import pallas as pl
from jax.experimental.pallas import tpu as pltpu

F32 = jnp.float32
BF16 = jnp.bfloat16

GRID_W = 64
CHUNK = 128
EPS = 1e-6
ROPE_BASE = 10000.0
G_A = 4
H_B = 4
DK_B = 64
H_C = 4
HD_C = 64
ROPE_PAIRS = HD_C // 4
LANES = 128
SUBLANES = 8
MXU_DIM = 256
MOD_COL_TILES = 4
PROJ_ROWS = 512
MLP_ROWS = 1024
ATTN_Q_ROWS = 1024
ATTN_KEY_BLOCK = 128
ATTN_SEQ_ROWS = 2048
RET_ROWS = 2048
RET_CHUNK = 256
RET_UNROLL = 4
VMEM_LIMIT = 60 * 1024 * 1024


def _cparams(sem):
    return pltpu.CompilerParams(dimension_semantics=sem, vmem_limit_bytes=VMEM_LIMIT)


def _const_spec(shape):
    n = len(shape)
    return pl.BlockSpec(shape, lambda *_: (0,) * n)


def _layer_spec(arr, layer, resident=False):
    n = arr.ndim - 1
    mode = dict(pipeline_mode=pl.Buffered(1)) if resident else {}
    return pl.BlockSpec((None,) + arr.shape[1:], lambda *_: (layer,) + (0,) * n, **mode)


def _silu(x):
    return x * jax.nn.sigmoid(x)


def _dot(a, b):
    return jnp.dot(a, b, preferred_element_type=F32)


def _dot_nt(a, b):
    return lax.dot_general(a, b, (((1,), (1,)), ((), ())), preferred_element_type=F32)


def _dot_tn(a, b):
    return lax.dot_general(a, b, (((0,), (0,)), ((), ())), preferred_element_type=F32)


def _lo_mask(rows):
    return lax.broadcasted_iota(jnp.int32, (rows, LANES), 1) < (LANES // 2)


def _split_halves(x):
    lo = _lo_mask(x.shape[0])
    zero = jnp.zeros_like(x)
    return jnp.concatenate([jnp.where(lo, x, zero), jnp.where(lo, zero, x)], axis=0)


def _group64_mean(x2):
    rows, n = x2.shape
    lo = _lo_mask(rows)
    outs = []
    for j in range(n // LANES):
        blk = x2[:, j * LANES:(j + 1) * LANES]
        s_lo = jnp.sum(jnp.where(lo, blk, 0.0), axis=-1, keepdims=True)
        s_hi = jnp.sum(jnp.where(lo, 0.0, blk), axis=-1, keepdims=True)
        outs.append(jnp.where(lo, s_lo, s_hi))
    out = outs[0] if len(outs) == 1 else jnp.concatenate(outs, axis=-1)
    return out * (1.0 / (LANES // 2))


def _group64_rmsnorm(x, g):
    return x * lax.rsqrt(_group64_mean(x * x) + EPS) * g


def _rmsnorm(x, g):
    return x * lax.rsqrt(jnp.mean(x * x, axis=-1, keepdims=True) + EPS) * g


def _rope(x, cos, sin):
    rows, n = x.shape
    lane = lax.broadcasted_iota(jnp.int32, (rows, LANES), 1)
    first = (lane % (2 * ROPE_PAIRS)) < ROPE_PAIRS
    outs = []
    for j in range(n // LANES):
        blk = x[:, j * LANES:(j + 1) * LANES]
        partner = jnp.where(first, pltpu.roll(blk, LANES - ROPE_PAIRS, 1), pltpu.roll(blk, ROPE_PAIRS, 1))
        outs.append(blk * cos + partner * sin)
    return jnp.concatenate(outs, axis=-1)


def _mod_kernel(cond_ref, w_ref, b_ref, o_ref):
    s = _silu(cond_ref[...]).astype(BF16)
    o_ref[0] = _dot(s, w_ref[0].astype(BF16)) + b_ref[0]


def _modulation(cond, w_mod, b_mod):
    depth, d, n = w_mod.shape
    rows = cond.shape[0]
    tn = n // MOD_COL_TILES
    return pl.pallas_call(
        _mod_kernel,
        grid=(depth, n // tn),
        in_specs=[
            _const_spec((rows, d)),
            pl.BlockSpec((1, d, tn), lambda l, j: (l, 0, j)),
            pl.BlockSpec((1, 1, tn), lambda l, j: (l, 0, j)),
        ],
        out_specs=pl.BlockSpec((1, rows, tn), lambda l, j: (l, 0, j)),
        out_shape=jax.ShapeDtypeStruct((depth, rows, n), F32),
        compiler_params=_cparams(("parallel", "parallel")),
        name="adaln_mod",
    )(cond, w_mod, b_mod.reshape(depth, 1, n))


def _mod_spec(mods, layer, row_of_step):
    return pl.BlockSpec((None, 1) + mods.shape[2:], lambda i: (layer, row_of_step(i), 0, 0))


def _proj_kernel(*refs, has_rope, seq_len, n_prev):
    if has_rope:
        (x_ref, mod_ref, g_ref, w_ref, sg_ref, ws_ref, sb_ref, qg_ref, kg_ref, cos_ref, sin_ref,
         ya_ref, qr_ref, kr_ref, vr_ref, gr_ref, qa_ref, ka_ref, va_ref) = refs
    elif n_prev:
        (x_ref, mod_ref, g_ref, w_ref, sg_ref, ws_ref, sb_ref, qg_ref, kg_ref, kt_prev_ref, vf_prev_ref,
         ya_ref, qr_ref, kr_ref, vr_ref, gr_ref, qa_ref, ka_ref, va_ref, kt_ref, vf_ref) = refs
        kt_ref[:, :n_prev] = kt_prev_ref[...]
        vf_ref[:, :n_prev] = vf_prev_ref[...]
    else:
        (x_ref, mod_ref, g_ref, w_ref, sg_ref, ws_ref, sb_ref, qg_ref, kg_ref,
         ya_ref, qr_ref, kr_ref, vr_ref, gr_ref, qa_ref, ka_ref, va_ref, kt_ref, vf_ref) = refs
    rows = x_ref.shape[0]
    m = mod_ref[0]
    hb = (_rmsnorm(x_ref[...], g_ref[...]) * (1.0 + m[1:2, :]) + m[0:1, :]).astype(BF16)
    wa = ya_ref.shape[1]
    wb = qr_ref.shape[1]
    wc = qa_ref.shape[1]

    def proj(c0, width):
        return _dot(hb, w_ref[:, c0:c0 + width])

    def emit_q(z):
        qn = _group64_rmsnorm(z, qg_ref[...])
        if has_rope:
            qn = _rope(qn, cos_ref[...], sin_ref[...])
        qa_ref[...] = (qn * (HD_C ** -0.5 * math.log2(math.e))).astype(BF16)

    def emit_k(z):
        kn = _group64_rmsnorm(z, kg_ref[...])
        if has_rope:
            kn = _rope(kn, cos_ref[...], sin_ref[...])
        else:
            for s in range(rows // seq_len):
                kt_ref[s, n_prev] = kn[s * seq_len:(s + 1) * seq_len, :].T
        ka_ref[...] = kn.astype(BF16)

    def emit_v(z):
        if not has_rope:
            for s in range(rows // seq_len):
                zs = z[s * seq_len:(s + 1) * seq_len, :]
                heads = jnp.stack([zs[:, h * LANES:(h + 1) * LANES] for h in range(H_C)], axis=0)
                vf_ref[s, n_prev] = jnp.swapaxes(heads, 0, 1)
        va_ref[...] = z.astype(BF16)

    def sgu_prepare(z):
        za = jax.nn.gelu(z)
        return za[:, :wa], _rmsnorm(za[:, wa:], sg_ref[...]).astype(BF16)

    def sgu_mix(u, v):
        lane = lax.broadcasted_iota(jnp.int32, (CHUNK, wa), 1)
        dg = wa // G_A
        for c in range(rows // CHUNK):
            rs = slice(c * CHUNK, (c + 1) * CHUNK)
            s = _dot(ws_ref[G_A - 1], v[rs])
            for g in range(G_A - 2, -1, -1):
                s = jnp.where(lane < (g + 1) * dg, _dot(ws_ref[g], v[rs]), s)
            ya_ref[rs, :] = (u[rs] * (s + sb_ref[...])).astype(BF16)

    def emit_ret_qk(z):
        qr_ref[...] = z[:, :wb].astype(BF16)
        kr_ref[...] = z[:, wb:] * (DK_B ** -0.5)

    def emit_ret_vg(z):
        vr_ref[...] = z[:, :wb].astype(BF16)
        gr_ref[...] = _silu(z[:, wb:])

    cb = 2 * wa
    cc = cb + 4 * wb
    z_a = proj(0, 2 * wa)
    z_q = proj(cc, wc)
    u, v = sgu_prepare(z_a)
    z_k = proj(cc + wc, wc)
    sgu_mix(u, v)
    emit_q(z_q)
    z_v = proj(cc + 2 * wc, wc)
    emit_k(z_k)
    z_r = proj(cb, 2 * wb)
    emit_v(z_v)
    z_g = proj(cb + 2 * wb, 2 * wb)
    emit_ret_qk(z_r)
    emit_ret_vg(z_g)


def _stacked_spec(lead, layers, tail):
    return pl.BlockSpec((lead, layers) + tail, lambda i: (i, 0) + (0,) * len(tail))


def _projection(x, mods, mod_row, layer, seq_len, g, w_bf16, sgu_g, sgu_w, sgu_bias, qg, kg, rope, prev_cache):
    t, d = x.shape
    rows = 2 * PROJ_ROWS if rope is not None else PROJ_ROWS
    wa = sgu_g.shape[2]
    wb = wa
    wc = qg.shape[2]
    has_rope = rope is not None
    tiles_per_seq = max(1, seq_len // rows)
    row_spec = lambda w: pl.BlockSpec((rows, w), lambda i: (i, 0))
    params = [g, w_bf16, sgu_g, sgu_w, sgu_bias, qg, kg]
    in_specs = [row_spec(d), _mod_spec(mods, layer, lambda i: mod_row(i // tiles_per_seq))]
    in_specs += [_layer_spec(p, layer) for p in params]
    args = [x, mods] + params
    outs = [(wa, BF16), (wb, BF16), (wb, F32), (wb, BF16), (wb, F32), (wc, BF16), (wc, BF16), (wc, BF16)]
    out_specs = [row_spec(w) for w, _ in outs]
    out_shape = [jax.ShapeDtypeStruct((t, w), dt) for w, dt in outs]
    if has_rope:
        tab_spec = pl.BlockSpec((rows, LANES), lambda i: (i % tiles_per_seq, 0))
        in_specs += [tab_spec, tab_spec]
        args += list(rope)
    else:
        seqs = rows // seq_len
        k_tail, v_tail = (wc, seq_len), (seq_len, H_C, LANES)
        if layer:
            in_specs += [_stacked_spec(seqs, layer, k_tail), _stacked_spec(seqs, layer, v_tail)]
            args += list(prev_cache)
        out_specs += [_stacked_spec(seqs, layer + 1, k_tail), _stacked_spec(seqs, layer + 1, v_tail)]
        out_shape += [jax.ShapeDtypeStruct((t // seq_len, layer + 1) + k_tail, F32),
                      jax.ShapeDtypeStruct((t // seq_len, layer + 1) + v_tail, F32)]
    return pl.pallas_call(
        functools.partial(_proj_kernel, has_rope=has_rope, seq_len=seq_len, n_prev=0 if has_rope else layer),
        grid=(t // rows,),
        in_specs=in_specs,
        out_specs=out_specs,
        out_shape=out_shape,
        compiler_params=_cparams(("parallel",)),
        name="projection",
    )(*args)


def _log_sigmoid(x):
    y = -x
    return -(jnp.maximum(y, 0.0) + jnp.log1p(jnp.exp(-jnp.abs(y))))


def _ret_kernel(*refs, chunk, n_chunks, has_state, n_prev):
    if has_state:
        (q_ref, k_ref, v_ref, gate_ref, r0f_ref, r0b_ref, lgp_ref, lgh_ref, norm_ref,
         y_ref, d_scr, dec_scr, kv_scr, rs_scr, o_scr) = refs
        state_outs = None
    elif n_prev:
        (q_ref, k_ref, v_ref, gate_ref, rf_prev_ref, rb_prev_ref, lgp_ref, lgh_ref, norm_ref,
         y_ref, rf_ref, rb_ref, d_scr, dec_scr, kv_scr, rs_scr, o_scr) = refs
        rf_ref[:, :n_prev] = rf_prev_ref[...]
        rb_ref[:, :n_prev] = rb_prev_ref[...]
        state_outs = (rf_ref, rb_ref)
    else:
        (q_ref, k_ref, v_ref, gate_ref, lgp_ref, lgh_ref, norm_ref,
         y_ref, rf_ref, rb_ref, d_scr, dec_scr, kv_scr, rs_scr, o_scr) = refs
        state_outs = (rf_ref, rb_ref)
    bt = q_ref.shape[0]
    n_pairs = H_B // 2
    half = LANES // 2
    cs = chunk

    @pl.when(pl.program_id(0) == 0)
    def _build_decay_tables():
        lgp = _log_sigmoid(lgp_ref[...])
        lgh = _log_sigmoid(lgh_ref[...])
        dist = (lax.broadcasted_iota(jnp.int32, (cs, cs), 0) - lax.broadcasted_iota(jnp.int32, (cs, cs), 1)).astype(F32)
        for h in range(H_B):
            lf = lgh[h:h + 1, :]
            lb = lgh[H_B + h:H_B + h + 1, :]
            d_f = jnp.where(dist >= 0, jnp.exp(lf * jnp.maximum(dist, 0.0)), 0.0)
            d_b = jnp.where(dist <= 0, jnp.exp(lb * jnp.maximum(-dist, 0.0)), 0.0)
            d_scr[h // 2, (h % 2) * cs:(h % 2 + 1) * cs, :] = d_f + d_b
        pos = lax.broadcasted_iota(jnp.int32, (cs, LANES), 0).astype(F32)
        for p in range(n_pairs):
            lf = lgp[p:p + 1, :]
            lb = lgp[n_pairs + p:n_pairs + p + 1, :]
            dec_scr[0, p] = jnp.exp(lf * (pos + 1.0))
            dec_scr[1, p] = jnp.exp(lf * (cs - 1.0 - pos))
            dec_scr[2, p] = jnp.exp(lb * (cs - pos))
            dec_scr[3, p] = jnp.exp(lb * pos)
            dec_scr[4, p] = jnp.broadcast_to(jnp.exp(lf * float(cs)), (cs, LANES))
            dec_scr[5, p] = jnp.broadcast_to(jnp.exp(lb * float(cs)), (cs, LANES))

    lo = _lo_mask(cs)
    blockdiag = ((lax.broadcasted_iota(jnp.int32, (LANES, LANES), 0) < half)
                 == (lax.broadcasted_iota(jnp.int32, (LANES, LANES), 1) < half))

    for b in range(bt):
        def intra(c, carry):
            rows = pl.ds(pl.multiple_of(c * cs, cs), cs)
            qc, kc, vc = q_ref[b, rows, :], k_ref[b, rows, :], v_ref[b, rows, :]
            for p in range(n_pairs):
                sl = slice(p * LANES, (p + 1) * LANES)
                k128, vb = kc[:, sl], vc[:, sl]
                inner = (_dot_nt(_split_halves(qc[:, sl]), k128.astype(BF16)) * d_scr[p]).astype(BF16)
                oo = _dot(inner, vb)
                o_scr[rows, sl] = jnp.where(lo, oo[:cs], oo[cs:])
                kk = jnp.concatenate([k128 * dec_scr[1, p], k128 * dec_scr[3, p]], axis=1).astype(BF16)
                kv = _dot_tn(kk, vb)
                kv_scr[0, c, p] = jnp.where(blockdiag, kv[:LANES], 0.0)
                kv_scr[1, c, p] = jnp.where(blockdiag, kv[LANES:], 0.0)
            return carry

        lax.fori_loop(0, n_chunks, intra, 0, unroll=min(n_chunks, RET_UNROLL))

        for p in range(n_pairs):
            for d in range(2):
                if has_state:
                    r0_ref = (r0f_ref, r0b_ref)[d]
                    zero = jnp.zeros((half, half), F32)
                    top = jnp.concatenate([r0_ref[b, 2 * p], zero], axis=1)
                    bot = jnp.concatenate([zero, r0_ref[b, 2 * p + 1]], axis=1)
                    r = jnp.concatenate([top, bot], axis=0)
                else:
                    r = jnp.zeros((LANES, LANES), F32)
                order = range(n_chunks) if d == 0 else range(n_chunks - 1, -1, -1)
                for c in order:
                    rs_scr[c, p, :, d * LANES:(d + 1) * LANES] = r.astype(BF16)
                    r = r * dec_scr[4 + d, p, :LANES, :] + kv_scr[d, c, p]
                if state_outs is not None:
                    state_outs[d][b, n_prev, 2 * p] = r[:half, :half]
                    state_outs[d][b, n_prev, 2 * p + 1] = r[half:, half:]

        def cross(c, carry):
            rows = pl.ds(pl.multiple_of(c * cs, cs), cs)
            qc = q_ref[b, rows, :]
            outs = []
            for p in range(n_pairs):
                sl = slice(p * LANES, (p + 1) * LANES)
                t = _dot(qc[:, sl], rs_scr[c, p])
                outs.append(o_scr[rows, sl] + t[:, :LANES] * dec_scr[0, p] + t[:, LANES:] * dec_scr[2, p])
            o = _group64_rmsnorm(jnp.concatenate(outs, axis=-1), norm_ref[...])
            y_ref[b, rows, :] = (gate_ref[b, rows, :] * o).astype(BF16)
            return carry

        lax.fori_loop(0, n_chunks, cross, 0, unroll=min(n_chunks, RET_UNROLL))


def _mixer_b(q, k, v, gate, layer, lg_pair, lg_head, norm, state, prev_states):
    b, l, wb = q.shape
    bt = max(1, RET_ROWS // l)
    chunk = min(l, RET_CHUNK)
    n_chunks = l // chunk
    has_state = state is not None
    tile = pl.BlockSpec((bt, l, wb), lambda i: (i, 0, 0))
    in_specs = [tile, tile, tile, tile]
    args = [q, k, v, gate]
    out_specs = [tile]
    out_shape = [jax.ShapeDtypeStruct((b, l, wb), BF16)]
    s_tail = (H_B, DK_B, DK_B)
    if has_state:
        sspec = pl.BlockSpec((bt, None) + s_tail, lambda i: (i, layer, 0, 0, 0))
        in_specs += [sspec, sspec]
        args += list(state)
    else:
        if layer:
            in_specs += [_stacked_spec(bt, layer, s_tail)] * 2
            args += list(prev_states)
        out_specs += [_stacked_spec(bt, layer + 1, s_tail)] * 2
        out_shape += [jax.ShapeDtypeStruct((b, layer + 1) + s_tail, F32)] * 2
    params = [lg_pair, lg_head, norm]
    in_specs += [_layer_spec(p, layer) for p in params]
    args += params
    n_pairs = H_B // 2
    return pl.pallas_call(
        functools.partial(_ret_kernel, chunk=chunk, n_chunks=n_chunks, has_state=has_state,
                          n_prev=0 if has_state else layer),
        grid=(b // bt,),
        in_specs=in_specs,
        out_specs=out_specs,
        out_shape=out_shape,
        scratch_shapes=[
            pltpu.VMEM((n_pairs, 2 * chunk, chunk), F32),
            pltpu.VMEM((6, n_pairs, chunk, LANES), F32),
            pltpu.VMEM((2, n_chunks, n_pairs, LANES, LANES), F32),
            pltpu.VMEM((n_chunks, n_pairs, LANES, 2 * LANES), BF16),
            pltpu.VMEM((l, wb), F32),
        ],
        compiler_params=_cparams(("arbitrary",)),
        name="retention",
    )(*args)


def _diff_softmax_t(st, lam, tq):
    n_keys = st.shape[0]
    st4 = st.reshape(n_keys // ATTN_KEY_BLOCK, ATTN_KEY_BLOCK // SUBLANES, SUBLANES, 2 * tq)
    m_loc = jnp.max(st4, axis=1)
    e4 = jnp.exp2(st4 - m_loc[:, None])
    l_loc = jnp.sum(e4, axis=1)
    m = jnp.max(jnp.max(m_loc, axis=0), axis=0, keepdims=True)
    s_loc = jnp.exp2(m_loc - m)
    l = jnp.sum(jnp.sum(l_loc * s_loc, axis=0), axis=0, keepdims=True)
    coef = jnp.concatenate([1.0 / l[:, :tq], lam / l[:, tq:]], axis=1)
    w = e4 * (coef * s_loc)[:, None]
    return (w[..., :tq] - w[..., tq:]).reshape(n_keys, tq).astype(BF16)


def _attn_kernel(*refs, has_cache, lam_init):
    if has_cache:
        (q_ref, k_ref, v_ref, ck_ref, cv_ref, dl_ref, dn_ref, y_ref, k_scr, v_scr) = refs
        n_past = ck_ref.shape[1]

        @pl.when(pl.program_id(1) == 0)
        def _gather_keys():
            k_scr[...] = ck_ref[...].T.astype(BF16)
            cv = jnp.swapaxes(cv_ref[...], 0, 1)
            for h in range(H_C):
                v_scr[:, h * LANES:(h + 1) * LANES] = cv[h].astype(BF16)

        keys = lambda b, sl: jnp.concatenate([k_scr[:, sl], k_ref[b, :, sl]], axis=0)
        vals = lambda b, sl: jnp.concatenate([v_scr[:, sl], v_ref[b, :, sl]], axis=0)
    else:
        (q_ref, k_ref, v_ref, dl_ref, dn_ref, y_ref) = refs
        keys = lambda b, sl: k_ref[b, :, sl]
        vals = lambda b, sl: v_ref[b, :, sl]
    bt, tq, _ = q_ref.shape
    dl = dl_ref[...]
    lam = (jnp.exp(jnp.sum(dl[0:1] * dl[1:2], axis=-1, keepdims=True))
           - jnp.exp(jnp.sum(dl[2:3] * dl[3:4], axis=-1, keepdims=True)) + lam_init)
    units = [(b, slice(h * LANES, (h + 1) * LANES)) for b in range(bt) for h in range(H_C)]
    scores = lambda b, sl: _dot_nt(keys(b, sl), _split_halves(q_ref[b, :, sl]))
    st_next = scores(*units[0])
    for i, (b, sl) in enumerate(units):
        st = st_next
        if i + 1 < len(units):
            st_next = scores(*units[i + 1])
        o = _dot_tn(_diff_softmax_t(st, lam, tq), vals(b, sl))
        y_ref[b, :, sl] = (_rmsnorm(o, dn_ref[...]) * (1.0 - lam_init)).astype(BF16)


def _mixer_c(q, k, v, layer, dl, dn, lam_init, cache):
    b, l, wc = q.shape
    has_cache = cache is not None
    tq = min(l, ATTN_Q_ROWS)
    bt = 1 if has_cache else max(1, ATTN_SEQ_ROWS // l)
    q_spec = pl.BlockSpec((bt, tq, wc), lambda i, j: (i, j, 0))
    kv_spec = pl.BlockSpec((bt, l, wc), lambda i, j: (i, 0, 0))
    in_specs = [q_spec, kv_spec, kv_spec]
    args = [q, k, v]
    n_keys = l
    if has_cache:
        ck_t, cv = cache
        n_past = cv.shape[2]
        n_keys += n_past
        in_specs += [pl.BlockSpec((None, None, wc, n_past), lambda i, j: (i, layer, 0, 0)),
                     pl.BlockSpec((None, None, n_past, H_C, LANES), lambda i, j: (i, layer, 0, 0, 0))]
        args += [ck_t, cv]
    in_specs += [_layer_spec(dl, layer), _layer_spec(dn, layer)]
    args += [dl, dn]
    scratch = [pltpu.VMEM((n_past, wc), BF16), pltpu.VMEM((n_past, wc), BF16)] if has_cache else []
    return pl.pallas_call(
        functools.partial(_attn_kernel, has_cache=has_cache, lam_init=lam_init),
        grid=(b // bt, l // tq),
        in_specs=in_specs,
        out_specs=q_spec,
        out_shape=jax.ShapeDtypeStruct((b, l, wc), BF16),
        scratch_shapes=scratch,
        compiler_params=_cparams(("parallel", "arbitrary")),
        name="diff_attention",
    )(*args)


def _mlp_kernel(ya_ref, yb_ref, yc_ref, x_ref, mod_ref, g_ref, wo_ref, up_ref, cw_ref, cb_ref, down_ref,
                o_ref, h_scr, g_scr, *, seq_len):
    rows = x_ref.shape[0]
    m = mod_ref[0]
    y = None
    k0 = 0
    for y_ref in (ya_ref, yb_ref, yc_ref):
        kw = y_ref.shape[1]
        part = _dot(y_ref[...], wo_ref[k0:k0 + kw, :])
        y = part if y is None else y + part
        k0 += kw
    x1 = x_ref[...] + m[2:3, :] * y
    o_ref[...] = x1
    h_scr[...] = (_rmsnorm(x1, g_ref[...]) * (1.0 + m[4:5, :]) + m[3:4, :]).astype(BF16)

    d_ff = down_ref.shape[0]
    fc = MXU_DIM
    pos = lax.broadcasted_iota(jnp.int32, (rows, fc), 0) % seq_len
    has_prev = pos > 0
    has_next = pos < seq_len - 1

    def conv(c0):
        u = _dot(h_scr[...], up_ref[:, c0:c0 + fc])
        w = cw_ref[:, c0:c0 + fc]
        prev = jnp.where(has_prev, pltpu.roll(u, 1, 0), 0.0)
        nxt = jnp.where(has_next, pltpu.roll(u, rows - 1, 0), 0.0)
        return prev * w[0:1, :] + u * w[1:2, :] + nxt * w[2:3, :] + cb_ref[:, c0:c0 + fc]

    for j in range(d_ff // fc):
        a = conv(j * fc)
        b = conv(d_ff + j * fc)
        g_scr[:, j * fc:(j + 1) * fc] = (_silu(a) * b).astype(BF16)

    o_ref[...] = o_ref[...] + m[5:6, :] * _dot(g_scr[...], down_ref[...])


def _mlp(ya, yb, yc, x, mods, mod_row, layer, seq_len, g, w_out, up, cw, cb, down):
    t, d = x.shape
    rows = MLP_ROWS
    tiles_per_seq = max(1, seq_len // rows)
    row_spec = lambda a: pl.BlockSpec((rows, a.shape[1]), lambda i: (i, 0))
    weights = [w_out, up, cw, cb, down]
    return pl.pallas_call(
        functools.partial(_mlp_kernel, seq_len=seq_len),
        grid=(t // rows,),
        in_specs=[row_spec(ya), row_spec(yb), row_spec(yc), row_spec(x),
                  _mod_spec(mods, layer, lambda i: mod_row(i // tiles_per_seq)), _layer_spec(g, layer)]
                 + [_layer_spec(w, layer, resident=True) for w in weights],
        out_specs=row_spec(x),
        out_shape=jax.ShapeDtypeStruct((t, d), F32),
        scratch_shapes=[pltpu.VMEM((rows, d), BF16), pltpu.VMEM((rows, down.shape[1]), BF16)],
        compiler_params=_cparams(("parallel",)),
        name="out_proj_mlp",
    )(ya, yb, yc, x, mods, g, *weights)


def _rope_tables(l):
    rows = l // GRID_W
    t_row = jnp.repeat(jnp.arange(rows, dtype=F32), GRID_W)
    t_col = jnp.tile(jnp.arange(GRID_W, dtype=F32), rows)
    inv = ROPE_BASE ** (-jnp.arange(ROPE_PAIRS, dtype=F32) / ROPE_PAIRS)
    ar, ac = t_row[:, None] * inv, t_col[:, None] * inv
    cos = jnp.concatenate([jnp.cos(ar), jnp.cos(ar), jnp.cos(ac), jnp.cos(ac)], axis=-1)
    sin = jnp.concatenate([-jnp.sin(ar), jnp.sin(ar), -jnp.sin(ac), jnp.sin(ac)], axis=-1)
    return jnp.tile(cos, (1, 2)), jnp.tile(sin, (1, 2))


def kernel(x_prompt, x_sample, c, cache_k, cache_v, state_ret_fwd, state_ret_bwd, c_ctx, norm1, w_mod, b_mod,
           w_in, sgu_norm, sgu_w, sgu_b, ret_logit_fwd, ret_logit_bwd, ret_norm, q_norm, k_norm, diff_lam,
           diff_norm, w_out, norm2, ffn_up, ffn_conv, ffn_conv_b, ffn_down):
    depth, d_model, _ = w_mod.shape
    batch, seq, _ = x_prompt.shape
    dec_batch, dec_seq, _ = x_sample.shape
    w_a = sgu_norm.shape[1]
    w_b = H_B * ret_norm.shape[2]
    w_c = 2 * H_C * HD_C
    past = cache_k.shape[2]

    pad = (-(dec_batch + 1)) % 8
    cond = jnp.concatenate([c, c_ctx[None, :], jnp.zeros((pad, d_model), F32)], axis=0)
    mods = _modulation(cond, w_mod, b_mod).reshape(depth, cond.shape[0], 6, d_model)

    w_in_b = w_in.astype(BF16)
    w_out_b = w_out.astype(BF16)
    sgu_w_b = sgu_w.astype(BF16)
    up_b = ffn_up.astype(BF16)
    down_b = ffn_down.astype(BF16)
    row3 = lambda a: a.reshape(depth, 1, -1)
    sgu_bias = jnp.repeat(jnp.swapaxes(sgu_b, 1, 2), w_a // G_A, axis=2)
    logits = jnp.stack([ret_logit_fwd, ret_logit_bwd], axis=1)
    lg_pair = jnp.repeat(logits, DK_B, axis=2).reshape(depth, 2 * (H_B // 2), LANES)
    lg_head = logits.reshape(depth, 2 * H_B, 1)
    qg = row3(jnp.tile(q_norm, (1, 2 * H_C)))
    kg = row3(jnp.tile(k_norm, (1, 2 * H_C)))
    norm1_r, norm2_r, sgu_g, ret_g, diff_g = row3(norm1), row3(norm2), row3(sgu_norm), row3(ret_norm), row3(diff_norm)
    conv_b = row3(ffn_conv_b)
    rope = _rope_tables(dec_seq)
    cache_kt = jnp.transpose(cache_k, (0, 1, 3, 4, 5, 2)).reshape(dec_batch, depth, w_c, past)

    def layer(x, l, nb, sl, mod_row, is_sample, new_cache, new_states):
        lam_init = 0.8 - 0.6 * math.exp(-0.3 * l)
        outs = _projection(x, mods, mod_row, l, sl, norm1_r, w_in_b, sgu_g, sgu_w_b, sgu_bias, qg, kg,
                           rope if is_sample else None, new_cache)
        ya, qr, kr, vr, gr, qa, ka, va = outs[:8]
        seq3 = lambda a: a.reshape(nb, sl, a.shape[1])
        state = (state_ret_fwd, state_ret_bwd) if is_sample else None
        yb, *states = _mixer_b(seq3(qr), seq3(kr), seq3(vr), seq3(gr), l, lg_pair, lg_head, ret_g, state,
                               new_states)
        cache = (cache_kt, cache_v) if is_sample else None
        yc = _mixer_c(seq3(qa), seq3(ka), seq3(va), l, diff_lam, diff_g, lam_init, cache)
        x = _mlp(ya, yb.reshape(nb * sl, w_b), yc.reshape(nb * sl, w_c), x, mods, mod_row, l, sl, norm2_r,
                 w_out_b, up_b, ffn_conv, conv_b, down_b)
        return x, outs[8:], states

    y_prompt = x_prompt.reshape(batch * seq, d_model)
    new_cache, new_states = None, None
    for l in range(depth):
        y_prompt, new_cache, new_states = layer(y_prompt, l, batch, seq, lambda b: dec_batch, False,
                                                new_cache, new_states)

    y_sample = x_sample.reshape(dec_batch * dec_seq, d_model)
    for l in range(depth):
        y_sample, _, _ = layer(y_sample, l, dec_batch, dec_seq, lambda b: b, True, None, None)

    new_kt, new_v = new_cache
    new_k = jnp.transpose(new_kt.reshape(batch, depth, H_C, 2, HD_C, seq), (0, 1, 5, 2, 3, 4))
    return (y_prompt.reshape(batch, seq, d_model), y_sample.reshape(dec_batch, dec_seq, d_model),
            new_k, new_v, new_states[0], new_states[1])
```

```python
import functools
import math

import jax
import jax.numpy as jnp
from jax import lax
from jax.experimental import pallas as pl
from jax.experimental.pallas import tpu as pltpu

F32 = jnp.float32
BF16 = jnp.bfloat16

GRID_W = 64
CHUNK = 128
EPS = 1e-6
ROPE_BASE = 10000.0
G_A = 4
H_B = 4
DK_B = 64
H_C = 4
HD_C = 64
ROPE_PAIRS = HD_C // 4
LANES = 128
SUBLANES = 8
MXU_DIM = 256
MOD_COL_TILES = 4
PROJ_ROWS = 512
MLP_ROWS = 1024
ATTN_Q_ROWS = 1024
ATTN_KEY_BLOCK = 128
ATTN_SEQ_ROWS = 2048
RET_ROWS = 2048
RET_CHUNK = 256
RET_UNROLL = 4
VMEM_LIMIT = 60 * 1024 * 1024


def _cparams(sem):
    return pltpu.CompilerParams(dimension_semantics=sem, vmem_limit_bytes=VMEM_LIMIT)


def _const_spec(shape):
    n = len(shape)
    return pl.BlockSpec(shape, lambda *_: (0,) * n)


def _layer_spec(arr, layer, resident=False):
    n = arr.ndim - 1
    mode = dict(pipeline_mode=pl.Buffered(1)) if resident else {}
    return pl.BlockSpec((None,) + arr.shape[1:], lambda *_: (layer,) + (0,) * n, **mode)


def _silu(x):
    return x * jax.nn.sigmoid(x)


def _dot(a, b):
    return jnp.dot(a, b, preferred_element_type=F32)


def _dot_nt(a, b):
    return lax.dot_general(a, b, (((1,), (1,)), ((), ())), preferred_element_type=F32)


def _dot_tn(a, b):
    return lax.dot_general(a, b, (((0,), (0,)), ((), ())), preferred_element_type=F32)


def _lo_mask(rows):
    return lax.broadcasted_iota(jnp.int32, (rows, LANES), 1) < (LANES // 2)


def _split_halves(x):
    lo = _lo_mask(x.shape[0])
    zero = jnp.zeros_like(x)
    return jnp.concatenate([jnp.where(lo, x, zero), jnp.where(lo, zero, x)], axis=0)


def _group64_mean(x2):
    rows, n = x2.shape
    lo = _lo_mask(rows)
    outs = []
    for j in range(n // LANES):
        blk = x2[:, j * LANES:(j + 1) * LANES]
        s_lo = jnp.sum(jnp.where(lo, blk, 0.0), axis=-1, keepdims=True)
        s_hi = jnp.sum(jnp.where(lo, 0.0, blk), axis=-1, keepdims=True)
        outs.append(jnp.where(lo, s_lo, s_hi))
    out = outs[0] if len(outs) == 1 else jnp.concatenate(outs, axis=-1)
    return out * (1.0 / (LANES // 2))


def _group64_rmsnorm(x, g):
    return x * lax.rsqrt(_group64_mean(x * x) + EPS) * g


def _rmsnorm(x, g):
    return x * lax.rsqrt(jnp.mean(x * x, axis=-1, keepdims=True) + EPS) * g


def _rope(x, cos, sin):
    rows, n = x.shape
    lane = lax.broadcasted_iota(jnp.int32, (rows, LANES), 1)
    first = (lane % (2 * ROPE_PAIRS)) < ROPE_PAIRS
    outs = []
    for j in range(n // LANES):
        blk = x[:, j * LANES:(j + 1) * LANES]
        partner = jnp.where(first, pltpu.roll(blk, LANES - ROPE_PAIRS, 1), pltpu.roll(blk, ROPE_PAIRS, 1))
        outs.append(blk * cos + partner * sin)
    return jnp.concatenate(outs, axis=-1)


def _mod_kernel(cond_ref, w_ref, b_ref, o_ref):
    s = _silu(cond_ref[...]).astype(BF16)
    o_ref[0] = _dot(s, w_ref[0].astype(BF16)) + b_ref[0]


def _modulation(cond, w_mod, b_mod):
    depth, d, n = w_mod.shape
    rows = cond.shape[0]
    tn = n // MOD_COL_TILES
    return pl.pallas_call(
        _mod_kernel,
        grid=(depth, n // tn),
        in_specs=[
            _const_spec((rows, d)),
            pl.BlockSpec((1, d, tn), lambda l, j: (l, 0, j)),
            pl.BlockSpec((1, 1, tn), lambda l, j: (l, 0, j)),
        ],
        out_specs=pl.BlockSpec((1, rows, tn), lambda l, j: (l, 0, j)),
        out_shape=jax.ShapeDtypeStruct((depth, rows, n), F32),
        compiler_params=_cparams(("parallel", "parallel")),
        name="adaln_mod",
    )(cond, w_mod, b_mod.reshape(depth, 1, n))


def _mod_spec(mods, layer, row_of_step):
    return pl.BlockSpec((None, 1) + mods.shape[2:], lambda i: (layer, row_of_step(i), 0, 0))


def _proj_kernel(*refs, has_rope, seq_len, n_prev):
    if has_rope:
        (x_ref, mod_ref, g_ref, w_ref, sg_ref, ws_ref, sb_ref, qg_ref, kg_ref, cos_ref, sin_ref,
         ya_ref, qr_ref, kr_ref, vr_ref, gr_ref, qa_ref, ka_ref, va_ref) = refs
    elif n_prev:
        (x_ref, mod_ref, g_ref, w_ref, sg_ref, ws_ref, sb_ref, qg_ref, kg_ref, kt_prev_ref, vf_prev_ref,
         ya_ref, qr_ref, kr_ref, vr_ref, gr_ref, qa_ref, ka_ref, va_ref, kt_ref, vf_ref) = refs
        kt_ref[:, :n_prev] = kt_prev_ref[...]
        vf_ref[:, :n_prev] = vf_prev_ref[...]
    else:
        (x_ref, mod_ref, g_ref, w_ref, sg_ref, ws_ref, sb_ref, qg_ref, kg_ref,
         ya_ref, qr_ref, kr_ref, vr_ref, gr_ref, qa_ref, ka_ref, va_ref, kt_ref, vf_ref) = refs
    rows = x_ref.shape[0]
    m = mod_ref[0]
    hb = (_rmsnorm(x_ref[...], g_ref[...]) * (1.0 + m[1:2, :]) + m[0:1, :]).astype(BF16)
    wa = ya_ref.shape[1]
    wb = qr_ref.shape[1]
    wc = qa_ref.shape[1]

    def proj(c0, width):
        return _dot(hb, w_ref[:, c0:c0 + width])

    def emit_q(z):
        qn = _group64_rmsnorm(z, qg_ref[...])
        if has_rope:
            qn = _rope(qn, cos_ref[...], sin_ref[...])
        qa_ref[...] = (qn * (HD_C ** -0.5 * math.log2(math.e))).astype(BF16)

    def emit_k(z):
        kn = _group64_rmsnorm(z, kg_ref[...])
        if has_rope:
            kn = _rope(kn, cos_ref[...], sin_ref[...])
        else:
            for s in range(rows // seq_len):
                kt_ref[s, n_prev] = kn[s * seq_len:(s + 1) * seq_len, :].T
        ka_ref[...] = kn.astype(BF16)

    def emit_v(z):
        if not has_rope:
            for s in range(rows // seq_len):
                zs = z[s * seq_len:(s + 1) * seq_len, :]
                heads = jnp.stack([zs[:, h * LANES:(h + 1) * LANES] for h in range(H_C)], axis=0)
                vf_ref[s, n_prev] = jnp.swapaxes(heads, 0, 1)
        va_ref[...] = z.astype(BF16)

    def sgu_prepare(z):
        za = jax.nn.gelu(z)
        return za[:, :wa], _rmsnorm(za[:, wa:], sg_ref[...]).astype(BF16)

    def sgu_mix(u, v):
        lane = lax.broadcasted_iota(jnp.int32, (CHUNK, wa), 1)
        dg = wa // G_A
        for c in range(rows // CHUNK):
            rs = slice(c * CHUNK, (c + 1) * CHUNK)
            s = _dot(ws_ref[G_A - 1], v[rs])
            for g in range(G_A - 2, -1, -1):
                s = jnp.where(lane < (g + 1) * dg, _dot(ws_ref[g], v[rs]), s)
            ya_ref[rs, :] = (u[rs] * (s + sb_ref[...])).astype(BF16)

    def emit_ret_qk(z):
        qr_ref[...] = z[:, :wb].astype(BF16)
        kr_ref[...] = z[:, wb:] * (DK_B ** -0.5)

    def emit_ret_vg(z):
        vr_ref[...] = z[:, :wb].astype(BF16)
        gr_ref[...] = _silu(z[:, wb:])

    cb = 2 * wa
    cc = cb + 4 * wb
    z_a = proj(0, 2 * wa)
    z_q = proj(cc, wc)
    u, v = sgu_prepare(z_a)
    z_k = proj(cc + wc, wc)
    sgu_mix(u, v)
    emit_q(z_q)
    z_v = proj(cc + 2 * wc, wc)
    emit_k(z_k)
    z_r = proj(cb, 2 * wb)
    emit_v(z_v)
    z_g = proj(cb + 2 * wb, 2 * wb)
    emit_ret_qk(z_r)
    emit_ret_vg(z_g)


def _stacked_spec(lead, layers, tail):
    return pl.BlockSpec((lead, layers) + tail, lambda i: (i, 0) + (0,) * len(tail))


def _projection(x, mods, mod_row, layer, seq_len, g, w_bf16, sgu_g, sgu_w, sgu_bias, qg, kg, rope, prev_cache):
    t, d = x.shape
    rows = 2 * PROJ_ROWS if rope is not None else PROJ_ROWS
    wa = sgu_g.shape[2]
    wb = wa
    wc = qg.shape[2]
    has_rope = rope is not None
    tiles_per_seq = max(1, seq_len // rows)
    row_spec = lambda w: pl.BlockSpec((rows, w), lambda i: (i, 0))
    params = [g, w_bf16, sgu_g, sgu_w, sgu_bias, qg, kg]
    in_specs = [row_spec(d), _mod_spec(mods, layer, lambda i: mod_row(i // tiles_per_seq))]
    in_specs += [_layer_spec(p, layer) for p in params]
    args = [x, mods] + params
    outs = [(wa, BF16), (wb, BF16), (wb, F32), (wb, BF16), (wb, F32), (wc, BF16), (wc, BF16), (wc, BF16)]
    out_specs = [row_spec(w) for w, _ in outs]
    out_shape = [jax.ShapeDtypeStruct((t, w), dt) for w, dt in outs]
    if has_rope:
        tab_spec = pl.BlockSpec((rows, LANES), lambda i: (i % tiles_per_seq, 0))
        in_specs += [tab_spec, tab_spec]
        args += list(rope)
    else:
        seqs = rows // seq_len
        k_tail, v_tail = (wc, seq_len), (seq_len, H_C, LANES)
        if layer:
            in_specs += [_stacked_spec(seqs, layer, k_tail), _stacked_spec(seqs, layer, v_tail)]
            args += list(prev_cache)
        out_specs += [_stacked_spec(seqs, layer + 1, k_tail), _stacked_spec(seqs, layer + 1, v_tail)]
        out_shape += [jax.ShapeDtypeStruct((t // seq_len, layer + 1) + k_tail, F32),
                      jax.ShapeDtypeStruct((t // seq_len, layer + 1) + v_tail, F32)]
    return pl.pallas_call(
        functools.partial(_proj_kernel, has_rope=has_rope, seq_len=seq_len, n_prev=0 if has_rope else layer),
        grid=(t // rows,),
        in_specs=in_specs,
        out_specs=out_specs,
        out_shape=out_shape,
        compiler_params=_cparams(("parallel",)),
        name="projection",
    )(*args)


def _log_sigmoid(x):
    y = -x
    return -(jnp.maximum(y, 0.0) + jnp.log1p(jnp.exp(-jnp.abs(y))))


def _ret_kernel(*refs, chunk, n_chunks, has_state, n_prev):
    if has_state:
        (q_ref, k_ref, v_ref, gate_ref, r0f_ref, r0b_ref, lgp_ref, lgh_ref, norm_ref,
         y_ref, d_scr, dec_scr, kv_scr, rs_scr, o_scr) = refs
        state_outs = None
    elif n_prev:
        (q_ref, k_ref, v_ref, gate_ref, rf_prev_ref, rb_prev_ref, lgp_ref, lgh_ref, norm_ref,
         y_ref, rf_ref, rb_ref, d_scr, dec_scr, kv_scr, rs_scr, o_scr) = refs
        rf_ref[:, :n_prev] = rf_prev_ref[...]
        rb_ref[:, :n_prev] = rb_prev_ref[...]
        state_outs = (rf_ref, rb_ref)
    else:
        (q_ref, k_ref, v_ref, gate_ref, lgp_ref, lgh_ref, norm_ref,
         y_ref, rf_ref, rb_ref, d_scr, dec_scr, kv_scr, rs_scr, o_scr) = refs
        state_outs = (rf_ref, rb_ref)
    bt = q_ref.shape[0]
    n_pairs = H_B // 2
    half = LANES // 2
    cs = chunk

    @pl.when(pl.program_id(0) == 0)
    def _build_decay_tables():
        lgp = _log_sigmoid(lgp_ref[...])
        lgh = _log_sigmoid(lgh_ref[...])
        dist = (lax.broadcasted_iota(jnp.int32, (cs, cs), 0) - lax.broadcasted_iota(jnp.int32, (cs, cs), 1)).astype(F32)
        for h in range(H_B):
            lf = lgh[h:h + 1, :]
            lb = lgh[H_B + h:H_B + h + 1, :]
            d_f = jnp.where(dist >= 0, jnp.exp(lf * jnp.maximum(dist, 0.0)), 0.0)
            d_b = jnp.where(dist <= 0, jnp.exp(lb * jnp.maximum(-dist, 0.0)), 0.0)
            d_scr[h // 2, (h % 2) * cs:(h % 2 + 1) * cs, :] = d_f + d_b
        pos = lax.broadcasted_iota(jnp.int32, (cs, LANES), 0).astype(F32)
        for p in range(n_pairs):
            lf = lgp[p:p + 1, :]
            lb = lgp[n_pairs + p:n_pairs + p + 1, :]
            dec_scr[0, p] = jnp.exp(lf * (pos + 1.0))
            dec_scr[1, p] = jnp.exp(lf * (cs - 1.0 - pos))
            dec_scr[2, p] = jnp.exp(lb * (cs - pos))
            dec_scr[3, p] = jnp.exp(lb * pos)
            dec_scr[4, p] = jnp.broadcast_to(jnp.exp(lf * float(cs)), (cs, LANES))
            dec_scr[5, p] = jnp.broadcast_to(jnp.exp(lb * float(cs)), (cs, LANES))

    lo = _lo_mask(cs)
    blockdiag = ((lax.broadcasted_iota(jnp.int32, (LANES, LANES), 0) < half)
                 == (lax.broadcasted_iota(jnp.int32, (LANES, LANES), 1) < half))

    for b in range(bt):
        def intra(c, carry):
            rows = pl.ds(pl.multiple_of(c * cs, cs), cs)
            qc, kc, vc = q_ref[b, rows, :], k_ref[b, rows, :], v_ref[b, rows, :]
            for p in range(n_pairs):
                sl = slice(p * LANES, (p + 1) * LANES)
                k128, vb = kc[:, sl], vc[:, sl]
                inner = (_dot_nt(_split_halves(qc[:, sl]), k128.astype(BF16)) * d_scr[p]).astype(BF16)
                oo = _dot(inner, vb)
                o_scr[rows, sl] = jnp.where(lo, oo[:cs], oo[cs:])
                kk = jnp.concatenate([k128 * dec_scr[1, p], k128 * dec_scr[3, p]], axis=1).astype(BF16)
                kv = _dot_tn(kk, vb)
                kv_scr[0, c, p] = jnp.where(blockdiag, kv[:LANES], 0.0)
                kv_scr[1, c, p] = jnp.where(blockdiag, kv[LANES:], 0.0)
            return carry

        lax.fori_loop(0, n_chunks, intra, 0, unroll=min(n_chunks, RET_UNROLL))

        for p in range(n_pairs):
            for d in range(2):
                if has_state:
                    r0_ref = (r0f_ref, r0b_ref)[d]
                    zero = jnp.zeros((half, half), F32)
                    top = jnp.concatenate([r0_ref[b, 2 * p], zero], axis=1)
                    bot = jnp.concatenate([zero, r0_ref[b, 2 * p + 1]], axis=1)
                    r = jnp.concatenate([top, bot], axis=0)
                else:
                    r = jnp.zeros((LANES, LANES), F32)
                order = range(n_chunks) if d == 0 else range(n_chunks - 1, -1, -1)
                for c in order:
                    rs_scr[c, p, :, d * LANES:(d + 1) * LANES] = r.astype(BF16)
                    r = r * dec_scr[4 + d, p, :LANES, :] + kv_scr[d, c, p]
                if state_outs is not None:
                    state_outs[d][b, n_prev, 2 * p] = r[:half, :half]
                    state_outs[d][b, n_prev, 2 * p + 1] = r[half:, half:]

        def cross(c, carry):
            rows = pl.ds(pl.multiple_of(c * cs, cs), cs)
            qc = q_ref[b, rows, :]
            outs = []
            for p in range(n_pairs):
                sl = slice(p * LANES, (p + 1) * LANES)
                t = _dot(qc[:, sl], rs_scr[c, p])
                outs.append(o_scr[rows, sl] + t[:, :LANES] * dec_scr[0, p] + t[:, LANES:] * dec_scr[2, p])
            o = _group64_rmsnorm(jnp.concatenate(outs, axis=-1), norm_ref[...])
            y_ref[b, rows, :] = (gate_ref[b, rows, :] * o).astype(BF16)
            return carry

        lax.fori_loop(0, n_chunks, cross, 0, unroll=min(n_chunks, RET_UNROLL))


def _mixer_b(q, k, v, gate, layer, lg_pair, lg_head, norm, state, prev_states):
    b, l, wb = q.shape
    bt = max(1, RET_ROWS // l)
    chunk = min(l, RET_CHUNK)
    n_chunks = l // chunk
    has_state = state is not None
    tile = pl.BlockSpec((bt, l, wb), lambda i: (i, 0, 0))
    in_specs = [tile, tile, tile, tile]
    args = [q, k, v, gate]
    out_specs = [tile]
    out_shape = [jax.ShapeDtypeStruct((b, l, wb), BF16)]
    s_tail = (H_B, DK_B, DK_B)
    if has_state:
        sspec = pl.BlockSpec((bt, None) + s_tail, lambda i: (i, layer, 0, 0, 0))
        in_specs += [sspec, sspec]
        args += list(state)
    else:
        if layer:
            in_specs += [_stacked_spec(bt, layer, s_tail)] * 2
            args += list(prev_states)
        out_specs += [_stacked_spec(bt, layer + 1, s_tail)] * 2
        out_shape += [jax.ShapeDtypeStruct((b, layer + 1) + s_tail, F32)] * 2
    params = [lg_pair, lg_head, norm]
    in_specs += [_layer_spec(p, layer) for p in params]
    args += params
    n_pairs = H_B // 2
    return pl.pallas_call(
        functools.partial(_ret_kernel, chunk=chunk, n_chunks=n_chunks, has_state=has_state,
                          n_prev=0 if has_state else layer),
        grid=(b // bt,),
        in_specs=in_specs,
        out_specs=out_specs,
        out_shape=out_shape,
        scratch_shapes=[
            pltpu.VMEM((n_pairs, 2 * chunk, chunk), F32),
            pltpu.VMEM((6, n_pairs, chunk, LANES), F32),
            pltpu.VMEM((2, n_chunks, n_pairs, LANES, LANES), F32),
            pltpu.VMEM((n_chunks, n_pairs, LANES, 2 * LANES), BF16),
            pltpu.VMEM((l, wb), F32),
        ],
        compiler_params=_cparams(("arbitrary",)),
        name="retention",
    )(*args)


def _diff_softmax_t(st, lam, tq):
    n_keys = st.shape[0]
    st4 = st.reshape(n_keys // ATTN_KEY_BLOCK, ATTN_KEY_BLOCK // SUBLANES, SUBLANES, 2 * tq)
    m_loc = jnp.max(st4, axis=1)
    e4 = jnp.exp2(st4 - m_loc[:, None])
    l_loc = jnp.sum(e4, axis=1)
    m = jnp.max(jnp.max(m_loc, axis=0), axis=0, keepdims=True)
    s_loc = jnp.exp2(m_loc - m)
    l = jnp.sum(jnp.sum(l_loc * s_loc, axis=0), axis=0, keepdims=True)
    coef = jnp.concatenate([1.0 / l[:, :tq], lam / l[:, tq:]], axis=1)
    w = e4 * (coef * s_loc)[:, None]
    return (w[..., :tq] - w[..., tq:]).reshape(n_keys, tq).astype(BF16)


def _attn_kernel(*refs, has_cache, lam_init):
    if has_cache:
        (q_ref, k_ref, v_ref, ck_ref, cv_ref, dl_ref, dn_ref, y_ref, k_scr, v_scr) = refs
        n_past = ck_ref.shape[1]

        @pl.when(pl.program_id(1) == 0)
        def _gather_keys():
            k_scr[...] = ck_ref[...].T.astype(BF16)
            cv = jnp.swapaxes(cv_ref[...], 0, 1)
            for h in range(H_C):
                v_scr[:, h * LANES:(h + 1) * LANES] = cv[h].astype(BF16)

        keys = lambda b, sl: jnp.concatenate([k_scr[:, sl], k_ref[b, :, sl]], axis=0)
        vals = lambda b, sl: jnp.concatenate([v_scr[:, sl], v_ref[b, :, sl]], axis=0)
    else:
        (q_ref, k_ref, v_ref, dl_ref, dn_ref, y_ref) = refs
        keys = lambda b, sl: k_ref[b, :, sl]
        vals = lambda b, sl: v_ref[b, :, sl]
    bt, tq, _ = q_ref.shape
    dl = dl_ref[...]
    lam = (jnp.exp(jnp.sum(dl[0:1] * dl[1:2], axis=-1, keepdims=True))
           - jnp.exp(jnp.sum(dl[2:3] * dl[3:4], axis=-1, keepdims=True)) + lam_init)
    units = [(b, slice(h * LANES, (h + 1) * LANES)) for b in range(bt) for h in range(H_C)]
    scores = lambda b, sl: _dot_nt(keys(b, sl), _split_halves(q_ref[b, :, sl]))
    st_next = scores(*units[0])
    for i, (b, sl) in enumerate(units):
        st = st_next
        if i + 1 < len(units):
            st_next = scores(*units[i + 1])
        o = _dot_tn(_diff_softmax_t(st, lam, tq), vals(b, sl))
        y_ref[b, :, sl] = (_rmsnorm(o, dn_ref[...]) * (1.0 - lam_init)).astype(BF16)


def _mixer_c(q, k, v, layer, dl, dn, lam_init, cache):
    b, l, wc = q.shape
    has_cache = cache is not None
    tq = min(l, ATTN_Q_ROWS)
    bt = 1 if has_cache else max(1, ATTN_SEQ_ROWS // l)
    q_spec = pl.BlockSpec((bt, tq, wc), lambda i, j: (i, j, 0))
    kv_spec = pl.BlockSpec((bt, l, wc), lambda i, j: (i, 0, 0))
    in_specs = [q_spec, kv_spec, kv_spec]
    args = [q, k, v]
    n_keys = l
    if has_cache:
        ck_t, cv = cache
        n_past = cv.shape[2]
        n_keys += n_past
        in_specs += [pl.BlockSpec((None, None, wc, n_past), lambda i, j: (i, layer, 0, 0)),
                     pl.BlockSpec((None, None, n_past, H_C, LANES), lambda i, j: (i, layer, 0, 0, 0))]
        args += [ck_t, cv]
    in_specs += [_layer_spec(dl, layer), _layer_spec(dn, layer)]
    args += [dl, dn]
    scratch = [pltpu.VMEM((n_past, wc), BF16), pltpu.VMEM((n_past, wc), BF16)] if has_cache else []
    return pl.pallas_call(
        functools.partial(_attn_kernel, has_cache=has_cache, lam_init=lam_init),
        grid=(b // bt, l // tq),
        in_specs=in_specs,
        out_specs=q_spec,
        out_shape=jax.ShapeDtypeStruct((b, l, wc), BF16),
        scratch_shapes=scratch,
        compiler_params=_cparams(("parallel", "arbitrary")),
        name="diff_attention",
    )(*args)


def _mlp_kernel(ya_ref, yb_ref, yc_ref, x_ref, mod_ref, g_ref, wo_ref, up_ref, cw_ref, cb_ref, down_ref,
                o_ref, h_scr, g_scr, *, seq_len):
    rows = x_ref.shape[0]
    m = mod_ref[0]
    y = None
    k0 = 0
    for y_ref in (ya_ref, yb_ref, yc_ref):
        kw = y_ref.shape[1]
        part = _dot(y_ref[...], wo_ref[k0:k0 + kw, :])
        y = part if y is None else y + part
        k0 += kw
    x1 = x_ref[...] + m[2:3, :] * y
    o_ref[...] = x1
    h_scr[...] = (_rmsnorm(x1, g_ref[...]) * (1.0 + m[4:5, :]) + m[3:4, :]).astype(BF16)

    d_ff = down_ref.shape[0]
    fc = MXU_DIM
    pos = lax.broadcasted_iota(jnp.int32, (rows, fc), 0) % seq_len
    has_prev = pos > 0
    has_next = pos < seq_len - 1

    def conv(c0):
        u = _dot(h_scr[...], up_ref[:, c0:c0 + fc])
        w = cw_ref[:, c0:c0 + fc]
        prev = jnp.where(has_prev, pltpu.roll(u, 1, 0), 0.0)
        nxt = jnp.where(has_next, pltpu.roll(u, rows - 1, 0), 0.0)
        return prev * w[0:1, :] + u * w[1:2, :] + nxt * w[2:3, :] + cb_ref[:, c0:c0 + fc]

    for j in range(d_ff // fc):
        a = conv(j * fc)
        b = conv(d_ff + j * fc)
        g_scr[:, j * fc:(j + 1) * fc] = (_silu(a) * b).astype(BF16)

    o_ref[...] = o_ref[...] + m[5:6, :] * _dot(g_scr[...], down_ref[...])


def _mlp(ya, yb, yc, x, mods, mod_row, layer, seq_len, g, w_out, up, cw, cb, down):
    t, d = x.shape
    rows = MLP_ROWS
    tiles_per_seq = max(1, seq_len // rows)
    row_spec = lambda a: pl.BlockSpec((rows, a.shape[1]), lambda i: (i, 0))
    weights = [w_out, up, cw, cb, down]
    return pl.pallas_call(
        functools.partial(_mlp_kernel, seq_len=seq_len),
        grid=(t // rows,),
        in_specs=[row_spec(ya), row_spec(yb), row_spec(yc), row_spec(x),
                  _mod_spec(mods, layer, lambda i: mod_row(i // tiles_per_seq)), _layer_spec(g, layer)]
                 + [_layer_spec(w, layer, resident=True) for w in weights],
        out_specs=row_spec(x),
        out_shape=jax.ShapeDtypeStruct((t, d), F32),
        scratch_shapes=[pltpu.VMEM((rows, d), BF16), pltpu.VMEM((rows, down.shape[1]), BF16)],
        compiler_params=_cparams(("parallel",)),
        name="out_proj_mlp",
    )(ya, yb, yc, x, mods, g, *weights)


def _rope_tables(l):
    rows = l // GRID_W
    t_row = jnp.repeat(jnp.arange(rows, dtype=F32), GRID_W)
    t_col = jnp.tile(jnp.arange(GRID_W, dtype=F32), rows)
    inv = ROPE_BASE ** (-jnp.arange(ROPE_PAIRS, dtype=F32) / ROPE_PAIRS)
    ar, ac = t_row[:, None] * inv, t_col[:, None] * inv
    cos = jnp.concatenate([jnp.cos(ar), jnp.cos(ar), jnp.cos(ac), jnp.cos(ac)], axis=-1)
    sin = jnp.concatenate([-jnp.sin(ar), jnp.sin(ar), -jnp.sin(ac), jnp.sin(ac)], axis=-1)
    return jnp.tile(cos, (1, 2)), jnp.tile(sin, (1, 2))


def kernel(x_prompt, x_sample, c, cache_k, cache_v, state_ret_fwd, state_ret_bwd, c_ctx, norm1, w_mod, b_mod,
           w_in, sgu_norm, sgu_w, sgu_b, ret_logit_fwd, ret_logit_bwd, ret_norm, q_norm, k_norm, diff_lam,
           diff_norm, w_out, norm2, ffn_up, ffn_conv, ffn_conv_b, ffn_down):
    depth, d_model, _ = w_mod.shape
    batch, seq, _ = x_prompt.shape
    dec_batch, dec_seq, _ = x_sample.shape
    w_a = sgu_norm.shape[1]
    w_b = H_B * ret_norm.shape[2]
    w_c = 2 * H_C * HD_C
    past = cache_k.shape[2]

    pad = (-(dec_batch + 1)) % 8
    cond = jnp.concatenate([c, c_ctx[None, :], jnp.zeros((pad, d_model), F32)], axis=0)
    mods = _modulation(cond, w_mod, b_mod).reshape(depth, cond.shape[0], 6, d_model)

    w_in_b = w_in.astype(BF16)
    w_out_b = w_out.astype(BF16)
    sgu_w_b = sgu_w.astype(BF16)
    up_b = ffn_up.astype(BF16)
    down_b = ffn_down.astype(BF16)
    mods, up_b, down_b = lax.optimization_barrier((mods, up_b, down_b))
    row3 = lambda a: a.reshape(depth, 1, -1)
    sgu_bias = jnp.repeat(jnp.swapaxes(sgu_b, 1, 2), w_a // G_A, axis=2)
    logits = jnp.stack([ret_logit_fwd, ret_logit_bwd], axis=1)
    lg_pair = jnp.repeat(logits, DK_B, axis=2).reshape(depth, 2 * (H_B // 2), LANES)
    lg_head = logits.reshape(depth, 2 * H_B, 1)
    qg = row3(jnp.tile(q_norm, (1, 2 * H_C)))
    kg = row3(jnp.tile(k_norm, (1, 2 * H_C)))
    norm1_r, norm2_r, sgu_g, ret_g, diff_g = row3(norm1), row3(norm2), row3(sgu_norm), row3(ret_norm), row3(diff_norm)
    conv_b = row3(ffn_conv_b)
    rope = _rope_tables(dec_seq)
    cache_kt = jnp.transpose(cache_k, (0, 1, 3, 4, 5, 2)).reshape(dec_batch, depth, w_c, past)

    def layer(x, l, nb, sl, mod_row, is_sample, new_cache, new_states):
        lam_init = 0.8 - 0.6 * math.exp(-0.3 * l)
        outs = _projection(x, mods, mod_row, l, sl, norm1_r, w_in_b, sgu_g, sgu_w_b, sgu_bias, qg, kg,
                           rope if is_sample else None, new_cache)
        ya, qr, kr, vr, gr, qa, ka, va = outs[:8]
        seq3 = lambda a: a.reshape(nb, sl, a.shape[1])
        state = (state_ret_fwd, state_ret_bwd) if is_sample else None
        yb, *states = _mixer_b(seq3(qr), seq3(kr), seq3(vr), seq3(gr), l, lg_pair, lg_head, ret_g, state,
                               new_states)
        cache = (cache_kt, cache_v) if is_sample else None
        yc = _mixer_c(seq3(qa), seq3(ka), seq3(va), l, diff_lam, diff_g, lam_init, cache)
        x = _mlp(ya, yb.reshape(nb * sl, w_b), yc.reshape(nb * sl, w_c), x, mods, mod_row, l, sl, norm2_r,
                 w_out_b, up_b, ffn_conv, conv_b, down_b)
        return x, outs[8:], states

    y_prompt = x_prompt.reshape(batch * seq, d_model)
    new_cache, new_states = None, None
    for l in range(depth):
        y_prompt, new_cache, new_states = layer(y_prompt, l, batch, seq, lambda b: dec_batch, False,
                                                new_cache, new_states)

    y_sample = x_sample.reshape(dec_batch * dec_seq, d_model)
    for l in range(depth):
        y_sample, _, _ = layer(y_sample, l, dec_batch, dec_seq, lambda b: b, True, None, None)

    new_kt, new_v = new_cache
    new_k = jnp.transpose(new_kt.reshape(batch, depth, H_C, 2, HD_C, seq), (0, 1, 5, 2, 3, 4))
    return (y_prompt.reshape(batch, seq, d_model), y_sample.reshape(dec_batch, dec_seq, d_model),
            new_k, new_v, new_states[0], new_states[1])
```

```python
import functools
import math

import jax
import jax.numpy as jnp
from jax import lax
from jax.experimental import pallas as pl
from jax.experimental.pallas import tpu as pltpu

F32 = jnp.float32
BF16 = jnp.bfloat16

GRID_W = 64
CHUNK = 128
EPS = 1e-6
ROPE_BASE = 10000.0
G_A = 4
H_B = 4
DK_B = 64
H_C = 4
HD_C = 64
ROPE_PAIRS = HD_C // 4
LANES = 128
SUBLANES = 8
MXU_DIM = 256
MOD_COL_TILES = 4
PROJ_ROWS = 512
MLP_ROWS = 1024
ATTN_Q_ROWS = 1024
ATTN_KEY_BLOCK = 128
ATTN_SEQ_ROWS = 2048
RET_ROWS = 2048
RET_CHUNK = 256
RET_UNROLL = 4
VMEM_LIMIT = 60 * 1024 * 1024
MOD_VMEM_MIB = 32
PROJ_VMEM_MIB = 48
RET_VMEM_MIB = 24
ATTN_VMEM_MIB = 48


def _cparams(sem, vmem_mib=None):
    limit = VMEM_LIMIT if vmem_mib is None else vmem_mib * 1024 * 1024
    return pltpu.CompilerParams(dimension_semantics=sem, vmem_limit_bytes=limit)


def _const_spec(shape):
    n = len(shape)
    return pl.BlockSpec(shape, lambda *_: (0,) * n)


def _layer_spec(arr, layer, resident=False):
    n = arr.ndim - 1
    mode = dict(pipeline_mode=pl.Buffered(1)) if resident else {}
    return pl.BlockSpec((None,) + arr.shape[1:], lambda *_: (layer,) + (0,) * n, **mode)


def _silu(x):
    return x * jax.nn.sigmoid(x)


def _dot(a, b):
    return jnp.dot(a, b, preferred_element_type=F32)


def _dot_nt(a, b):
    return lax.dot_general(a, b, (((1,), (1,)), ((), ())), preferred_element_type=F32)


def _dot_tn(a, b):
    return lax.dot_general(a, b, (((0,), (0,)), ((), ())), preferred_element_type=F32)


def _lo_mask(rows):
    return lax.broadcasted_iota(jnp.int32, (rows, LANES), 1) < (LANES // 2)


def _split_halves(x):
    lo = _lo_mask(x.shape[0])
    zero = jnp.zeros_like(x)
    return jnp.concatenate([jnp.where(lo, x, zero), jnp.where(lo, zero, x)], axis=0)


def _group64_mean(x2):
    rows, n = x2.shape
    lo = _lo_mask(rows)
    outs = []
    for j in range(n // LANES):
        blk = x2[:, j * LANES:(j + 1) * LANES]
        s_lo = jnp.sum(jnp.where(lo, blk, 0.0), axis=-1, keepdims=True)
        s_hi = jnp.sum(jnp.where(lo, 0.0, blk), axis=-1, keepdims=True)
        outs.append(jnp.where(lo, s_lo, s_hi))
    out = outs[0] if len(outs) == 1 else jnp.concatenate(outs, axis=-1)
    return out * (1.0 / (LANES // 2))


def _group64_rmsnorm(x, g):
    return x * lax.rsqrt(_group64_mean(x * x) + EPS) * g


def _rmsnorm(x, g):
    return x * lax.rsqrt(jnp.mean(x * x, axis=-1, keepdims=True) + EPS) * g


def _rope(x, cos, sin):
    rows, n = x.shape
    lane = lax.broadcasted_iota(jnp.int32, (rows, LANES), 1)
    first = (lane % (2 * ROPE_PAIRS)) < ROPE_PAIRS
    outs = []
    for j in range(n // LANES):
        blk = x[:, j * LANES:(j + 1) * LANES]
        partner = jnp.where(first, pltpu.roll(blk, LANES - ROPE_PAIRS, 1), pltpu.roll(blk, ROPE_PAIRS, 1))
        outs.append(blk * cos + partner * sin)
    return jnp.concatenate(outs, axis=-1)


def _mod_kernel(cond_ref, w_ref, b_ref, o_ref):
    s = _silu(cond_ref[...]).astype(BF16)
    o_ref[0] = _dot(s, w_ref[0].astype(BF16)) + b_ref[0]


def _modulation(cond, w_mod, b_mod):
    depth, d, n = w_mod.shape
    rows = cond.shape[0]
    tn = n // MOD_COL_TILES
    return pl.pallas_call(
        _mod_kernel,
        grid=(depth, n // tn),
        in_specs=[
            _const_spec((rows, d)),
            pl.BlockSpec((1, d, tn), lambda l, j: (l, 0, j)),
            pl.BlockSpec((1, 1, tn), lambda l, j: (l, 0, j)),
        ],
        out_specs=pl.BlockSpec((1, rows, tn), lambda l, j: (l, 0, j)),
        out_shape=jax.ShapeDtypeStruct((depth, rows, n), F32),
        compiler_params=_cparams(("parallel", "parallel"), MOD_VMEM_MIB),
        name="adaln_mod",
    )(cond, w_mod, b_mod.reshape(depth, 1, n))


def _mod_spec(mods, layer, row_of_step):
    return pl.BlockSpec((None, 1) + mods.shape[2:], lambda i: (layer, row_of_step(i), 0, 0))


def _proj_kernel(*refs, has_rope, seq_len, n_prev):
    if has_rope:
        (x_ref, mod_ref, g_ref, w_ref, sg_ref, ws_ref, sb_ref, qg_ref, kg_ref, cos_ref, sin_ref,
         ya_ref, qr_ref, kr_ref, vr_ref, gr_ref, qa_ref, ka_ref, va_ref) = refs
    elif n_prev:
        (x_ref, mod_ref, g_ref, w_ref, sg_ref, ws_ref, sb_ref, qg_ref, kg_ref, kt_prev_ref, vf_prev_ref,
         ya_ref, qr_ref, kr_ref, vr_ref, gr_ref, qa_ref, ka_ref, va_ref, kt_ref, vf_ref) = refs
        kt_ref[:, :n_prev] = kt_prev_ref[...]
        vf_ref[:, :n_prev] = vf_prev_ref[...]
    else:
        (x_ref, mod_ref, g_ref, w_ref, sg_ref, ws_ref, sb_ref, qg_ref, kg_ref,
         ya_ref, qr_ref, kr_ref, vr_ref, gr_ref, qa_ref, ka_ref, va_ref, kt_ref, vf_ref) = refs
    rows = x_ref.shape[0]
    m = mod_ref[0]
    hb = (_rmsnorm(x_ref[...], g_ref[...]) * (1.0 + m[1:2, :]) + m[0:1, :]).astype(BF16)
    wa = ya_ref.shape[1]
    wb = qr_ref.shape[1]
    wc = qa_ref.shape[1]

    def proj(c0, width):
        return _dot(hb, w_ref[:, c0:c0 + width])

    def emit_q(z):
        qn = _group64_rmsnorm(z, qg_ref[...])
        if has_rope:
            qn = _rope(qn, cos_ref[...], sin_ref[...])
        qa_ref[...] = (qn * (HD_C ** -0.5 * math.log2(math.e))).astype(BF16)

    def emit_k(z):
        kn = _group64_rmsnorm(z, kg_ref[...])
        if has_rope:
            kn = _rope(kn, cos_ref[...], sin_ref[...])
        else:
            for s in range(rows // seq_len):
                kt_ref[s, n_prev] = kn[s * seq_len:(s + 1) * seq_len, :].T
        ka_ref[...] = kn.astype(BF16)

    def emit_v(z):
        if not has_rope:
            for s in range(rows // seq_len):
                zs = z[s * seq_len:(s + 1) * seq_len, :]
                heads = jnp.stack([zs[:, h * LANES:(h + 1) * LANES] for h in range(H_C)], axis=0)
                vf_ref[s, n_prev] = jnp.swapaxes(heads, 0, 1)
        va_ref[...] = z.astype(BF16)

    def sgu_prepare(z):
        za = jax.nn.gelu(z)
        return za[:, :wa], _rmsnorm(za[:, wa:], sg_ref[...]).astype(BF16)

    def sgu_mix(u, v):
        lane = lax.broadcasted_iota(jnp.int32, (CHUNK, wa), 1)
        dg = wa // G_A
        for c in range(rows // CHUNK):
            rs = slice(c * CHUNK, (c + 1) * CHUNK)
            s = _dot(ws_ref[G_A - 1], v[rs])
            for g in range(G_A - 2, -1, -1):
                s = jnp.where(lane < (g + 1) * dg, _dot(ws_ref[g], v[rs]), s)
            ya_ref[rs, :] = (u[rs] * (s + sb_ref[...])).astype(BF16)

    def emit_ret_qk(z):
        qr_ref[...] = z[:, :wb].astype(BF16)
        kr_ref[...] = z[:, wb:] * (DK_B ** -0.5)

    def emit_ret_vg(z):
        vr_ref[...] = z[:, :wb].astype(BF16)
        gr_ref[...] = _silu(z[:, wb:])

    cb = 2 * wa
    cc = cb + 4 * wb
    z_a = proj(0, 2 * wa)
    z_q = proj(cc, wc)
    u, v = sgu_prepare(z_a)
    z_k = proj(cc + wc, wc)
    sgu_mix(u, v)
    emit_q(z_q)
    z_v = proj(cc + 2 * wc, wc)
    emit_k(z_k)
    z_r = proj(cb, 2 * wb)
    emit_v(z_v)
    z_g = proj(cb + 2 * wb, 2 * wb)
    emit_ret_qk(z_r)
    emit_ret_vg(z_g)


def _stacked_spec(lead, layers, tail):
    return pl.BlockSpec((lead, layers) + tail, lambda i: (i, 0) + (0,) * len(tail))


def _projection(x, mods, mod_row, layer, seq_len, g, w_bf16, sgu_g, sgu_w, sgu_bias, qg, kg, rope, prev_cache):
    t, d = x.shape
    rows = 2 * PROJ_ROWS if rope is not None else PROJ_ROWS
    wa = sgu_g.shape[2]
    wb = wa
    wc = qg.shape[2]
    has_rope = rope is not None
    tiles_per_seq = max(1, seq_len // rows)
    row_spec = lambda w: pl.BlockSpec((rows, w), lambda i: (i, 0))
    params = [g, w_bf16, sgu_g, sgu_w, sgu_bias, qg, kg]
    in_specs = [row_spec(d), _mod_spec(mods, layer, lambda i: mod_row(i // tiles_per_seq))]
    in_specs += [_layer_spec(p, layer) for p in params]
    args = [x, mods] + params
    outs = [(wa, BF16), (wb, BF16), (wb, F32), (wb, BF16), (wb, F32), (wc, BF16), (wc, BF16), (wc, BF16)]
    out_specs = [row_spec(w) for w, _ in outs]
    out_shape = [jax.ShapeDtypeStruct((t, w), dt) for w, dt in outs]
    if has_rope:
        tab_spec = pl.BlockSpec((rows, LANES), lambda i: (i % tiles_per_seq, 0))
        in_specs += [tab_spec, tab_spec]
        args += list(rope)
    else:
        seqs = rows // seq_len
        k_tail, v_tail = (wc, seq_len), (seq_len, H_C, LANES)
        if layer:
            in_specs += [_stacked_spec(seqs, layer, k_tail), _stacked_spec(seqs, layer, v_tail)]
            args += list(prev_cache)
        out_specs += [_stacked_spec(seqs, layer + 1, k_tail), _stacked_spec(seqs, layer + 1, v_tail)]
        out_shape += [jax.ShapeDtypeStruct((t // seq_len, layer + 1) + k_tail, F32),
                      jax.ShapeDtypeStruct((t // seq_len, layer + 1) + v_tail, F32)]
    return pl.pallas_call(
        functools.partial(_proj_kernel, has_rope=has_rope, seq_len=seq_len, n_prev=0 if has_rope else layer),
        grid=(t // rows,),
        in_specs=in_specs,
        out_specs=out_specs,
        out_shape=out_shape,
        compiler_params=_cparams(("parallel",), PROJ_VMEM_MIB),
        name="projection",
    )(*args)


def _log_sigmoid(x):
    y = -x
    return -(jnp.maximum(y, 0.0) + jnp.log1p(jnp.exp(-jnp.abs(y))))


def _ret_kernel(*refs, chunk, n_chunks, has_state, n_prev):
    if has_state:
        (q_ref, k_ref, v_ref, gate_ref, r0f_ref, r0b_ref, lgp_ref, lgh_ref, norm_ref,
         y_ref, d_scr, dec_scr, kv_scr, rs_scr, o_scr) = refs
        state_outs = None
    elif n_prev:
        (q_ref, k_ref, v_ref, gate_ref, rf_prev_ref, rb_prev_ref, lgp_ref, lgh_ref, norm_ref,
         y_ref, rf_ref, rb_ref, d_scr, dec_scr, kv_scr, rs_scr, o_scr) = refs
        rf_ref[:, :n_prev] = rf_prev_ref[...]
        rb_ref[:, :n_prev] = rb_prev_ref[...]
        state_outs = (rf_ref, rb_ref)
    else:
        (q_ref, k_ref, v_ref, gate_ref, lgp_ref, lgh_ref, norm_ref,
         y_ref, rf_ref, rb_ref, d_scr, dec_scr, kv_scr, rs_scr, o_scr) = refs
        state_outs = (rf_ref, rb_ref)
    bt = q_ref.shape[0]
    n_pairs = H_B // 2
    half = LANES // 2
    cs = chunk

    @pl.when(pl.program_id(0) == 0)
    def _build_decay_tables():
        lgp = _log_sigmoid(lgp_ref[...])
        lgh = _log_sigmoid(lgh_ref[...])
        dist = (lax.broadcasted_iota(jnp.int32, (cs, cs), 0) - lax.broadcasted_iota(jnp.int32, (cs, cs), 1)).astype(F32)
        for h in range(H_B):
            lf = lgh[h:h + 1, :]
            lb = lgh[H_B + h:H_B + h + 1, :]
            d_f = jnp.where(dist >= 0, jnp.exp(lf * jnp.maximum(dist, 0.0)), 0.0)
            d_b = jnp.where(dist <= 0, jnp.exp(lb * jnp.maximum(-dist, 0.0)), 0.0)
            d_scr[h // 2, (h % 2) * cs:(h % 2 + 1) * cs, :] = d_f + d_b
        pos = lax.broadcasted_iota(jnp.int32, (cs, LANES), 0).astype(F32)
        for p in range(n_pairs):
            lf = lgp[p:p + 1, :]
            lb = lgp[n_pairs + p:n_pairs + p + 1, :]
            dec_scr[0, p] = jnp.exp(lf * (pos + 1.0))
            dec_scr[1, p] = jnp.exp(lf * (cs - 1.0 - pos))
            dec_scr[2, p] = jnp.exp(lb * (cs - pos))
            dec_scr[3, p] = jnp.exp(lb * pos)
            dec_scr[4, p] = jnp.broadcast_to(jnp.exp(lf * float(cs)), (cs, LANES))
            dec_scr[5, p] = jnp.broadcast_to(jnp.exp(lb * float(cs)), (cs, LANES))

    lo = _lo_mask(cs)
    blockdiag = ((lax.broadcasted_iota(jnp.int32, (LANES, LANES), 0) < half)
                 == (lax.broadcasted_iota(jnp.int32, (LANES, LANES), 1) < half))

    for b in range(bt):
        def intra(c, carry):
            rows = pl.ds(pl.multiple_of(c * cs, cs), cs)
            qc, kc, vc = q_ref[b, rows, :], k_ref[b, rows, :], v_ref[b, rows, :]
            for p in range(n_pairs):
                sl = slice(p * LANES, (p + 1) * LANES)
                k128, vb = kc[:, sl], vc[:, sl]
                inner = (_dot_nt(_split_halves(qc[:, sl]), k128.astype(BF16)) * d_scr[p]).astype(BF16)
                oo = _dot(inner, vb)
                o_scr[rows, sl] = jnp.where(lo, oo[:cs], oo[cs:])
                kk = jnp.concatenate([k128 * dec_scr[1, p], k128 * dec_scr[3, p]], axis=1).astype(BF16)
                kv = _dot_tn(kk, vb)
                kv_scr[0, c, p] = jnp.where(blockdiag, kv[:LANES], 0.0)
                kv_scr[1, c, p] = jnp.where(blockdiag, kv[LANES:], 0.0)
            return carry

        lax.fori_loop(0, n_chunks, intra, 0, unroll=min(n_chunks, RET_UNROLL))

        for p in range(n_pairs):
            for d in range(2):
                if has_state:
                    r0_ref = (r0f_ref, r0b_ref)[d]
                    zero = jnp.zeros((half, half), F32)
                    top = jnp.concatenate([r0_ref[b, 2 * p], zero], axis=1)
                    bot = jnp.concatenate([zero, r0_ref[b, 2 * p + 1]], axis=1)
                    r = jnp.concatenate([top, bot], axis=0)
                else:
                    r = jnp.zeros((LANES, LANES), F32)
                order = range(n_chunks) if d == 0 else range(n_chunks - 1, -1, -1)
                for c in order:
                    rs_scr[c, p, :, d * LANES:(d + 1) * LANES] = r.astype(BF16)
                    r = r * dec_scr[4 + d, p, :LANES, :] + kv_scr[d, c, p]
                if state_outs is not None:
                    state_outs[d][b, n_prev, 2 * p] = r[:half, :half]
                    state_outs[d][b, n_prev, 2 * p + 1] = r[half:, half:]

        def cross(c, carry):
            rows = pl.ds(pl.multiple_of(c * cs, cs), cs)
            qc = q_ref[b, rows, :]
            outs = []
            for p in range(n_pairs):
                sl = slice(p * LANES, (p + 1) * LANES)
                t = _dot(qc[:, sl], rs_scr[c, p])
                outs.append(o_scr[rows, sl] + t[:, :LANES] * dec_scr[0, p] + t[:, LANES:] * dec_scr[2, p])
            o = _group64_rmsnorm(jnp.concatenate(outs, axis=-1), norm_ref[...])
            y_ref[b, rows, :] = (gate_ref[b, rows, :] * o).astype(BF16)
            return carry

        lax.fori_loop(0, n_chunks, cross, 0, unroll=min(n_chunks, RET_UNROLL))


def _mixer_b(q, k, v, gate, layer, lg_pair, lg_head, norm, state, prev_states):
    b, l, wb = q.shape
    bt = max(1, RET_ROWS // l)
    chunk = min(l, RET_CHUNK)
    n_chunks = l // chunk
    has_state = state is not None
    tile = pl.BlockSpec((bt, l, wb), lambda i: (i, 0, 0))
    in_specs = [tile, tile, tile, tile]
    args = [q, k, v, gate]
    out_specs = [tile]
    out_shape = [jax.ShapeDtypeStruct((b, l, wb), BF16)]
    s_tail = (H_B, DK_B, DK_B)
    if has_state:
        sspec = pl.BlockSpec((bt, None) + s_tail, lambda i: (i, layer, 0, 0, 0))
        in_specs += [sspec, sspec]
        args += list(state)
    else:
        if layer:
            in_specs += [_stacked_spec(bt, layer, s_tail)] * 2
            args += list(prev_states)
        out_specs += [_stacked_spec(bt, layer + 1, s_tail)] * 2
        out_shape += [jax.ShapeDtypeStruct((b, layer + 1) + s_tail, F32)] * 2
    params = [lg_pair, lg_head, norm]
    in_specs += [_layer_spec(p, layer) for p in params]
    args += params
    n_pairs = H_B // 2
    return pl.pallas_call(
        functools.partial(_ret_kernel, chunk=chunk, n_chunks=n_chunks, has_state=has_state,
                          n_prev=0 if has_state else layer),
        grid=(b // bt,),
        in_specs=in_specs,
        out_specs=out_specs,
        out_shape=out_shape,
        scratch_shapes=[
            pltpu.VMEM((n_pairs, 2 * chunk, chunk), F32),
            pltpu.VMEM((6, n_pairs, chunk, LANES), F32),
            pltpu.VMEM((2, n_chunks, n_pairs, LANES, LANES), F32),
            pltpu.VMEM((n_chunks, n_pairs, LANES, 2 * LANES), BF16),
            pltpu.VMEM((l, wb), F32),
        ],
        compiler_params=_cparams(("arbitrary",), RET_VMEM_MIB),
        name="retention",
    )(*args)


def _diff_softmax_t(st, lam, tq):
    n_keys = st.shape[0]
    st4 = st.reshape(n_keys // ATTN_KEY_BLOCK, ATTN_KEY_BLOCK // SUBLANES, SUBLANES, 2 * tq)
    m_loc = jnp.max(st4, axis=1)
    e4 = jnp.exp2(st4 - m_loc[:, None])
    l_loc = jnp.sum(e4, axis=1)
    m = jnp.max(jnp.max(m_loc, axis=0), axis=0, keepdims=True)
    s_loc = jnp.exp2(m_loc - m)
    l = jnp.sum(jnp.sum(l_loc * s_loc, axis=0), axis=0, keepdims=True)
    coef = jnp.concatenate([1.0 / l[:, :tq], lam / l[:, tq:]], axis=1)
    w = e4 * (coef * s_loc)[:, None]
    return (w[..., :tq] - w[..., tq:]).reshape(n_keys, tq).astype(BF16)


def _attn_kernel(*refs, has_cache, lam_init):
    if has_cache:
        (q_ref, k_ref, v_ref, ck_ref, cv_ref, dl_ref, dn_ref, y_ref, k_scr, v_scr) = refs
        n_past = ck_ref.shape[1]

        @pl.when(pl.program_id(1) == 0)
        def _gather_keys():
            k_scr[...] = ck_ref[...].T.astype(BF16)
            cv = jnp.swapaxes(cv_ref[...], 0, 1)
            for h in range(H_C):
                v_scr[:, h * LANES:(h + 1) * LANES] = cv[h].astype(BF16)

        keys = lambda b, sl: jnp.concatenate([k_scr[:, sl], k_ref[b, :, sl]], axis=0)
        vals = lambda b, sl: jnp.concatenate([v_scr[:, sl], v_ref[b, :, sl]], axis=0)
    else:
        (q_ref, k_ref, v_ref, dl_ref, dn_ref, y_ref) = refs
        keys = lambda b, sl: k_ref[b, :, sl]
        vals = lambda b, sl: v_ref[b, :, sl]
    bt, tq, _ = q_ref.shape
    dl = dl_ref[...]
    lam = (jnp.exp(jnp.sum(dl[0:1] * dl[1:2], axis=-1, keepdims=True))
           - jnp.exp(jnp.sum(dl[2:3] * dl[3:4], axis=-1, keepdims=True)) + lam_init)
    units = [(b, slice(h * LANES, (h + 1) * LANES)) for b in range(bt) for h in range(H_C)]
    scores = lambda b, sl: _dot_nt(keys(b, sl), _split_halves(q_ref[b, :, sl]))
    st_next = scores(*units[0])
    for i, (b, sl) in enumerate(units):
        st = st_next
        if i + 1 < len(units):
            st_next = scores(*units[i + 1])
        o = _dot_tn(_diff_softmax_t(st, lam, tq), vals(b, sl))
        y_ref[b, :, sl] = (_rmsnorm(o, dn_ref[...]) * (1.0 - lam_init)).astype(BF16)


def _mixer_c(q, k, v, layer, dl, dn, lam_init, cache):
    b, l, wc = q.shape
    has_cache = cache is not None
    tq = min(l, ATTN_Q_ROWS)
    bt = 1 if has_cache else max(1, ATTN_SEQ_ROWS // l)
    q_spec = pl.BlockSpec((bt, tq, wc), lambda i, j: (i, j, 0))
    kv_spec = pl.BlockSpec((bt, l, wc), lambda i, j: (i, 0, 0))
    in_specs = [q_spec, kv_spec, kv_spec]
    args = [q, k, v]
    n_keys = l
    if has_cache:
        ck_t, cv = cache
        n_past = cv.shape[2]
        n_keys += n_past
        in_specs += [pl.BlockSpec((None, None, wc, n_past), lambda i, j: (i, layer, 0, 0)),
                     pl.BlockSpec((None, None, n_past, H_C, LANES), lambda i, j: (i, layer, 0, 0, 0))]
        args += [ck_t, cv]
    in_specs += [_layer_spec(dl, layer), _layer_spec(dn, layer)]
    args += [dl, dn]
    scratch = [pltpu.VMEM((n_past, wc), BF16), pltpu.VMEM((n_past, wc), BF16)] if has_cache else []
    return pl.pallas_call(
        functools.partial(_attn_kernel, has_cache=has_cache, lam_init=lam_init),
        grid=(b // bt, l // tq),
        in_specs=in_specs,
        out_specs=q_spec,
        out_shape=jax.ShapeDtypeStruct((b, l, wc), BF16),
        scratch_shapes=scratch,
        compiler_params=_cparams(("parallel", "arbitrary"), ATTN_VMEM_MIB),
        name="diff_attention",
    )(*args)


def _mlp_kernel(ya_ref, yb_ref, yc_ref, x_ref, mod_ref, g_ref, wo_ref, up_ref, cw_ref, cb_ref, down_ref,
                o_ref, h_scr, g_scr, *, seq_len):
    rows = x_ref.shape[0]
    m = mod_ref[0]
    y = None
    k0 = 0
    for y_ref in (ya_ref, yb_ref, yc_ref):
        kw = y_ref.shape[1]
        part = _dot(y_ref[...], wo_ref[k0:k0 + kw, :])
        y = part if y is None else y + part
        k0 += kw
    x1 = x_ref[...] + m[2:3, :] * y
    o_ref[...] = x1
    h_scr[...] = (_rmsnorm(x1, g_ref[...]) * (1.0 + m[4:5, :]) + m[3:4, :]).astype(BF16)

    d_ff = down_ref.shape[0]
    fc = MXU_DIM
    pos = lax.broadcasted_iota(jnp.int32, (rows, fc), 0) % seq_len
    has_prev = pos > 0
    has_next = pos < seq_len - 1

    def conv(c0):
        u = _dot(h_scr[...], up_ref[:, c0:c0 + fc])
        w = cw_ref[:, c0:c0 + fc]
        prev = jnp.where(has_prev, pltpu.roll(u, 1, 0), 0.0)
        nxt = jnp.where(has_next, pltpu.roll(u, rows - 1, 0), 0.0)
        return prev * w[0:1, :] + u * w[1:2, :] + nxt * w[2:3, :] + cb_ref[:, c0:c0 + fc]

    for j in range(d_ff // fc):
        a = conv(j * fc)
        b = conv(d_ff + j * fc)
        g_scr[:, j * fc:(j + 1) * fc] = (_silu(a) * b).astype(BF16)

    o_ref[...] = o_ref[...] + m[5:6, :] * _dot(g_scr[...], down_ref[...])


def _mlp(ya, yb, yc, x, mods, mod_row, layer, seq_len, g, w_out, up, cw, cb, down):
    t, d = x.shape
    rows = MLP_ROWS
    tiles_per_seq = max(1, seq_len // rows)
    row_spec = lambda a: pl.BlockSpec((rows, a.shape[1]), lambda i: (i, 0))
    weights = [w_out, up, cw, cb, down]
    return pl.pallas_call(
        functools.partial(_mlp_kernel, seq_len=seq_len),
        grid=(t // rows,),
        in_specs=[row_spec(ya), row_spec(yb), row_spec(yc), row_spec(x),
                  _mod_spec(mods, layer, lambda i: mod_row(i // tiles_per_seq)), _layer_spec(g, layer)]
                 + [_layer_spec(w, layer, resident=True) for w in weights],
        out_specs=row_spec(x),
        out_shape=jax.ShapeDtypeStruct((t, d), F32),
        scratch_shapes=[pltpu.VMEM((rows, d), BF16), pltpu.VMEM((rows, down.shape[1]), BF16)],
        compiler_params=_cparams(("parallel",)),
        name="out_proj_mlp",
    )(ya, yb, yc, x, mods, g, *weights)


def _rope_tables(l):
    rows = l // GRID_W
    t_row = jnp.repeat(jnp.arange(rows, dtype=F32), GRID_W)
    t_col = jnp.tile(jnp.arange(GRID_W, dtype=F32), rows)
    inv = ROPE_BASE ** (-jnp.arange(ROPE_PAIRS, dtype=F32) / ROPE_PAIRS)
    ar, ac = t_row[:, None] * inv, t_col[:, None] * inv
    cos = jnp.concatenate([jnp.cos(ar), jnp.cos(ar), jnp.cos(ac), jnp.cos(ac)], axis=-1)
    sin = jnp.concatenate([-jnp.sin(ar), jnp.sin(ar), -jnp.sin(ac), jnp.sin(ac)], axis=-1)
    return jnp.tile(cos, (1, 2)), jnp.tile(sin, (1, 2))


def kernel(x_prompt, x_sample, c, cache_k, cache_v, state_ret_fwd, state_ret_bwd, c_ctx, norm1, w_mod, b_mod,
           w_in, sgu_norm, sgu_w, sgu_b, ret_logit_fwd, ret_logit_bwd, ret_norm, q_norm, k_norm, diff_lam,
           diff_norm, w_out, norm2, ffn_up, ffn_conv, ffn_conv_b, ffn_down):
    depth, d_model, _ = w_mod.shape
    batch, seq, _ = x_prompt.shape
    dec_batch, dec_seq, _ = x_sample.shape
    w_a = sgu_norm.shape[1]
    w_b = H_B * ret_norm.shape[2]
    w_c = 2 * H_C * HD_C
    past = cache_k.shape[2]

    pad = (-(dec_batch + 1)) % 8
    cond = jnp.concatenate([c, c_ctx[None, :], jnp.zeros((pad, d_model), F32)], axis=0)
    mods = _modulation(cond, w_mod, b_mod).reshape(depth, cond.shape[0], 6, d_model)

    w_in_b = w_in.astype(BF16)
    w_out_b = w_out.astype(BF16)
    sgu_w_b = sgu_w.astype(BF16)
    up_b = ffn_up.astype(BF16)
    down_b = ffn_down.astype(BF16)
    row3 = lambda a: a.reshape(depth, 1, -1)
    sgu_bias = jnp.repeat(jnp.swapaxes(sgu_b, 1, 2), w_a // G_A, axis=2)
    logits = jnp.stack([ret_logit_fwd, ret_logit_bwd], axis=1)
    lg_pair = jnp.repeat(logits, DK_B, axis=2).reshape(depth, 2 * (H_B // 2), LANES)
    lg_head = logits.reshape(depth, 2 * H_B, 1)
    qg = row3(jnp.tile(q_norm, (1, 2 * H_C)))
    kg = row3(jnp.tile(k_norm, (1, 2 * H_C)))
    norm1_r, norm2_r, sgu_g, ret_g, diff_g = row3(norm1), row3(norm2), row3(sgu_norm), row3(ret_norm), row3(diff_norm)
    conv_b = row3(ffn_conv_b)
    rope = _rope_tables(dec_seq)
    cache_kt = jnp.transpose(cache_k, (0, 1, 3, 4, 5, 2)).reshape(dec_batch, depth, w_c, past)

    def layer(x, l, nb, sl, mod_row, is_sample, new_cache, new_states):
        lam_init = 0.8 - 0.6 * math.exp(-0.3 * l)
        outs = _projection(x, mods, mod_row, l, sl, norm1_r, w_in_b, sgu_g, sgu_w_b, sgu_bias, qg, kg,
                           rope if is_sample else None, new_cache)
        ya, qr, kr, vr, gr, qa, ka, va = outs[:8]
        seq3 = lambda a: a.reshape(nb, sl, a.shape[1])
        state = (state_ret_fwd, state_ret_bwd) if is_sample else None
        yb, *states = _mixer_b(seq3(qr), seq3(kr), seq3(vr), seq3(gr), l, lg_pair, lg_head, ret_g, state,
                               new_states)
        cache = (cache_kt, cache_v) if is_sample else None
        yc = _mixer_c(seq3(qa), seq3(ka), seq3(va), l, diff_lam, diff_g, lam_init, cache)
        x = _mlp(ya, yb.reshape(nb * sl, w_b), yc.reshape(nb * sl, w_c), x, mods, mod_row, l, sl, norm2_r,
                 w_out_b, up_b, ffn_conv, conv_b, down_b)
        return x, outs[8:], states

    y_prompt = x_prompt.reshape(batch * seq, d_model)
    new_cache, new_states = None, None
    for l in range(depth):
        y_prompt, new_cache, new_states = layer(y_prompt, l, batch, seq, lambda b: dec_batch, False,
                                                new_cache, new_states)

    y_sample = x_sample.reshape(dec_batch * dec_seq, d_model)
    for l in range(depth):
        y_sample, _, _ = layer(y_sample, l, dec_batch, dec_seq, lambda b: b, True, None, None)

    new_kt, new_v = new_cache
    new_k = jnp.transpose(new_kt.reshape(batch, depth, H_C, 2, HD_C, seq), (0, 1, 5, 2, 3, 4))
    return (y_prompt.reshape(batch, seq, d_model), y_sample.reshape(dec_batch, dec_seq, d_model),
            new_k, new_v, new_states[0], new_states[1])
```

```python
import functools
import math

import jax
import jax.numpy as jnp
from jax import lax
from jax.experimental import pallas as pl
from jax.experimental.pallas import tpu as pltpu

F32 = jnp.float32
BF16 = jnp.bfloat16

GRID_W = 64
CHUNK = 128
EPS = 1e-6
ROPE_BASE = 10000.0
G_A = 4
H_B = 4
DK_B = 64
H_C = 4
HD_C = 64
ROPE_PAIRS = HD_C // 4
LANES = 128
SUBLANES = 8
MXU_DIM = 256
MOD_COL_TILES = 4
PROJ_ROWS = 512
MLP_ROWS = 1024
ATTN_Q_ROWS = 1024
ATTN_KEY_BLOCK = 128
ATTN_SEQ_ROWS = 2048
RET_ROWS = 2048
RET_CHUNK = 256
RET_UNROLL = 4
VMEM_LIMIT = 64 * 1024 * 1024


def _cparams(sem):
    return pltpu.CompilerParams(dimension_semantics=sem, vmem_limit_bytes=VMEM_LIMIT)


def _const_spec(shape):
    n = len(shape)
    return pl.BlockSpec(shape, lambda *_: (0,) * n)


def _layer_spec(arr, layer, resident=False):
    n = arr.ndim - 1
    mode = dict(pipeline_mode=pl.Buffered(1)) if resident else {}
    return pl.BlockSpec((None,) + arr.shape[1:], lambda *_: (layer,) + (0,) * n, **mode)


def _silu(x):
    return x * jax.nn.sigmoid(x)


def _dot(a, b):
    return jnp.dot(a, b, preferred_element_type=F32)


def _dot_nt(a, b):
    return lax.dot_general(a, b, (((1,), (1,)), ((), ())), preferred_element_type=F32)


def _dot_tn(a, b):
    return lax.dot_general(a, b, (((0,), (0,)), ((), ())), preferred_element_type=F32)


def _lo_mask(rows):
    return lax.broadcasted_iota(jnp.int32, (rows, LANES), 1) < (LANES // 2)


def _split_halves(x):
    lo = _lo_mask(x.shape[0])
    zero = jnp.zeros_like(x)
    return jnp.concatenate([jnp.where(lo, x, zero), jnp.where(lo, zero, x)], axis=0)


def _group64_mean(x2):
    rows, n = x2.shape
    lo = _lo_mask(rows)
    outs = []
    for j in range(n // LANES):
        blk = x2[:, j * LANES:(j + 1) * LANES]
        s_lo = jnp.sum(jnp.where(lo, blk, 0.0), axis=-1, keepdims=True)
        s_hi = jnp.sum(jnp.where(lo, 0.0, blk), axis=-1, keepdims=True)
        outs.append(jnp.where(lo, s_lo, s_hi))
    out = outs[0] if len(outs) == 1 else jnp.concatenate(outs, axis=-1)
    return out * (1.0 / (LANES // 2))


def _group64_rmsnorm(x, g):
    return x * lax.rsqrt(_group64_mean(x * x) + EPS) * g


def _rmsnorm(x, g):
    return x * lax.rsqrt(jnp.mean(x * x, axis=-1, keepdims=True) + EPS) * g


def _rope(x, cos, sin):
    rows, n = x.shape
    lane = lax.broadcasted_iota(jnp.int32, (rows, LANES), 1)
    first = (lane % (2 * ROPE_PAIRS)) < ROPE_PAIRS
    outs = []
    for j in range(n // LANES):
        blk = x[:, j * LANES:(j + 1) * LANES]
        partner = jnp.where(first, pltpu.roll(blk, LANES - ROPE_PAIRS, 1), pltpu.roll(blk, ROPE_PAIRS, 1))
        outs.append(blk * cos + partner * sin)
    return jnp.concatenate(outs, axis=-1)


def _mod_kernel(cond_ref, w_ref, b_ref, o_ref):
    s = _silu(cond_ref[...]).astype(BF16)
    o_ref[0] = _dot(s, w_ref[0].astype(BF16)) + b_ref[0]


def _modulation(cond, w_mod, b_mod):
    depth, d, n = w_mod.shape
    rows = cond.shape[0]
    tn = n // MOD_COL_TILES
    return pl.pallas_call(
        _mod_kernel,
        grid=(depth, n // tn),
        in_specs=[
            _const_spec((rows, d)),
            pl.BlockSpec((1, d, tn), lambda l, j: (l, 0, j)),
            pl.BlockSpec((1, 1, tn), lambda l, j: (l, 0, j)),
        ],
        out_specs=pl.BlockSpec((1, rows, tn), lambda l, j: (l, 0, j)),
        out_shape=jax.ShapeDtypeStruct((depth, rows, n), F32),
        compiler_params=_cparams(("parallel", "parallel")),
        name="adaln_mod",
    )(cond, w_mod, b_mod.reshape(depth, 1, n))


def _mod_spec(mods, layer, row_of_step):
    return pl.BlockSpec((None, 1) + mods.shape[2:], lambda i: (layer, row_of_step(i), 0, 0))


def _proj_kernel(*refs, has_rope, seq_len, n_prev):
    if has_rope:
        (x_ref, mod_ref, g_ref, w_ref, sg_ref, ws_ref, sb_ref, qg_ref, kg_ref, cos_ref, sin_ref,
         ya_ref, qr_ref, kr_ref, vr_ref, gr_ref, qa_ref, ka_ref, va_ref) = refs
    elif n_prev:
        (x_ref, mod_ref, g_ref, w_ref, sg_ref, ws_ref, sb_ref, qg_ref, kg_ref, kt_prev_ref, vf_prev_ref,
         ya_ref, qr_ref, kr_ref, vr_ref, gr_ref, qa_ref, ka_ref, va_ref, kt_ref, vf_ref) = refs
        kt_ref[:, :n_prev] = kt_prev_ref[...]
        vf_ref[:, :n_prev] = vf_prev_ref[...]
    else:
        (x_ref, mod_ref, g_ref, w_ref, sg_ref, ws_ref, sb_ref, qg_ref, kg_ref,
         ya_ref, qr_ref, kr_ref, vr_ref, gr_ref, qa_ref, ka_ref, va_ref, kt_ref, vf_ref) = refs
    rows = x_ref.shape[0]
    m = mod_ref[0]
    hb = (_rmsnorm(x_ref[...], g_ref[...]) * (1.0 + m[1:2, :]) + m[0:1, :]).astype(BF16)
    wa = ya_ref.shape[1]
    wb = qr_ref.shape[1]
    wc = qa_ref.shape[1]

    def proj(c0, width):
        return _dot(hb, w_ref[:, c0:c0 + width])

    def emit_q(z):
        qn = _group64_rmsnorm(z, qg_ref[...])
        if has_rope:
            qn = _rope(qn, cos_ref[...], sin_ref[...])
        qa_ref[...] = (qn * (HD_C ** -0.5 * math.log2(math.e))).astype(BF16)

    def emit_k(z):
        kn = _group64_rmsnorm(z, kg_ref[...])
        if has_rope:
            kn = _rope(kn, cos_ref[...], sin_ref[...])
        else:
            for s in range(rows // seq_len):
                kt_ref[s, n_prev] = kn[s * seq_len:(s + 1) * seq_len, :].T
        ka_ref[...] = kn.astype(BF16)

    def emit_v(z):
        if not has_rope:
            for s in range(rows // seq_len):
                zs = z[s * seq_len:(s + 1) * seq_len, :]
                heads = jnp.stack([zs[:, h * LANES:(h + 1) * LANES] for h in range(H_C)], axis=0)
                vf_ref[s, n_prev] = jnp.swapaxes(heads, 0, 1)
        va_ref[...] = z.astype(BF16)

    def sgu_prepare(z):
        za = jax.nn.gelu(z)
        return za[:, :wa], _rmsnorm(za[:, wa:], sg_ref[...]).astype(BF16)

    def sgu_mix(u, v):
        lane = lax.broadcasted_iota(jnp.int32, (CHUNK, wa), 1)
        dg = wa // G_A
        for c in range(rows // CHUNK):
            rs = slice(c * CHUNK, (c + 1) * CHUNK)
            s = _dot(ws_ref[G_A - 1], v[rs])
            for g in range(G_A - 2, -1, -1):
                s = jnp.where(lane < (g + 1) * dg, _dot(ws_ref[g], v[rs]), s)
            ya_ref[rs, :] = (u[rs] * (s + sb_ref[...])).astype(BF16)

    def emit_ret_qk(z):
        qr_ref[...] = z[:, :wb].astype(BF16)
        kr_ref[...] = z[:, wb:] * (DK_B ** -0.5)

    def emit_ret_vg(z):
        vr_ref[...] = z[:, :wb].astype(BF16)
        gr_ref[...] = _silu(z[:, wb:])

    cb = 2 * wa
    cc = cb + 4 * wb
    z_a = proj(0, 2 * wa)
    z_q = proj(cc, wc)
    u, v = sgu_prepare(z_a)
    z_k = proj(cc + wc, wc)
    sgu_mix(u, v)
    emit_q(z_q)
    z_v = proj(cc + 2 * wc, wc)
    emit_k(z_k)
    z_r = proj(cb, 2 * wb)
    emit_v(z_v)
    z_g = proj(cb + 2 * wb, 2 * wb)
    emit_ret_qk(z_r)
    emit_ret_vg(z_g)


def _stacked_spec(lead, layers, tail):
    return pl.BlockSpec((lead, layers) + tail, lambda i: (i, 0) + (0,) * len(tail))


def _projection(x, mods, mod_row, layer, seq_len, g, w_bf16, sgu_g, sgu_w, sgu_bias, qg, kg, rope, prev_cache):
    t, d = x.shape
    rows = 2 * PROJ_ROWS if rope is not None else PROJ_ROWS
    wa = sgu_g.shape[2]
    wb = wa
    wc = qg.shape[2]
    has_rope = rope is not None
    tiles_per_seq = max(1, seq_len // rows)
    row_spec = lambda w: pl.BlockSpec((rows, w), lambda i: (i, 0))
    params = [g, w_bf16, sgu_g, sgu_w, sgu_bias, qg, kg]
    in_specs = [row_spec(d), _mod_spec(mods, layer, lambda i: mod_row(i // tiles_per_seq))]
    in_specs += [_layer_spec(p, layer) for p in params]
    args = [x, mods] + params
    outs = [(wa, BF16), (wb, BF16), (wb, F32), (wb, BF16), (wb, F32), (wc, BF16), (wc, BF16), (wc, BF16)]
    out_specs = [row_spec(w) for w, _ in outs]
    out_shape = [jax.ShapeDtypeStruct((t, w), dt) for w, dt in outs]
    if has_rope:
        tab_spec = pl.BlockSpec((rows, LANES), lambda i: (i % tiles_per_seq, 0))
        in_specs += [tab_spec, tab_spec]
        args += list(rope)
    else:
        seqs = rows // seq_len
        k_tail, v_tail = (wc, seq_len), (seq_len, H_C, LANES)
        if layer:
            in_specs += [_stacked_spec(seqs, layer, k_tail), _stacked_spec(seqs, layer, v_tail)]
            args += list(prev_cache)
        out_specs += [_stacked_spec(seqs, layer + 1, k_tail), _stacked_spec(seqs, layer + 1, v_tail)]
        out_shape += [jax.ShapeDtypeStruct((t // seq_len, layer + 1) + k_tail, F32),
                      jax.ShapeDtypeStruct((t // seq_len, layer + 1) + v_tail, F32)]
    return pl.pallas_call(
        functools.partial(_proj_kernel, has_rope=has_rope, seq_len=seq_len, n_prev=0 if has_rope else layer),
        grid=(t // rows,),
        in_specs=in_specs,
        out_specs=out_specs,
        out_shape=out_shape,
        compiler_params=_cparams(("parallel",)),
        name="projection",
    )(*args)


def _log_sigmoid(x):
    y = -x
    return -(jnp.maximum(y, 0.0) + jnp.log1p(jnp.exp(-jnp.abs(y))))


def _ret_kernel(*refs, chunk, n_chunks, has_state, n_prev):
    if has_state:
        (q_ref, k_ref, v_ref, gate_ref, r0f_ref, r0b_ref, lgp_ref, lgh_ref, norm_ref,
         y_ref, d_scr, dec_scr, kv_scr, rs_scr, o_scr) = refs
        state_outs = None
    elif n_prev:
        (q_ref, k_ref, v_ref, gate_ref, rf_prev_ref, rb_prev_ref, lgp_ref, lgh_ref, norm_ref,
         y_ref, rf_ref, rb_ref, d_scr, dec_scr, kv_scr, rs_scr, o_scr) = refs
        rf_ref[:, :n_prev] = rf_prev_ref[...]
        rb_ref[:, :n_prev] = rb_prev_ref[...]
        state_outs = (rf_ref, rb_ref)
    else:
        (q_ref, k_ref, v_ref, gate_ref, lgp_ref, lgh_ref, norm_ref,
         y_ref, rf_ref, rb_ref, d_scr, dec_scr, kv_scr, rs_scr, o_scr) = refs
        state_outs = (rf_ref, rb_ref)
    bt = q_ref.shape[0]
    n_pairs = H_B // 2
    half = LANES // 2
    cs = chunk

    @pl.when(pl.program_id(0) == 0)
    def _build_decay_tables():
        lgp = _log_sigmoid(lgp_ref[...])
        lgh = _log_sigmoid(lgh_ref[...])
        dist = (lax.broadcasted_iota(jnp.int32, (cs, cs), 0) - lax.broadcasted_iota(jnp.int32, (cs, cs), 1)).astype(F32)
        for h in range(H_B):
            lf = lgh[h:h + 1, :]
            lb = lgh[H_B + h:H_B + h + 1, :]
            d_f = jnp.where(dist >= 0, jnp.exp(lf * jnp.maximum(dist, 0.0)), 0.0)
            d_b = jnp.where(dist <= 0, jnp.exp(lb * jnp.maximum(-dist, 0.0)), 0.0)
            d_scr[h // 2, (h % 2) * cs:(h % 2 + 1) * cs, :] = d_f + d_b
        pos = lax.broadcasted_iota(jnp.int32, (cs, LANES), 0).astype(F32)
        for p in range(n_pairs):
            lf = lgp[p:p + 1, :]
            lb = lgp[n_pairs + p:n_pairs + p + 1, :]
            dec_scr[0, p] = jnp.exp(lf * (pos + 1.0))
            dec_scr[1, p] = jnp.exp(lf * (cs - 1.0 - pos))
            dec_scr[2, p] = jnp.exp(lb * (cs - pos))
            dec_scr[3, p] = jnp.exp(lb * pos)
            dec_scr[4, p] = jnp.broadcast_to(jnp.exp(lf * float(cs)), (cs, LANES))
            dec_scr[5, p] = jnp.broadcast_to(jnp.exp(lb * float(cs)), (cs, LANES))

    lo = _lo_mask(cs)
    blockdiag = ((lax.broadcasted_iota(jnp.int32, (LANES, LANES), 0) < half)
                 == (lax.broadcasted_iota(jnp.int32, (LANES, LANES), 1) < half))

    for b in range(bt):
        def intra(c, carry):
            rows = pl.ds(pl.multiple_of(c * cs, cs), cs)
            qc, kc, vc = q_ref[b, rows, :], k_ref[b, rows, :], v_ref[b, rows, :]
            for p in range(n_pairs):
                sl = slice(p * LANES, (p + 1) * LANES)
                k128, vb = kc[:, sl], vc[:, sl]
                inner = (_dot_nt(_split_halves(qc[:, sl]), k128.astype(BF16)) * d_scr[p]).astype(BF16)
                oo = _dot(inner, vb)
                o_scr[rows, sl] = jnp.where(lo, oo[:cs], oo[cs:])
                kk = jnp.concatenate([k128 * dec_scr[1, p], k128 * dec_scr[3, p]], axis=1).astype(BF16)
                kv = _dot_tn(kk, vb)
                kv_scr[0, c, p] = jnp.where(blockdiag, kv[:LANES], 0.0)
                kv_scr[1, c, p] = jnp.where(blockdiag, kv[LANES:], 0.0)
            return carry

        lax.fori_loop(0, n_chunks, intra, 0, unroll=min(n_chunks, RET_UNROLL))

        for p in range(n_pairs):
            for d in range(2):
                if has_state:
                    r0_ref = (r0f_ref, r0b_ref)[d]
                    zero = jnp.zeros((half, half), F32)
                    top = jnp.concatenate([r0_ref[b, 2 * p], zero], axis=1)
                    bot = jnp.concatenate([zero, r0_ref[b, 2 * p + 1]], axis=1)
                    r = jnp.concatenate([top, bot], axis=0)
                else:
                    r = jnp.zeros((LANES, LANES), F32)
                order = range(n_chunks) if d == 0 else range(n_chunks - 1, -1, -1)
                for c in order:
                    rs_scr[c, p, :, d * LANES:(d + 1) * LANES] = r.astype(BF16)
                    r = r * dec_scr[4 + d, p, :LANES, :] + kv_scr[d, c, p]
                if state_outs is not None:
                    state_outs[d][b, n_prev, 2 * p] = r[:half, :half]
                    state_outs[d][b, n_prev, 2 * p + 1] = r[half:, half:]

        def cross(c, carry):
            rows = pl.ds(pl.multiple_of(c * cs, cs), cs)
            qc = q_ref[b, rows, :]
            outs = []
            for p in range(n_pairs):
                sl = slice(p * LANES, (p + 1) * LANES)
                t = _dot(qc[:, sl], rs_scr[c, p])
                outs.append(o_scr[rows, sl] + t[:, :LANES] * dec_scr[0, p] + t[:, LANES:] * dec_scr[2, p])
            o = _group64_rmsnorm(jnp.concatenate(outs, axis=-1), norm_ref[...])
            y_ref[b, rows, :] = (gate_ref[b, rows, :] * o).astype(BF16)
            return carry

        lax.fori_loop(0, n_chunks, cross, 0, unroll=min(n_chunks, RET_UNROLL))


def _mixer_b(q, k, v, gate, layer, lg_pair, lg_head, norm, state, prev_states):
    b, l, wb = q.shape
    bt = max(1, RET_ROWS // l)
    chunk = min(l, RET_CHUNK)
    n_chunks = l // chunk
    has_state = state is not None
    tile = pl.BlockSpec((bt, l, wb), lambda i: (i, 0, 0))
    in_specs = [tile, tile, tile, tile]
    args = [q, k, v, gate]
    out_specs = [tile]
    out_shape = [jax.ShapeDtypeStruct((b, l, wb), BF16)]
    s_tail = (H_B, DK_B, DK_B)
    if has_state:
        sspec = pl.BlockSpec((bt, None) + s_tail, lambda i: (i, layer, 0, 0, 0))
        in_specs += [sspec, sspec]
        args += list(state)
    else:
        if layer:
            in_specs += [_stacked_spec(bt, layer, s_tail)] * 2
            args += list(prev_states)
        out_specs += [_stacked_spec(bt, layer + 1, s_tail)] * 2
        out_shape += [jax.ShapeDtypeStruct((b, layer + 1) + s_tail, F32)] * 2
    params = [lg_pair, lg_head, norm]
    in_specs += [_layer_spec(p, layer) for p in params]
    args += params
    n_pairs = H_B // 2
    return pl.pallas_call(
        functools.partial(_ret_kernel, chunk=chunk, n_chunks=n_chunks, has_state=has_state,
                          n_prev=0 if has_state else layer),
        grid=(b // bt,),
        in_specs=in_specs,
        out_specs=out_specs,
        out_shape=out_shape,
        scratch_shapes=[
            pltpu.VMEM((n_pairs, 2 * chunk, chunk), F32),
            pltpu.VMEM((6, n_pairs, chunk, LANES), F32),
            pltpu.VMEM((2, n_chunks, n_pairs, LANES, LANES), F32),
            pltpu.VMEM((n_chunks, n_pairs, LANES, 2 * LANES), BF16),
            pltpu.VMEM((l, wb), F32),
        ],
        compiler_params=_cparams(("arbitrary",)),
        name="retention",
    )(*args)


def _diff_softmax_t(st, lam, tq):
    n_keys = st.shape[0]
    st4 = st.reshape(n_keys // ATTN_KEY_BLOCK, ATTN_KEY_BLOCK // SUBLANES, SUBLANES, 2 * tq)
    m_loc = jnp.max(st4, axis=1)
    e4 = jnp.exp2(st4 - m_loc[:, None])
    l_loc = jnp.sum(e4, axis=1)
    m = jnp.max(jnp.max(m_loc, axis=0), axis=0, keepdims=True)
    s_loc = jnp.exp2(m_loc - m)
    l = jnp.sum(jnp.sum(l_loc * s_loc, axis=0), axis=0, keepdims=True)
    coef = jnp.concatenate([1.0 / l[:, :tq], lam / l[:, tq:]], axis=1)
    w = e4 * (coef * s_loc)[:, None]
    return (w[..., :tq] - w[..., tq:]).reshape(n_keys, tq).astype(BF16)


def _attn_kernel(*refs, has_cache, lam_init):
    if has_cache:
        (q_ref, k_ref, v_ref, ck_ref, cv_ref, dl_ref, dn_ref, y_ref, k_scr, v_scr) = refs
        n_past = ck_ref.shape[1]

        @pl.when(pl.program_id(1) == 0)
        def _gather_keys():
            k_scr[...] = ck_ref[...].T.astype(BF16)
            cv = jnp.swapaxes(cv_ref[...], 0, 1)
            for h in range(H_C):
                v_scr[:, h * LANES:(h + 1) * LANES] = cv[h].astype(BF16)

        keys = lambda b, sl: jnp.concatenate([k_scr[:, sl], k_ref[b, :, sl]], axis=0)
        vals = lambda b, sl: jnp.concatenate([v_scr[:, sl], v_ref[b, :, sl]], axis=0)
    else:
        (q_ref, k_ref, v_ref, dl_ref, dn_ref, y_ref) = refs
        keys = lambda b, sl: k_ref[b, :, sl]
        vals = lambda b, sl: v_ref[b, :, sl]
    bt, tq, _ = q_ref.shape
    dl = dl_ref[...]
    lam = (jnp.exp(jnp.sum(dl[0:1] * dl[1:2], axis=-1, keepdims=True))
           - jnp.exp(jnp.sum(dl[2:3] * dl[3:4], axis=-1, keepdims=True)) + lam_init)
    units = [(b, slice(h * LANES, (h + 1) * LANES)) for b in range(bt) for h in range(H_C)]
    scores = lambda b, sl: _dot_nt(keys(b, sl), _split_halves(q_ref[b, :, sl]))
    st_next = scores(*units[0])
    for i, (b, sl) in enumerate(units):
        st = st_next
        if i + 1 < len(units):
            st_next = scores(*units[i + 1])
        o = _dot_tn(_diff_softmax_t(st, lam, tq), vals(b, sl))
        y_ref[b, :, sl] = (_rmsnorm(o, dn_ref[...]) * (1.0 - lam_init)).astype(BF16)


def _mixer_c(q, k, v, layer, dl, dn, lam_init, cache):
    b, l, wc = q.shape
    has_cache = cache is not None
    tq = min(l, ATTN_Q_ROWS)
    bt = 1 if has_cache else max(1, ATTN_SEQ_ROWS // l)
    q_spec = pl.BlockSpec((bt, tq, wc), lambda i, j: (i, j, 0))
    kv_spec = pl.BlockSpec((bt, l, wc), lambda i, j: (i, 0, 0))
    in_specs = [q_spec, kv_spec, kv_spec]
    args = [q, k, v]
    n_keys = l
    if has_cache:
        ck_t, cv = cache
        n_past = cv.shape[2]
        n_keys += n_past
        in_specs += [pl.BlockSpec((None, None, wc, n_past), lambda i, j: (i, layer, 0, 0)),
                     pl.BlockSpec((None, None, n_past, H_C, LANES), lambda i, j: (i, layer, 0, 0, 0))]
        args += [ck_t, cv]
    in_specs += [_layer_spec(dl, layer), _layer_spec(dn, layer)]
    args += [dl, dn]
    scratch = [pltpu.VMEM((n_past, wc), BF16), pltpu.VMEM((n_past, wc), BF16)] if has_cache else []
    return pl.pallas_call(
        functools.partial(_attn_kernel, has_cache=has_cache, lam_init=lam_init),
        grid=(b // bt, l // tq),
        in_specs=in_specs,
        out_specs=q_spec,
        out_shape=jax.ShapeDtypeStruct((b, l, wc), BF16),
        scratch_shapes=scratch,
        compiler_params=_cparams(("parallel", "arbitrary")),
        name="diff_attention",
    )(*args)


def _mlp_kernel(ya_ref, yb_ref, yc_ref, x_ref, mod_ref, g_ref, wo_ref, up_ref, cw_ref, cb_ref, down_ref,
                o_ref, h_scr, g_scr, *, seq_len):
    rows = x_ref.shape[0]
    m = mod_ref[0]
    y = None
    k0 = 0
    for y_ref in (ya_ref, yb_ref, yc_ref):
        kw = y_ref.shape[1]
        part = _dot(y_ref[...], wo_ref[k0:k0 + kw, :])
        y = part if y is None else y + part
        k0 += kw
    x1 = x_ref[...] + m[2:3, :] * y
    o_ref[...] = x1
    h_scr[...] = (_rmsnorm(x1, g_ref[...]) * (1.0 + m[4:5, :]) + m[3:4, :]).astype(BF16)

    d_ff = down_ref.shape[0]
    fc = MXU_DIM
    pos = lax.broadcasted_iota(jnp.int32, (rows, fc), 0) % seq_len
    has_prev = pos > 0
    has_next = pos < seq_len - 1

    def conv(c0):
        u = _dot(h_scr[...], up_ref[:, c0:c0 + fc])
        w = cw_ref[:, c0:c0 + fc]
        prev = jnp.where(has_prev, pltpu.roll(u, 1, 0), 0.0)
        nxt = jnp.where(has_next, pltpu.roll(u, rows - 1, 0), 0.0)
        return prev * w[0:1, :] + u * w[1:2, :] + nxt * w[2:3, :] + cb_ref[:, c0:c0 + fc]

    for j in range(d_ff // fc):
        a = conv(j * fc)
        b = conv(d_ff + j * fc)
        g_scr[:, j * fc:(j + 1) * fc] = (_silu(a) * b).astype(BF16)

    o_ref[...] = o_ref[...] + m[5:6, :] * _dot(g_scr[...], down_ref[...])


def _mlp(ya, yb, yc, x, mods, mod_row, layer, seq_len, g, w_out, up, cw, cb, down):
    t, d = x.shape
    rows = MLP_ROWS
    tiles_per_seq = max(1, seq_len // rows)
    row_spec = lambda a: pl.BlockSpec((rows, a.shape[1]), lambda i: (i, 0))
    weights = [w_out, up, cw, cb, down]
    return pl.pallas_call(
        functools.partial(_mlp_kernel, seq_len=seq_len),
        grid=(t // rows,),
        in_specs=[row_spec(ya), row_spec(yb), row_spec(yc), row_spec(x),
                  _mod_spec(mods, layer, lambda i: mod_row(i // tiles_per_seq)), _layer_spec(g, layer)]
                 + [_layer_spec(w, layer, resident=True) for w in weights],
        out_specs=row_spec(x),
        out_shape=jax.ShapeDtypeStruct((t, d), F32),
        scratch_shapes=[pltpu.VMEM((rows, d), BF16), pltpu.VMEM((rows, down.shape[1]), BF16)],
        compiler_params=_cparams(("parallel",)),
        name="out_proj_mlp",
    )(ya, yb, yc, x, mods, g, *weights)


def _rope_tables(l):
    rows = l // GRID_W
    t_row = jnp.repeat(jnp.arange(rows, dtype=F32), GRID_W)
    t_col = jnp.tile(jnp.arange(GRID_W, dtype=F32), rows)
    inv = ROPE_BASE ** (-jnp.arange(ROPE_PAIRS, dtype=F32) / ROPE_PAIRS)
    ar, ac = t_row[:, None] * inv, t_col[:, None] * inv
    cos = jnp.concatenate([jnp.cos(ar), jnp.cos(ar), jnp.cos(ac), jnp.cos(ac)], axis=-1)
    sin = jnp.concatenate([-jnp.sin(ar), jnp.sin(ar), -jnp.sin(ac), jnp.sin(ac)], axis=-1)
    return jnp.tile(cos, (1, 2)), jnp.tile(sin, (1, 2))


def kernel(x_prompt, x_sample, c, cache_k, cache_v, state_ret_fwd, state_ret_bwd, c_ctx, norm1, w_mod, b_mod,
           w_in, sgu_norm, sgu_w, sgu_b, ret_logit_fwd, ret_logit_bwd, ret_norm, q_norm, k_norm, diff_lam,
           diff_norm, w_out, norm2, ffn_up, ffn_conv, ffn_conv_b, ffn_down):
    depth, d_model, _ = w_mod.shape
    batch, seq, _ = x_prompt.shape
    dec_batch, dec_seq, _ = x_sample.shape
    w_a = sgu_norm.shape[1]
    w_b = H_B * ret_norm.shape[2]
    w_c = 2 * H_C * HD_C
    past = cache_k.shape[2]

    pad = (-(dec_batch + 1)) % 8
    cond = jnp.concatenate([c, c_ctx[None, :], jnp.zeros((pad, d_model), F32)], axis=0)
    mods = _modulation(cond, w_mod, b_mod).reshape(depth, cond.shape[0], 6, d_model)

    w_in_b = w_in.astype(BF16)
    w_out_b = w_out.astype(BF16)
    sgu_w_b = sgu_w.astype(BF16)
    up_b = ffn_up.astype(BF16)
    down_b = ffn_down.astype(BF16)
    row3 = lambda a: a.reshape(depth, 1, -1)
    sgu_bias = jnp.repeat(jnp.swapaxes(sgu_b, 1, 2), w_a // G_A, axis=2)
    logits = jnp.stack([ret_logit_fwd, ret_logit_bwd], axis=1)
    lg_pair = jnp.repeat(logits, DK_B, axis=2).reshape(depth, 2 * (H_B // 2), LANES)
    lg_head = logits.reshape(depth, 2 * H_B, 1)
    qg = row3(jnp.tile(q_norm, (1, 2 * H_C)))
    kg = row3(jnp.tile(k_norm, (1, 2 * H_C)))
    norm1_r, norm2_r, sgu_g, ret_g, diff_g = row3(norm1), row3(norm2), row3(sgu_norm), row3(ret_norm), row3(diff_norm)
    conv_b = row3(ffn_conv_b)
    rope = _rope_tables(dec_seq)
    cache_kt = jnp.transpose(cache_k, (0, 1, 3, 4, 5, 2)).reshape(dec_batch, depth, w_c, past)

    def layer(x, l, nb, sl, mod_row, is_sample, new_cache, new_states):
        lam_init = 0.8 - 0.6 * math.exp(-0.3 * l)
        outs = _projection(x, mods, mod_row, l, sl, norm1_r, w_in_b, sgu_g, sgu_w_b, sgu_bias, qg, kg,
                           rope if is_sample else None, new_cache)
        ya, qr, kr, vr, gr, qa, ka, va = outs[:8]
        seq3 = lambda a: a.reshape(nb, sl, a.shape[1])
        state = (state_ret_fwd, state_ret_bwd) if is_sample else None
        yb, *states = _mixer_b(seq3(qr), seq3(kr), seq3(vr), seq3(gr), l, lg_pair, lg_head, ret_g, state,
                               new_states)
        cache = (cache_kt, cache_v) if is_sample else None
        yc = _mixer_c(seq3(qa), seq3(ka), seq3(va), l, diff_lam, diff_g, lam_init, cache)
        x = _mlp(ya, yb.reshape(nb * sl, w_b), yc.reshape(nb * sl, w_c), x, mods, mod_row, l, sl, norm2_r,
                 w_out_b, up_b, ffn_conv, conv_b, down_b)
        return x, outs[8:], states

    y_prompt = x_prompt.reshape(batch * seq, d_model)
    new_cache, new_states = None, None
    for l in range(depth):
        y_prompt, new_cache, new_states = layer(y_prompt, l, batch, seq, lambda b: dec_batch, False,
                                                new_cache, new_states)

    y_sample = x_sample.reshape(dec_batch * dec_seq, d_model)
    for l in range(depth):
        y_sample, _, _ = layer(y_sample, l, dec_batch, dec_seq, lambda b: b, True, None, None)

    new_kt, new_v = new_cache
    new_k = jnp.transpose(new_kt.reshape(batch, depth, H_C, 2, HD_C, seq), (0, 1, 5, 2, 3, 4))
    return (y_prompt.reshape(batch, seq, d_model), y_sample.reshape(dec_batch, dec_seq, d_model),
            new_k, new_v, new_states[0], new_states[1])
```

```python
import functools
import math

import jax
import jax.numpy as jnp
from jax import lax
from jax.experimental import pallas as pl
from jax.experimental.pallas import tpu as pltpu

F32 = jnp.float32
BF16 = jnp.bfloat16

GRID_W = 64
CHUNK = 128
EPS = 1e-6
ROPE_BASE = 10000.0
G_A = 4
H_B = 4
DK_B = 64
H_C = 4
HD_C = 64
ROPE_PAIRS = HD_C // 4
LANES = 128
SUBLANES = 8
MXU_DIM = 256
MOD_COL_TILES = 4
PROJ_ROWS = 512
MLP_ROWS = 1024
ATTN_Q_ROWS = 1024
ATTN_KEY_BLOCK = 128
ATTN_SEQ_ROWS = 2048
RET_ROWS = 2048
RET_CHUNK = 256
RET_UNROLL = 4
VMEM_LIMIT = 60 * 1024 * 1024


def _cparams(sem):
    return pltpu.CompilerParams(dimension_semantics=sem, vmem_limit_bytes=VMEM_LIMIT)


def _const_spec(shape):
    n = len(shape)
    return pl.BlockSpec(shape, lambda *_: (0,) * n)


def _layer_spec(arr, layer, resident=False):
    n = arr.ndim - 1
    mode = dict(pipeline_mode=pl.Buffered(1)) if resident else {}
    return pl.BlockSpec((None,) + arr.shape[1:], lambda *_: (layer,) + (0,) * n, **mode)


def _silu(x):
    return x * jax.nn.sigmoid(x)


def _dot(a, b):
    return jnp.dot(a, b, preferred_element_type=F32)


def _dot_nt(a, b):
    return lax.dot_general(a, b, (((1,), (1,)), ((), ())), preferred_element_type=F32)


def _dot_tn(a, b):
    return lax.dot_general(a, b, (((0,), (0,)), ((), ())), preferred_element_type=F32)


def _lo_mask(rows):
    return lax.broadcasted_iota(jnp.int32, (rows, LANES), 1) < (LANES // 2)


def _split_halves(x):
    lo = _lo_mask(x.shape[0])
    zero = jnp.zeros_like(x)
    return jnp.concatenate([jnp.where(lo, x, zero), jnp.where(lo, zero, x)], axis=0)


def _group64_mean(x2):
    rows, n = x2.shape
    lo = _lo_mask(rows)
    outs = []
    for j in range(n // LANES):
        blk = x2[:, j * LANES:(j + 1) * LANES]
        s_lo = jnp.sum(jnp.where(lo, blk, 0.0), axis=-1, keepdims=True)
        s_hi = jnp.sum(jnp.where(lo, 0.0, blk), axis=-1, keepdims=True)
        outs.append(jnp.where(lo, s_lo, s_hi))
    out = outs[0] if len(outs) == 1 else jnp.concatenate(outs, axis=-1)
    return out * (1.0 / (LANES // 2))


def _group64_rmsnorm(x, g):
    return x * lax.rsqrt(_group64_mean(x * x) + EPS) * g


def _rmsnorm(x, g):
    return x * lax.rsqrt(jnp.mean(x * x, axis=-1, keepdims=True) + EPS) * g


def _rope(x, cos, sin):
    rows, n = x.shape
    lane = lax.broadcasted_iota(jnp.int32, (rows, LANES), 1)
    first = (lane % (2 * ROPE_PAIRS)) < ROPE_PAIRS
    outs = []
    for j in range(n // LANES):
        blk = x[:, j * LANES:(j + 1) * LANES]
        partner = jnp.where(first, pltpu.roll(blk, LANES - ROPE_PAIRS, 1), pltpu.roll(blk, ROPE_PAIRS, 1))
        outs.append(blk * cos + partner * sin)
    return jnp.concatenate(outs, axis=-1)


def _mod_kernel(cond_ref, w_ref, b_ref, o_ref):
    s = _silu(cond_ref[...]).astype(BF16)
    o_ref[0] = _dot(s, w_ref[0].astype(BF16)) + b_ref[0]


def _modulation(cond, w_mod, b_mod):
    depth, d, n = w_mod.shape
    rows = cond.shape[0]
    tn = n // MOD_COL_TILES
    return pl.pallas_call(
        _mod_kernel,
        grid=(depth, n // tn),
        in_specs=[
            _const_spec((rows, d)),
            pl.BlockSpec((1, d, tn), lambda l, j: (l, 0, j)),
            pl.BlockSpec((1, 1, tn), lambda l, j: (l, 0, j)),
        ],
        out_specs=pl.BlockSpec((1, rows, tn), lambda l, j: (l, 0, j)),
        out_shape=jax.ShapeDtypeStruct((depth, rows, n), F32),
        compiler_params=_cparams(("parallel", "parallel")),
        name="adaln_mod",
    )(cond, w_mod, b_mod.reshape(depth, 1, n))


def _mod_spec(mods, layer, row_of_step):
    return pl.BlockSpec((None, 1) + mods.shape[2:], lambda i: (layer, row_of_step(i), 0, 0))


def _proj_kernel(*refs, has_rope, seq_len, n_prev):
    if has_rope:
        (x_ref, mod_ref, g_ref, w_ref, sg_ref, ws_ref, sb_ref, qg_ref, kg_ref, cos_ref, sin_ref,
         ya_ref, qr_ref, kr_ref, vr_ref, gr_ref, qa_ref, ka_ref, va_ref) = refs
    elif n_prev:
        (x_ref, mod_ref, g_ref, w_ref, sg_ref, ws_ref, sb_ref, qg_ref, kg_ref, kt_prev_ref, vf_prev_ref,
         ya_ref, qr_ref, kr_ref, vr_ref, gr_ref, qa_ref, ka_ref, va_ref, kt_ref, vf_ref) = refs
        kt_ref[:, :n_prev] = kt_prev_ref[...]
        vf_ref[:, :n_prev] = vf_prev_ref[...]
    else:
        (x_ref, mod_ref, g_ref, w_ref, sg_ref, ws_ref, sb_ref, qg_ref, kg_ref,
         ya_ref, qr_ref, kr_ref, vr_ref, gr_ref, qa_ref, ka_ref, va_ref, kt_ref, vf_ref) = refs
    rows = x_ref.shape[0]
    m = mod_ref[0]
    hb = (_rmsnorm(x_ref[...], g_ref[...]) * (1.0 + m[1:2, :]) + m[0:1, :]).astype(BF16)
    wa = ya_ref.shape[1]
    wb = qr_ref.shape[1]
    wc = qa_ref.shape[1]

    def proj(c0, width):
        return _dot(hb, w_ref[:, c0:c0 + width])

    def emit_q(z):
        qn = _group64_rmsnorm(z, qg_ref[...])
        if has_rope:
            qn = _rope(qn, cos_ref[...], sin_ref[...])
        qa_ref[...] = (qn * (HD_C ** -0.5 * math.log2(math.e))).astype(BF16)

    def emit_k(z):
        kn = _group64_rmsnorm(z, kg_ref[...])
        if has_rope:
            kn = _rope(kn, cos_ref[...], sin_ref[...])
        else:
            for s in range(rows // seq_len):
                kt_ref[s, n_prev] = kn[s * seq_len:(s + 1) * seq_len, :].T
        ka_ref[...] = kn.astype(BF16)

    def emit_v(z):
        if not has_rope:
            for s in range(rows // seq_len):
                zs = z[s * seq_len:(s + 1) * seq_len, :]
                heads = jnp.stack([zs[:, h * LANES:(h + 1) * LANES] for h in range(H_C)], axis=0)
                vf_ref[s, n_prev] = jnp.swapaxes(heads, 0, 1)
        va_ref[...] = z.astype(BF16)

    def sgu_prepare(z):
        za = jax.nn.gelu(z)
        return za[:, :wa], _rmsnorm(za[:, wa:], sg_ref[...]).astype(BF16)

    def sgu_mix(u, v):
        lane = lax.broadcasted_iota(jnp.int32, (CHUNK, wa), 1)
        dg = wa // G_A
        for c in range(rows // CHUNK):
            rs = slice(c * CHUNK, (c + 1) * CHUNK)
            s = _dot(ws_ref[G_A - 1], v[rs])
            for g in range(G_A - 2, -1, -1):
                s = jnp.where(lane < (g + 1) * dg, _dot(ws_ref[g], v[rs]), s)
            ya_ref[rs, :] = (u[rs] * (s + sb_ref[...])).astype(BF16)

    def emit_ret_qk(z):
        qr_ref[...] = z[:, :wb].astype(BF16)
        kr_ref[...] = z[:, wb:] * (DK_B ** -0.5)

    def emit_ret_vg(z):
        vr_ref[...] = z[:, :wb].astype(BF16)
        gr_ref[...] = _silu(z[:, wb:])

    cb = 2 * wa
    cc = cb + 4 * wb
    z_a = proj(0, 2 * wa)
    z_q = proj(cc, wc)
    u, v = sgu_prepare(z_a)
    z_k = proj(cc + wc, wc)
    sgu_mix(u, v)
    emit_q(z_q)
    z_v = proj(cc + 2 * wc, wc)
    emit_k(z_k)
    z_r = proj(cb, 2 * wb)
    emit_v(z_v)
    z_g = proj(cb + 2 * wb, 2 * wb)
    emit_ret_qk(z_r)
    emit_ret_vg(z_g)


def _stacked_spec(lead, layers, tail):
    return pl.BlockSpec((lead, layers) + tail, lambda i: (i, 0) + (0,) * len(tail))


def _projection(x, mods, mod_row, layer, seq_len, g, w_bf16, sgu_g, sgu_w, sgu_bias, qg, kg, rope, prev_cache):
    t, d = x.shape
    rows = 2 * PROJ_ROWS if rope is not None else PROJ_ROWS
    wa = sgu_g.shape[2]
    wb = wa
    wc = qg.shape[2]
    has_rope = rope is not None
    tiles_per_seq = max(1, seq_len // rows)
    row_spec = lambda w: pl.BlockSpec((rows, w), lambda i: (i, 0))
    params = [g, w_bf16, sgu_g, sgu_w, sgu_bias, qg, kg]
    in_specs = [row_spec(d), _mod_spec(mods, layer, lambda i: mod_row(i // tiles_per_seq))]
    in_specs += [_layer_spec(p, layer) for p in params]
    args = [x, mods] + params
    outs = [(wa, BF16), (wb, BF16), (wb, F32), (wb, BF16), (wb, F32), (wc, BF16), (wc, BF16), (wc, BF16)]
    out_specs = [row_spec(w) for w, _ in outs]
    out_shape = [jax.ShapeDtypeStruct((t, w), dt) for w, dt in outs]
    if has_rope:
        tab_spec = pl.BlockSpec((rows, LANES), lambda i: (i % tiles_per_seq, 0))
        in_specs += [tab_spec, tab_spec]
        args += list(rope)
    else:
        seqs = rows // seq_len
        k_tail, v_tail = (wc, seq_len), (seq_len, H_C, LANES)
        if layer:
            in_specs += [_stacked_spec(seqs, layer, k_tail), _stacked_spec(seqs, layer, v_tail)]
            args += list(prev_cache)
        out_specs += [_stacked_spec(seqs, layer + 1, k_tail), _stacked_spec(seqs, layer + 1, v_tail)]
        out_shape += [jax.ShapeDtypeStruct((t // seq_len, layer + 1) + k_tail, F32),
                      jax.ShapeDtypeStruct((t // seq_len, layer + 1) + v_tail, F32)]
    return pl.pallas_call(
        functools.partial(_proj_kernel, has_rope=has_rope, seq_len=seq_len, n_prev=0 if has_rope else layer),
        grid=(t // rows,),
        in_specs=in_specs,
        out_specs=out_specs,
        out_shape=out_shape,
        compiler_params=_cparams(("parallel",)),
        name="projection",
    )(*args)


def _log_sigmoid(x):
    y = -x
    return -(jnp.maximum(y, 0.0) + jnp.log1p(jnp.exp(-jnp.abs(y))))


def _ret_kernel(*refs, chunk, n_chunks, has_state, n_prev):
    if has_state:
        (q_ref, k_ref, v_ref, gate_ref, r0f_ref, r0b_ref, lgp_ref, lgh_ref, norm_ref,
         y_ref, d_scr, dec_scr, kv_scr, rs_scr, o_scr) = refs
        state_outs = None
    elif n_prev:
        (q_ref, k_ref, v_ref, gate_ref, rf_prev_ref, rb_prev_ref, lgp_ref, lgh_ref, norm_ref,
         y_ref, rf_ref, rb_ref, d_scr, dec_scr, kv_scr, rs_scr, o_scr) = refs
        rf_ref[:, :n_prev] = rf_prev_ref[...]
        rb_ref[:, :n_prev] = rb_prev_ref[...]
        state_outs = (rf_ref, rb_ref)
    else:
        (q_ref, k_ref, v_ref, gate_ref, lgp_ref, lgh_ref, norm_ref,
         y_ref, rf_ref, rb_ref, d_scr, dec_scr, kv_scr, rs_scr, o_scr) = refs
        state_outs = (rf_ref, rb_ref)
    bt = q_ref.shape[0]
    n_pairs = H_B // 2
    half = LANES // 2
    cs = chunk

    @pl.when(pl.program_id(0) == 0)
    def _build_decay_tables():
        lgp = _log_sigmoid(lgp_ref[...])
        lgh = _log_sigmoid(lgh_ref[...])
        dist = (lax.broadcasted_iota(jnp.int32, (cs, cs), 0) - lax.broadcasted_iota(jnp.int32, (cs, cs), 1)).astype(F32)
        for h in range(H_B):
            lf = lgh[h:h + 1, :]
            lb = lgh[H_B + h:H_B + h + 1, :]
            d_f = jnp.where(dist >= 0, jnp.exp(lf * jnp.maximum(dist, 0.0)), 0.0)
            d_b = jnp.where(dist <= 0, jnp.exp(lb * jnp.maximum(-dist, 0.0)), 0.0)
            d_scr[h // 2, (h % 2) * cs:(h % 2 + 1) * cs, :] = d_f + d_b
        pos = lax.broadcasted_iota(jnp.int32, (cs, LANES), 0).astype(F32)
        for p in range(n_pairs):
            lf = lgp[p:p + 1, :]
            lb = lgp[n_pairs + p:n_pairs + p + 1, :]
            dec_scr[0, p] = jnp.exp(lf * (pos + 1.0))
            dec_scr[1, p] = jnp.exp(lf * (cs - 1.0 - pos))
            dec_scr[2, p] = jnp.exp(lb * (cs - pos))
            dec_scr[3, p] = jnp.exp(lb * pos)
            dec_scr[4, p] = jnp.broadcast_to(jnp.exp(lf * float(cs)), (cs, LANES))
            dec_scr[5, p] = jnp.broadcast_to(jnp.exp(lb * float(cs)), (cs, LANES))

    lo = _lo_mask(cs)
    blockdiag = ((lax.broadcasted_iota(jnp.int32, (LANES, LANES), 0) < half)
                 == (lax.broadcasted_iota(jnp.int32, (LANES, LANES), 1) < half))

    for b in range(bt):
        def intra(c, carry):
            rows = pl.ds(pl.multiple_of(c * cs, cs), cs)
            qc, kc, vc = q_ref[b, rows, :], k_ref[b, rows, :], v_ref[b, rows, :]
            for p in range(n_pairs):
                sl = slice(p * LANES, (p + 1) * LANES)
                k128, vb = kc[:, sl], vc[:, sl]
                inner = (_dot_nt(_split_halves(qc[:, sl]), k128.astype(BF16)) * d_scr[p]).astype(BF16)
                oo = _dot(inner, vb)
                o_scr[rows, sl] = jnp.where(lo, oo[:cs], oo[cs:])
                kk = jnp.concatenate([k128 * dec_scr[1, p], k128 * dec_scr[3, p]], axis=1).astype(BF16)
                kv = _dot_tn(kk, vb)
                kv_scr[0, c, p] = jnp.where(blockdiag, kv[:LANES], 0.0)
                kv_scr[1, c, p] = jnp.where(blockdiag, kv[LANES:], 0.0)
            return carry

        lax.fori_loop(0, n_chunks, intra, 0, unroll=min(n_chunks, RET_UNROLL))

        for p in range(n_pairs):
            for d in range(2):
                if has_state:
                    r0_ref = (r0f_ref, r0b_ref)[d]
                    zero = jnp.zeros((half, half), F32)
                    top = jnp.concatenate([r0_ref[b, 2 * p], zero], axis=1)
                    bot = jnp.concatenate([zero, r0_ref[b, 2 * p + 1]], axis=1)
                    r = jnp.concatenate([top, bot], axis=0)
                else:
                    r = jnp.zeros((LANES, LANES), F32)
                order = range(n_chunks) if d == 0 else range(n_chunks - 1, -1, -1)
                for c in order:
                    rs_scr[c, p, :, d * LANES:(d + 1) * LANES] = r.astype(BF16)
                    r = r * dec_scr[4 + d, p, :LANES, :] + kv_scr[d, c, p]
                if state_outs is not None:
                    state_outs[d][b, n_prev, 2 * p] = r[:half, :half]
                    state_outs[d][b, n_prev, 2 * p + 1] = r[half:, half:]

        def cross(c, carry):
            rows = pl.ds(pl.multiple_of(c * cs, cs), cs)
            qc = q_ref[b, rows, :]
            outs = []
            for p in range(n_pairs):
                sl = slice(p * LANES, (p + 1) * LANES)
                t = _dot(qc[:, sl], rs_scr[c, p])
                outs.append(o_scr[rows, sl] + t[:, :LANES] * dec_scr[0, p] + t[:, LANES:] * dec_scr[2, p])
            o = _group64_rmsnorm(jnp.concatenate(outs, axis=-1), norm_ref[...])
            y_ref[b, rows, :] = (gate_ref[b, rows, :] * o).astype(BF16)
            return carry

        lax.fori_loop(0, n_chunks, cross, 0, unroll=min(n_chunks, RET_UNROLL))


def _mixer_b(q, k, v, gate, layer, lg_pair, lg_head, norm, state, prev_states):
    b, l, wb = q.shape
    bt = max(1, RET_ROWS // l)
    chunk = min(l, RET_CHUNK)
    n_chunks = l // chunk
    has_state = state is not None
    tile = pl.BlockSpec((bt, l, wb), lambda i: (i, 0, 0))
    in_specs = [tile, tile, tile, tile]
    args = [q, k, v, gate]
    out_specs = [tile]
    out_shape = [jax.ShapeDtypeStruct((b, l, wb), BF16)]
    s_tail = (H_B, DK_B, DK_B)
    if has_state:
        sspec = pl.BlockSpec((bt, None) + s_tail, lambda i: (i, layer, 0, 0, 0))
        in_specs += [sspec, sspec]
        args += list(state)
    else:
        if layer:
            in_specs += [_stacked_spec(bt, layer, s_tail)] * 2
            args += list(prev_states)
        out_specs += [_stacked_spec(bt, layer + 1, s_tail)] * 2
        out_shape += [jax.ShapeDtypeStruct((b, layer + 1) + s_tail, F32)] * 2
    params = [lg_pair, lg_head, norm]
    in_specs += [_layer_spec(p, layer) for p in params]
    args += params
    n_pairs = H_B // 2
    return pl.pallas_call(
        functools.partial(_ret_kernel, chunk=chunk, n_chunks=n_chunks, has_state=has_state,
                          n_prev=0 if has_state else layer),
        grid=(b // bt,),
        in_specs=in_specs,
        out_specs=out_specs,
        out_shape=out_shape,
        scratch_shapes=[
            pltpu.VMEM((n_pairs, 2 * chunk, chunk), F32),
            pltpu.VMEM((6, n_pairs, chunk, LANES), F32),
            pltpu.VMEM((2, n_chunks, n_pairs, LANES, LANES), F32),
            pltpu.VMEM((n_chunks, n_pairs, LANES, 2 * LANES), BF16),
            pltpu.VMEM((l, wb), F32),
        ],
        compiler_params=_cparams(("arbitrary",)),
        name="retention",
    )(*args)


def _diff_softmax_t(st, lam, tq):
    n_keys = st.shape[0]
    st4 = st.reshape(n_keys // ATTN_KEY_BLOCK, ATTN_KEY_BLOCK // SUBLANES, SUBLANES, 2 * tq)
    m_loc = jnp.max(st4, axis=1)
    e4 = jnp.exp2(st4 - m_loc[:, None])
    l_loc = jnp.sum(e4, axis=1)
    m = jnp.max(jnp.max(m_loc, axis=0), axis=0, keepdims=True)
    s_loc = jnp.exp2(m_loc - m)
    l = jnp.sum(jnp.sum(l_loc * s_loc, axis=0), axis=0, keepdims=True)
    coef = jnp.concatenate([1.0 / l[:, :tq], lam / l[:, tq:]], axis=1)
    w = e4 * (coef * s_loc)[:, None]
    return (w[..., :tq] - w[..., tq:]).reshape(n_keys, tq).astype(BF16)


def _attn_kernel(*refs, has_cache, lam_init):
    if has_cache:
        (q_ref, k_ref, v_ref, ck_ref, cv_ref, dl_ref, dn_ref, y_ref, k_scr, v_scr) = refs
        n_past = ck_ref.shape[1]

        @pl.when(pl.program_id(1) == 0)
        def _gather_keys():
            k_scr[...] = ck_ref[...].T.astype(BF16)
            cv = jnp.swapaxes(cv_ref[...], 0, 1)
            for h in range(H_C):
                v_scr[:, h * LANES:(h + 1) * LANES] = cv[h].astype(BF16)

        keys = lambda b, sl: jnp.concatenate([k_scr[:, sl], k_ref[b, :, sl]], axis=0)
        vals = lambda b, sl: jnp.concatenate([v_scr[:, sl], v_ref[b, :, sl]], axis=0)
    else:
        (q_ref, k_ref, v_ref, dl_ref, dn_ref, y_ref) = refs
        keys = lambda b, sl: k_ref[b, :, sl]
        vals = lambda b, sl: v_ref[b, :, sl]
    bt, tq, _ = q_ref.shape
    dl = dl_ref[...]
    lam = (jnp.exp(jnp.sum(dl[0:1] * dl[1:2], axis=-1, keepdims=True))
           - jnp.exp(jnp.sum(dl[2:3] * dl[3:4], axis=-1, keepdims=True)) + lam_init)
    units = [(b, slice(h * LANES, (h + 1) * LANES)) for b in range(bt) for h in range(H_C)]
    scores = lambda b, sl: _dot_nt(keys(b, sl), _split_halves(q_ref[b, :, sl]))
    st_next = scores(*units[0])
    for i, (b, sl) in enumerate(units):
        st = st_next
        if i + 1 < len(units):
            st_next = scores(*units[i + 1])
        o = _dot_tn(_diff_softmax_t(st, lam, tq), vals(b, sl))
        y_ref[b, :, sl] = (_rmsnorm(o, dn_ref[...]) * (1.0 - lam_init)).astype(BF16)


def _mixer_c(q, k, v, layer, dl, dn, lam_init, cache):
    b, l, wc = q.shape
    has_cache = cache is not None
    tq = min(l, ATTN_Q_ROWS)
    bt = 1 if has_cache else max(1, ATTN_SEQ_ROWS // l)
    q_spec = pl.BlockSpec((bt, tq, wc), lambda i, j: (i, j, 0))
    kv_spec = pl.BlockSpec((bt, l, wc), lambda i, j: (i, 0, 0))
    in_specs = [q_spec, kv_spec, kv_spec]
    args = [q, k, v]
    n_keys = l
    if has_cache:
        ck_t, cv = cache
        n_past = cv.shape[2]
        n_keys += n_past
        in_specs += [pl.BlockSpec((None, None, wc, n_past), lambda i, j: (i, layer, 0, 0)),
                     pl.BlockSpec((None, None, n_past, H_C, LANES), lambda i, j: (i, layer, 0, 0, 0))]
        args += [ck_t, cv]
    in_specs += [_layer_spec(dl, layer), _layer_spec(dn, layer)]
    args += [dl, dn]
    scratch = [pltpu.VMEM((n_past, wc), BF16), pltpu.VMEM((n_past, wc), BF16)] if has_cache else []
    return pl.pallas_call(
        functools.partial(_attn_kernel, has_cache=has_cache, lam_init=lam_init),
        grid=(b // bt, l // tq),
        in_specs=in_specs,
        out_specs=q_spec,
        out_shape=jax.ShapeDtypeStruct((b, l, wc), BF16),
        scratch_shapes=scratch,
        compiler_params=_cparams(("parallel", "arbitrary")),
        name="diff_attention",
    )(*args)


def _mlp_kernel(ya_ref, yb_ref, yc_ref, x_ref, mod_ref, g_ref, wo_ref, up_ref, cw_ref, cb_ref, down_ref,
                o_ref, h_scr, g_scr, *, seq_len):
    rows = x_ref.shape[0]
    m = mod_ref[0]
    y = None
    k0 = 0
    for y_ref in (ya_ref, yb_ref, yc_ref):
        kw = y_ref.shape[1]
        part = _dot(y_ref[...], wo_ref[k0:k0 + kw, :])
        y = part if y is None else y + part
        k0 += kw
    x1 = x_ref[...] + m[2:3, :] * y
    o_ref[...] = x1
    h_scr[...] = (_rmsnorm(x1, g_ref[...]) * (1.0 + m[4:5, :]) + m[3:4, :]).astype(BF16)

    d_ff = down_ref.shape[0]
    fc = MXU_DIM
    pos = lax.broadcasted_iota(jnp.int32, (rows, fc), 0) % seq_len
    has_prev = pos > 0
    has_next = pos < seq_len - 1

    def conv(u, c0):
        w = cw_ref[:, c0:c0 + LANES]
        prev = jnp.where(has_prev[:, :LANES], pltpu.roll(u, 1, 0), 0.0)
        nxt = jnp.where(has_next[:, :LANES], pltpu.roll(u, rows - 1, 0), 0.0)
        return prev * w[0:1, :] + u * w[1:2, :] + nxt * w[2:3, :] + cb_ref[:, c0:c0 + LANES]

    for j in range(d_ff // fc):
        ua = _dot(h_scr[...], up_ref[:, j * fc:(j + 1) * fc])
        ub = _dot(h_scr[...], up_ref[:, d_ff + j * fc:d_ff + (j + 1) * fc])
        for c in range(0, fc, LANES):
            a = conv(ua[:, c:c + LANES], j * fc + c)
            b = conv(ub[:, c:c + LANES], d_ff + j * fc + c)
            g_scr[:, j * fc + c:j * fc + c + LANES] = (_silu(a) * b).astype(BF16)

    o_ref[...] = o_ref[...] + m[5:6, :] * _dot(g_scr[...], down_ref[...])


def _mlp(ya, yb, yc, x, mods, mod_row, layer, seq_len, g, w_out, up, cw, cb, down):
    t, d = x.shape
    rows = MLP_ROWS
    tiles_per_seq = max(1, seq_len // rows)
    row_spec = lambda a: pl.BlockSpec((rows, a.shape[1]), lambda i: (i, 0))
    weights = [w_out, up, cw, cb, down]
    return pl.pallas_call(
        functools.partial(_mlp_kernel, seq_len=seq_len),
        grid=(t // rows,),
        in_specs=[row_spec(ya), row_spec(yb), row_spec(yc), row_spec(x),
                  _mod_spec(mods, layer, lambda i: mod_row(i // tiles_per_seq)), _layer_spec(g, layer)]
                 + [_layer_spec(w, layer, resident=True) for w in weights],
        out_specs=row_spec(x),
        out_shape=jax.ShapeDtypeStruct((t, d), F32),
        scratch_shapes=[pltpu.VMEM((rows, d), BF16), pltpu.VMEM((rows, down.shape[1]), BF16)],
        compiler_params=_cparams(("parallel",)),
        name="out_proj_mlp",
    )(ya, yb, yc, x, mods, g, *weights)


def _rope_tables(l):
    rows = l // GRID_W
    t_row = jnp.repeat(jnp.arange(rows, dtype=F32), GRID_W)
    t_col = jnp.tile(jnp.arange(GRID_W, dtype=F32), rows)
    inv = ROPE_BASE ** (-jnp.arange(ROPE_PAIRS, dtype=F32) / ROPE_PAIRS)
    ar, ac = t_row[:, None] * inv, t_col[:, None] * inv
    cos = jnp.concatenate([jnp.cos(ar), jnp.cos(ar), jnp.cos(ac), jnp.cos(ac)], axis=-1)
    sin = jnp.concatenate([-jnp.sin(ar), jnp.sin(ar), -jnp.sin(ac), jnp.sin(ac)], axis=-1)
    return jnp.tile(cos, (1, 2)), jnp.tile(sin, (1, 2))


def kernel(x_prompt, x_sample, c, cache_k, cache_v, state_ret_fwd, state_ret_bwd, c_ctx, norm1, w_mod, b_mod,
           w_in, sgu_norm, sgu_w, sgu_b, ret_logit_fwd, ret_logit_bwd, ret_norm, q_norm, k_norm, diff_lam,
           diff_norm, w_out, norm2, ffn_up, ffn_conv, ffn_conv_b, ffn_down):
    depth, d_model, _ = w_mod.shape
    batch, seq, _ = x_prompt.shape
    dec_batch, dec_seq, _ = x_sample.shape
    w_a = sgu_norm.shape[1]
    w_b = H_B * ret_norm.shape[2]
    w_c = 2 * H_C * HD_C
    past = cache_k.shape[2]

    pad = (-(dec_batch + 1)) % 8
    cond = jnp.concatenate([c, c_ctx[None, :], jnp.zeros((pad, d_model), F32)], axis=0)
    mods = _modulation(cond, w_mod, b_mod).reshape(depth, cond.shape[0], 6, d_model)

    w_in_b = w_in.astype(BF16)
    w_out_b = w_out.astype(BF16)
    sgu_w_b = sgu_w.astype(BF16)
    up_b = ffn_up.astype(BF16)
    down_b = ffn_down.astype(BF16)
    row3 = lambda a: a.reshape(depth, 1, -1)
    sgu_bias = jnp.repeat(jnp.swapaxes(sgu_b, 1, 2), w_a // G_A, axis=2)
    logits = jnp.stack([ret_logit_fwd, ret_logit_bwd], axis=1)
    lg_pair = jnp.repeat(logits, DK_B, axis=2).reshape(depth, 2 * (H_B // 2), LANES)
    lg_head = logits.reshape(depth, 2 * H_B, 1)
    qg = row3(jnp.tile(q_norm, (1, 2 * H_C)))
    kg = row3(jnp.tile(k_norm, (1, 2 * H_C)))
    norm1_r, norm2_r, sgu_g, ret_g, diff_g = row3(norm1), row3(norm2), row3(sgu_norm), row3(ret_norm), row3(diff_norm)
    conv_b = row3(ffn_conv_b)
    rope = _rope_tables(dec_seq)
    cache_kt = jnp.transpose(cache_k, (0, 1, 3, 4, 5, 2)).reshape(dec_batch, depth, w_c, past)

    def layer(x, l, nb, sl, mod_row, is_sample, new_cache, new_states):
        lam_init = 0.8 - 0.6 * math.exp(-0.3 * l)
        outs = _projection(x, mods, mod_row, l, sl, norm1_r, w_in_b, sgu_g, sgu_w_b, sgu_bias, qg, kg,
                           rope if is_sample else None, new_cache)
        ya, qr, kr, vr, gr, qa, ka, va = outs[:8]
        seq3 = lambda a: a.reshape(nb, sl, a.shape[1])
        state = (state_ret_fwd, state_ret_bwd) if is_sample else None
        yb, *states = _mixer_b(seq3(qr), seq3(kr), seq3(vr), seq3(gr), l, lg_pair, lg_head, ret_g, state,
                               new_states)
        cache = (cache_kt, cache_v) if is_sample else None
        yc = _mixer_c(seq3(qa), seq3(ka), seq3(va), l, diff_lam, diff_g, lam_init, cache)
        x = _mlp(ya, yb.reshape(nb * sl, w_b), yc.reshape(nb * sl, w_c), x, mods, mod_row, l, sl, norm2_r,
                 w_out_b, up_b, ffn_conv, conv_b, down_b)
        return x, outs[8:], states

    y_prompt = x_prompt.reshape(batch * seq, d_model)
    new_cache, new_states = None, None
    for l in range(depth):
        y_prompt, new_cache, new_states = layer(y_prompt, l, batch, seq, lambda b: dec_batch, False,
                                                new_cache, new_states)

    y_sample = x_sample.reshape(dec_batch * dec_seq, d_model)
    for l in range(depth):
        y_sample, _, _ = layer(y_sample, l, dec_batch, dec_seq, lambda b: b, True, None, None)

    new_kt, new_v = new_cache
    new_k = jnp.transpose(new_kt.reshape(batch, depth, H_C, 2, HD_C, seq), (0, 1, 5, 2, 3, 4))
    return (y_prompt.reshape(batch, seq, d_model), y_sample.reshape(dec_batch, dec_seq, d_model),
            new_k, new_v, new_states[0], new_states[1])
```
